```python
import math
import jax, jax.numpy as jnp
from jax import lax
import numpy as np

D_MODEL = 1024
BATCH = 1
SEQ = 16384
DEPTH = 1

D_MIX = D_MODEL
GDN_HEADS = 4
GDN_DK = 128
GDN_DV = 128
GDN_CHUNK = 64
CONV_K = 4
A_LOG_MIN = 1.0
A_LOG_MAX = 8.0
DIFF_HEADS = 4
DIFF_D = 64
ROPE_THETA = 10000.0
Q_BLOCK = 128
LAMBDA_STD = 0.1
EPS = 1e-6
SUBLN_EPS = 1e-5

GDN_QK = GDN_HEADS * GDN_DK
GDN_V = GDN_HEADS * GDN_DV
DIFF_QK = DIFF_HEADS * 2 * DIFF_D
DIFF_V = DIFF_HEADS * 2 * DIFF_D
SPLIT_SIZES = (GDN_QK, GDN_QK, GDN_V, GDN_V, GDN_HEADS, GDN_HEADS,
               DIFF_QK, DIFF_QK, DIFF_V, DIFF_V)
D_IN_PROJ = sum(SPLIT_SIZES)
CONV_CH = 2 * GDN_QK + GDN_V

kernel_name = "hybrid_gdn_diffattn_parallel_heads"


def rms_norm(x, w, eps=EPS):
    x = x.astype(jnp.float32)
    return x * lax.rsqrt(jnp.mean(x * x, axis=-1, keepdims=True) + eps) * w.astype(jnp.float32)


def l2_norm(x, eps=EPS):
    return x * lax.rsqrt(jnp.sum(x * x, axis=-1, keepdims=True) + eps)


def rope_tables(seq, dim):
    inv_freq = ROPE_THETA ** (-jnp.arange(0, dim, 2, dtype=jnp.float32) / dim)
    ang = jnp.arange(seq, dtype=jnp.float32)[:, None] * inv_freq[None, :]
    return jnp.cos(ang), jnp.sin(ang)


def apply_rope(x, cos, sin):
    c = cos[None, :, None, None, :]
    s = sin[None, :, None, None, :]
    x1, x2 = jnp.split(x, 2, axis=-1)
    return jnp.concatenate([x1 * c - x2 * s, x2 * c + x1 * s], axis=-1)


def causal_depthwise_conv(x, w):
    c = x.shape[-1]
    return lax.conv_general_dilated(
        x, w[:, None, :], window_strides=(1,), padding=[(CONV_K - 1, 0)],
        dimension_numbers=("NWC", "WIO", "NWC"), feature_group_count=c)


def gated_delta_rule_chunked(q, k, v, g, beta):
    b, h, s, dk = q.shape
    dv = v.shape[-1]
    c = GDN_CHUNK
    n = s // c
    q = q.reshape(b, h, n, c, dk)
    k = k.reshape(b, h, n, c, dk)
    v = v.reshape(b, h, n, c, dv)
    beta = beta.reshape(b, h, n, c)
    g = jnp.cumsum(g.reshape(b, h, n, c), axis=-1)
    tril = jnp.tril(jnp.ones((c, c), dtype=bool))
    strict = jnp.tril(jnp.ones((c, c), dtype=bool), -1)
    decay = jnp.exp(jnp.where(tril, g[..., :, None] - g[..., None, :], -jnp.inf))
    k_beta = k * beta[..., None]
    v_beta = v * beta[..., None]
    lower = jnp.where(strict, jnp.einsum("bhncd,bhnmd->bhncm", k_beta, k) * decay, 0.0)
    a_mat = lower + jnp.eye(c, dtype=q.dtype)
    rhs = jnp.concatenate([v_beta, k_beta * jnp.exp(g)[..., None]], axis=-1)
    sol = lax.linalg.triangular_solve(a_mat, rhs, left_side=True, lower=True, unit_diagonal=True)
    u, w = sol[..., :dv], sol[..., dv:]
    qk_intra = jnp.where(tril, jnp.einsum("bhncd,bhnmd->bhncm", q, k) * decay, 0.0)

    def step(state, inp):
        q_c, k_c, u_c, w_c, g_c, a_c = inp
        v_new = u_c - jnp.einsum("bhcd,bhde->bhce", w_c, state)
        o = (jnp.einsum("bhcd,bhde->bhce", q_c * jnp.exp(g_c)[..., None], state)
             + jnp.einsum("bhcm,bhme->bhce", a_c, v_new))
        g_last = g_c[..., -1]
        k_dec = k_c * jnp.exp(g_last[..., None] - g_c)[..., None]
        state = state * jnp.exp(g_last)[..., None, None] + jnp.einsum("bhcd,bhce->bhde", k_dec, v_new)
        return state, o

    xs = tuple(jnp.moveaxis(t, 2, 0) for t in (q, k, u, w, g, qk_intra))
    state0 = jnp.zeros((b, h, dk, dv), dtype=q.dtype)
    _, o = lax.scan(step, state0, xs)
    return jnp.moveaxis(o, 0, 2).reshape(b, h, s, dv)


def diff_attention(q, k, v, lam):
    b, s, h, _, d = q.shape
    nblk = s // Q_BLOCK
    scale = 1.0 / math.sqrt(d)
    qb = jnp.moveaxis(q.reshape(b, nblk, Q_BLOCK, h, 2, d), 1, 0)
    key_pos = jnp.arange(s)

    def one_block(args):
        i, q_i = args
        sc = jnp.einsum("bqhcd,bkhcd->bhcqk", q_i, k) * scale
        q_pos = i * Q_BLOCK + jnp.arange(Q_BLOCK)
        mask = key_pos[None, :] <= q_pos[:, None]
        p = jax.nn.softmax(jnp.where(mask, sc, -jnp.inf), axis=-1)
        p = p[:, :, 0] - lam * p[:, :, 1]
        return jnp.einsum("bhqk,bkhe->bqhe", p, v)

    o = lax.map(one_block, (jnp.arange(nblk), qb))
    return jnp.moveaxis(o, 0, 1).reshape(b, s, h, 2 * d)


def hybrid_mixer(layer_idx, x, cos, sin, w_norm, w_in, conv_w, a_log, dt_bias, gdn_norm_w,
                 q_norm_w, k_norm_w, lambda_q1, lambda_k1, lambda_q2, lambda_k2, subln_w, w_out):
    b, s, _ = x.shape
    f32 = jnp.float32
    hn = rms_norm(x, w_norm)
    proj = jnp.einsum("bsd,de->bse", hn, w_in.astype(f32))
    idx = [int(t) for t in np.cumsum(SPLIT_SIZES)[:-1]]
    q_a, k_a, v_a, z_a, b_a, a_a, q_b, k_b, v_b, z_b = jnp.split(proj, idx, axis=-1)

    qkv = jax.nn.silu(causal_depthwise_conv(jnp.concatenate([q_a, k_a, v_a], -1), conv_w.astype(f32)))
    q_a, k_a, v_a = jnp.split(qkv, [GDN_QK, 2 * GDN_QK], axis=-1)
    to_heads = lambda t, dh: jnp.transpose(t.reshape(b, s, GDN_HEADS, dh), (0, 2, 1, 3))
    q_a = l2_norm(to_heads(q_a, GDN_DK)) * (GDN_DK ** -0.5)
    k_a = l2_norm(to_heads(k_a, GDN_DK))
    v_a = to_heads(v_a, GDN_DV)
    beta = jnp.transpose(jax.nn.sigmoid(b_a), (0, 2, 1))
    g = -jnp.exp(a_log.astype(f32)) * jax.nn.softplus(a_a + dt_bias.astype(f32))
    g = jnp.transpose(g, (0, 2, 1))
    o_a = gated_delta_rule_chunked(q_a, k_a, v_a, g, beta)
    o_a = jnp.transpose(o_a, (0, 2, 1, 3))
    o_a = rms_norm(o_a, gdn_norm_w) * jax.nn.silu(z_a.reshape(b, s, GDN_HEADS, GDN_DV))
    o_a = o_a.reshape(b, s, GDN_V)

    q_b = apply_rope(rms_norm(q_b.reshape(b, s, DIFF_HEADS, 2, DIFF_D), q_norm_w), cos, sin)
    k_b = apply_rope(rms_norm(k_b.reshape(b, s, DIFF_HEADS, 2, DIFF_D), k_norm_w), cos, sin)
    v_b = v_b.reshape(b, s, DIFF_HEADS, 2 * DIFF_D)
    lam_init = 0.8 - 0.6 * math.exp(-0.3 * layer_idx)
    lam = (jnp.exp(jnp.sum(lambda_q1.astype(f32) * lambda_k1.astype(f32)))
           - jnp.exp(jnp.sum(lambda_q2.astype(f32) * lambda_k2.astype(f32))) + lam_init)
    o_b = diff_attention(q_b, k_b, v_b, lam)
    o_b = rms_norm(o_b, subln_w, SUBLN_EPS) * (1.0 - lam_init)
    o_b = (o_b * jax.nn.silu(z_b.reshape(b, s, DIFF_HEADS, 2 * DIFF_D))).reshape(b, s, DIFF_V)

    o = jnp.concatenate([o_a, o_b], axis=-1).astype(x.dtype)
    return jnp.einsum("bse,ed->bsd", o, w_out)


def setup_inputs(seed: int = 0) -> dict:
    key = jax.random.key(seed)
    ks = jax.random.split(key, 16)
    nrm = lambda k, shape, std: jax.random.normal(k, shape, jnp.float32) * std
    return {
        "x": jax.random.normal(ks[0], (BATCH, SEQ, D_MODEL), jnp.float32),
        "w_norm": 1.0 + nrm(ks[1], (DEPTH, D_MODEL), 0.02),
        "w_in": nrm(ks[2], (DEPTH, D_MODEL, D_IN_PROJ), D_MODEL ** -0.5),
        "conv_w": nrm(ks[3], (DEPTH, CONV_K, CONV_CH), CONV_K ** -0.5),
        "a_log": jnp.log(jax.random.uniform(ks[4], (DEPTH, GDN_HEADS), jnp.float32, A_LOG_MIN, A_LOG_MAX)),
        "dt_bias": nrm(ks[5], (DEPTH, GDN_HEADS), 0.1),
        "gdn_norm_w": 1.0 + nrm(ks[6], (DEPTH, GDN_DV), 0.02),
        "q_norm_w": 1.0 + nrm(ks[7], (DEPTH, DIFF_D), 0.02),
        "k_norm_w": 1.0 + nrm(ks[8], (DEPTH, DIFF_D), 0.02),
        "lambda_q1": nrm(ks[9], (DEPTH, DIFF_D), LAMBDA_STD),
        "lambda_k1": nrm(ks[10], (DEPTH, DIFF_D), LAMBDA_STD),
        "lambda_q2": nrm(ks[11], (DEPTH, DIFF_D), LAMBDA_STD),
        "lambda_k2": nrm(ks[12], (DEPTH, DIFF_D), LAMBDA_STD),
        "subln_w": 1.0 + nrm(ks[13], (DEPTH, 2 * DIFF_D), 0.02),
        "w_out": nrm(ks[14], (DEPTH, D_MIX, D_MODEL), D_MIX ** -0.5),
    }


def reference(x, w_norm, w_in, conv_w, a_log, dt_bias, gdn_norm_w, q_norm_w, k_norm_w,
              lambda_q1, lambda_k1, lambda_q2, lambda_k2, subln_w, w_out):
    cos, sin = rope_tables(x.shape[1], DIFF_D)
    for l in range(DEPTH):
        x = x + hybrid_mixer(l, x, cos, sin, w_norm[l], w_in[l], conv_w[l], a_log[l], dt_bias[l],
                             gdn_norm_w[l], q_norm_w[l], k_norm_w[l], lambda_q1[l], lambda_k1[l],
                             lambda_q2[l], lambda_k2[l], subln_w[l], w_out[l])
    return x
```

```python
import functools
import math

import jax
import jax.numpy as jnp
from jax import lax
from jax.experimental import pallas as pl
from jax.experimental.pallas import tpu as pltpu

F32 = jnp.float32
BF16 = jnp.bfloat16

GDN_HEADS = 4
GDN_D = 128
CONV_K = 4
DIFF_HEADS = 4
DIFF_D = 64
ROPE_THETA = 10000.0
EPS = 1e-6
SUBLN_EPS = 1e-5

LANES = 128
CHUNK = LANES
NEG_BIG = -1e30

PROJ_TM = 512
GDN_TB = 256
ATT_TQ = 512
ATT_TK = 512
OUT_TM = 512
ONES_ROWS = 8
VMEM_LIMIT = 56 * 1024 * 1024

NT_DIMS = (((1,), (1,)), ((), ()))
TN_DIMS = (((0,), (0,)), ((), ()))


def _sigmoid(v):
    return 1.0 / (1.0 + jnp.exp(-v))


def _silu(v):
    return v * _sigmoid(v)


def _proj_kernel(x_ref, wn_ref, wt_ref, wba_ref, cos_ref, sin_ref, qw_ref, kw_ref,
                 gdn_ref, ba_ref, qb_ref, kb_ref, vb_ref, zb_ref):
    x = x_ref[...]
    ms = jnp.mean(x * x, axis=-1, keepdims=True)
    hn = (x * lax.rsqrt(ms + EPS) * wn_ref[...]).astype(BF16)

    def proj_t(lo, n):
        return lax.dot_general(wt_ref[lo:lo + n, :], hn, NT_DIMS, preferred_element_type=F32)

    blk = 512
    for b in range(4):
        gdn_ref[b * blk:(b + 1) * blk, :] = proj_t(b * blk, blk).astype(BF16)
    ba_ref[...] = lax.dot_general(wba_ref[...], hn, NT_DIMS, preferred_element_type=F32)

    cos = cos_ref[...]
    sin = sin_ref[...]
    half = DIFF_D // 2

    def norm_rope(t, w, scale):
        outs = []
        for g in range(t.shape[0] // DIFF_D):
            tg = t[g * DIFF_D:(g + 1) * DIFF_D, :]
            tg = tg * lax.rsqrt(jnp.mean(tg * tg, axis=0, keepdims=True) + EPS) * w
            t1, t2 = tg[:half, :], tg[half:, :]
            outs.append((t1 * cos - t2 * sin) * scale)
            outs.append((t2 * cos + t1 * sin) * scale)
        return jnp.concatenate(outs, axis=0)

    qb_ref[...] = norm_rope(proj_t(4 * blk, blk), qw_ref[...], 1.0 / math.sqrt(DIFF_D)).astype(BF16)
    kt = norm_rope(proj_t(5 * blk, blk), kw_ref[...], 1.0)
    kb_ref[...] = kt.T.astype(BF16)
    vb_ref[...] = proj_t(6 * blk, blk).astype(BF16)
    zb_ref[...] = proj_t(7 * blk, blk).astype(BF16)


def _proj_call(x2, w_norm, wt, wba, cos_t, sin_t, qw, kw):
    s, d = x2.shape
    tm = PROJ_TM
    const = lambda shape: pl.BlockSpec(shape, lambda i: (0,) * len(shape))
    cols = lambda rows: pl.BlockSpec((rows, tm), lambda i: (0, i))
    return pl.pallas_call(
        _proj_kernel,
        grid=(s // tm,),
        in_specs=[
            pl.BlockSpec((tm, d), lambda i: (i, 0)),
            const((1, d)),
            const(wt.shape),
            const(wba.shape),
            cols(DIFF_D // 2),
            cols(DIFF_D // 2),
            const((DIFF_D, tm)),
            const((DIFF_D, tm)),
        ],
        out_specs=[
            cols(2048),
            cols(8),
            cols(512),
            pl.BlockSpec((tm, 512), lambda i: (i, 0)),
            cols(512),
            cols(512),
        ],
        out_shape=[
            jax.ShapeDtypeStruct((2048, s), BF16),
            jax.ShapeDtypeStruct((8, s), F32),
            jax.ShapeDtypeStruct((512, s), BF16),
            jax.ShapeDtypeStruct((s, 512), BF16),
            jax.ShapeDtypeStruct((512, s), BF16),
            jax.ShapeDtypeStruct((512, s), BF16),
        ],
        compiler_params=pltpu.CompilerParams(
            dimension_semantics=("arbitrary",), vmem_limit_bytes=VMEM_LIMIT),
        name="proj",
    )(x2, w_norm, wt, wba, cos_t, sin_t, qw, kw)


def _gdn_kernel(g_ref, ba_ref, cw_ref, alog_ref, dt_ref, gw_ref, o_ref, halo_ref, st_ref):
    nqkv = 3 * GDN_HEADS * GDN_D
    n_chunks = g_ref.shape[1] // CHUNK

    @pl.when(pl.program_id(0) == 0)
    def _():
        halo_ref[...] = jnp.zeros_like(halo_ref)
        st_ref[...] = jnp.zeros_like(st_ref)

    lane = lax.broadcasted_iota(jnp.int32, (CHUNK, CHUNK), 1)
    subl = lax.broadcasted_iota(jnp.int32, (CHUNK, CHUNK), 0)
    eye = (lane == subl).astype(F32)
    lane8 = lax.broadcasted_iota(jnp.int32, (8, CHUNK), 1)
    lane_c = lax.broadcasted_iota(jnp.int32, (nqkv, CHUNK), 1)

    def rolls(v):
        return [pltpu.roll(v, j, axis=1) for j in range(1, CONV_K)]

    prev_rolls = rolls(halo_ref[...])
    for c in range(n_chunks):
        lanes = slice(c * CHUNK, (c + 1) * CHUNK)
        cur = g_ref[0:nqkv, lanes].astype(F32)
        cur_rolls = rolls(cur)
        if c == n_chunks - 1:
            halo_ref[...] = cur
        y = cw_ref[CONV_K - 1] * cur
        for j in range(1, CONV_K):
            shifted = jnp.where(lane_c >= j, cur_rolls[j - 1], prev_rolls[j - 1])
            y = y + cw_ref[CONV_K - 1 - j] * shifted
        prev_rolls = cur_rolls
        y = _silu(y)

        ba = ba_ref[:, lanes]
        beta8 = _sigmoid(ba)
        sp = ba + dt_ref[...]
        softplus = jnp.maximum(sp, 0.0) + jnp.log(1.0 + jnp.exp(-jnp.abs(sp)))
        gc8 = -jnp.exp(alog_ref[...]) * softplus
        sh = 1
        while sh < CHUNK:
            gc8 = gc8 + jnp.where(lane8 >= sh, pltpu.roll(gc8, sh, axis=1), 0.0)
            sh *= 2

        for h in range(GDN_HEADS):
            rows = slice(h * GDN_D, (h + 1) * GDN_D)
            qt = y[h * GDN_D:(h + 1) * GDN_D, :]
            kt = y[(GDN_HEADS + h) * GDN_D:(GDN_HEADS + h + 1) * GDN_D, :]
            vt = y[(2 * GDN_HEADS + h) * GDN_D:(2 * GDN_HEADS + h + 1) * GDN_D, :]
            qt = qt * lax.rsqrt(jnp.sum(qt * qt, axis=0, keepdims=True) + EPS) * (GDN_D ** -0.5)
            kt = kt * lax.rsqrt(jnp.sum(kt * kt, axis=0, keepdims=True) + EPS)
            beta = beta8[h:h + 1, :]
            gc = gc8[GDN_HEADS + h:GDN_HEADS + h + 1, :]

            g_row = jnp.broadcast_to(gc, (CHUNK, CHUNK))
            g_col = g_row.T
            g_last = g_col[CHUNK - 1:CHUNK, :]
            decay_t = jnp.exp(jnp.where(lane >= subl, g_row - g_col, NEG_BIG))

            k_nat = kt.T
            gram = jnp.dot(k_nat.astype(BF16), jnp.concatenate([kt, qt], axis=1).astype(BF16),
                           preferred_element_type=F32)
            l_t = jnp.where(lane > subl, gram[:, :CHUNK] * decay_t * beta, 0.0)
            a_t = gram[:, CHUNK:] * decay_t

            n_mat = -l_t
            p_mat = eye + n_mat
            m = 2
            while m < CHUNK:
                nb = n_mat.astype(BF16)
                n_mat = jnp.dot(nb, nb, preferred_element_type=F32)
                p_mat = p_mat + jnp.dot(p_mat.astype(BF16), n_mat.astype(BF16),
                                        preferred_element_type=F32)
                m *= 2

            e_gc = jnp.exp(gc)
            rhs = jnp.concatenate([vt * beta, kt * (beta * e_gc)], axis=0)
            uw = jnp.dot(rhs.astype(BF16), p_mat.astype(BF16), preferred_element_type=F32)
            u_t, w_t = uw[:GDN_D, :], uw[GDN_D:, :]
            r1 = jnp.concatenate([w_t, qt * e_gc], axis=1).astype(BF16)
            k_dec = k_nat * jnp.exp(g_last - g_col)
            r2 = jnp.concatenate([k_dec, a_t], axis=1).astype(BF16)

            s_t = st_ref[h]
            x1 = jnp.dot(s_t.astype(BF16), r1, preferred_element_type=F32)
            vn_t = u_t - x1[:, :CHUNK]
            x2 = jnp.dot(vn_t.astype(BF16), r2, preferred_element_type=F32)
            o_t = x1[:, CHUNK:] + x2[:, GDN_D:]
            st_ref[h] = s_t * jnp.exp(g_last) + x2[:, :GDN_D]

            o_n = o_t * lax.rsqrt(jnp.mean(o_t * o_t, axis=0, keepdims=True) + EPS) * gw_ref[...]
            z_t = g_ref[nqkv + h * GDN_D:nqkv + (h + 1) * GDN_D, lanes].astype(F32)
            o_ref[rows, lanes] = (o_n * _silu(z_t)).astype(o_ref.dtype)


def _gdn_call(gdn_t, ba_t, cw, alog8, dt8, gw):
    s = gdn_t.shape[1]
    tb = GDN_TB
    nqkv = 3 * GDN_HEADS * GDN_D
    const = lambda shape: pl.BlockSpec(shape, lambda i: (0,) * len(shape))
    return pl.pallas_call(
        _gdn_kernel,
        grid=(s // tb,),
        in_specs=[
            pl.BlockSpec((gdn_t.shape[0], tb), lambda i: (0, i)),
            pl.BlockSpec((8, tb), lambda i: (0, i)),
            const(cw.shape),
            const(alog8.shape),
            const(dt8.shape),
            const(gw.shape),
        ],
        out_specs=pl.BlockSpec((GDN_HEADS * GDN_D, tb), lambda i: (0, i)),
        out_shape=jax.ShapeDtypeStruct((GDN_HEADS * GDN_D, s), BF16),
        scratch_shapes=[
            pltpu.VMEM((nqkv, CHUNK), F32),
            pltpu.VMEM((GDN_HEADS, GDN_D, GDN_D), F32),
        ],
        compiler_params=pltpu.CompilerParams(
            dimension_semantics=("arbitrary",), vmem_limit_bytes=VMEM_LIMIT),
        name="gdn",
    )(gdn_t, ba_t, cw, alog8, dt8, gw)


def _attn_kernel(lam_ref, q_ref, k_ref, v_ref, z_ref, sw_ref, o_ref, acc_ref, m_ref, *, lam_init):
    tq, tk = ATT_TQ, ATT_TK
    dv = 2 * DIFF_D
    i = pl.program_id(1)
    qt = q_ref[...]
    row = lax.broadcasted_iota(jnp.int32, qt.shape, 0)
    zero = jnp.zeros_like(qt)
    q_pad = (jnp.where(row < DIFF_D, qt, zero), jnp.where(row >= DIFF_D, qt, zero))
    ones = jnp.ones((ONES_ROWS, tk), BF16)

    acc_ref[...] = jnp.zeros_like(acc_ref)
    m_ref[...] = jnp.full_like(m_ref, NEG_BIG)

    def step(j, masked):
        start = pl.multiple_of(j * tk, tk)
        kt = k_ref[pl.ds(start, tk), :]
        vt = jnp.concatenate([v_ref[:, pl.ds(start, tk)], ones], axis=0)
        if masked:
            kv_pos = lax.broadcasted_iota(jnp.int32, (tk, tq), 0)
            q_pos = lax.broadcasted_iota(jnp.int32, (tk, tq), 1)
            keep = kv_pos <= q_pos
        for c in range(2):
            s_t = jnp.dot(kt, q_pad[c], preferred_element_type=F32)
            if masked:
                s_t = jnp.where(keep, s_t, NEG_BIG)
            m_old = m_ref[c]
            m_new = jnp.maximum(m_old, jnp.max(s_t, axis=0, keepdims=True))
            p_t = jnp.exp(s_t - m_new).astype(BF16)
            m_ref[c] = m_new
            acc_ref[c] = acc_ref[c] * jnp.exp(m_old - m_new) + jnp.dot(
                vt, p_t, preferred_element_type=F32)

    step(i, True)

    def body(j, carry):
        step(j, False)
        return carry

    lax.fori_loop(0, i, body, 0)

    lam = lam_ref[0, 0]
    o1 = acc_ref[0, :dv, :] / acc_ref[0, dv:dv + 1, :]
    o2 = acc_ref[1, :dv, :] / acc_ref[1, dv:dv + 1, :]
    o_t = o1 - lam * o2
    o_n = o_t * lax.rsqrt(jnp.mean(o_t * o_t, axis=0, keepdims=True) + SUBLN_EPS) * sw_ref[...]
    o_n = o_n * (1.0 - lam_init)
    o_ref[...] = (o_n * _silu(z_ref[...].astype(F32))).astype(o_ref.dtype)


def _attn_call(lam, q_t, k_nat, v_t, z_t, sw, lam_init):
    dv = 2 * DIFF_D
    s = k_nat.shape[0]
    h = k_nat.shape[1] // dv
    tq = ATT_TQ
    assert ATT_TQ == ATT_TK
    return pl.pallas_call(
        functools.partial(_attn_kernel, lam_init=lam_init),
        grid=(h, s // tq),
        in_specs=[
            pl.BlockSpec(memory_space=pltpu.SMEM),
            pl.BlockSpec((dv, tq), lambda hh, i: (hh, i)),
            pl.BlockSpec((s, dv), lambda hh, i: (0, hh)),
            pl.BlockSpec((dv, s), lambda hh, i: (hh, 0)),
            pl.BlockSpec((dv, tq), lambda hh, i: (hh, i)),
            pl.BlockSpec((dv, tq), lambda hh, i: (0, 0)),
        ],
        out_specs=pl.BlockSpec((dv, tq), lambda hh, i: (hh, i)),
        out_shape=jax.ShapeDtypeStruct((h * dv, s), BF16),
        scratch_shapes=[
            pltpu.VMEM((2, dv + ONES_ROWS, tq), F32),
            pltpu.VMEM((2, 1, tq), F32),
        ],
        compiler_params=pltpu.CompilerParams(
            dimension_semantics=("arbitrary", "arbitrary"), vmem_limit_bytes=VMEM_LIMIT),
        name="attn",
    )(lam, q_t, k_nat, v_t, z_t, sw)


def _out_kernel(x_ref, oa_ref, ob_ref, w_ref, y_ref):
    o_t = jnp.concatenate([oa_ref[...], ob_ref[...]], axis=0)
    y = lax.dot_general(o_t, w_ref[...], TN_DIMS, preferred_element_type=F32)
    y_ref[...] = x_ref[...] + y


def _out_call(x2, oa_t, ob_t, w_out):
    s, d = x2.shape
    tm = OUT_TM
    return pl.pallas_call(
        _out_kernel,
        grid=(s // tm,),
        in_specs=[
            pl.BlockSpec((tm, d), lambda i: (i, 0)),
            pl.BlockSpec((oa_t.shape[0], tm), lambda i: (0, i)),
            pl.BlockSpec((ob_t.shape[0], tm), lambda i: (0, i)),
            pl.BlockSpec(w_out.shape, lambda i: (0, 0)),
        ],
        out_specs=pl.BlockSpec((tm, d), lambda i: (i, 0)),
        out_shape=jax.ShapeDtypeStruct((s, d), F32),
        compiler_params=pltpu.CompilerParams(
            dimension_semantics=("arbitrary",), vmem_limit_bytes=VMEM_LIMIT),
        name="out",
    )(x2, oa_t, ob_t, w_out)


def _layer(l, x2, cos_t, sin_t, w_norm, w_in, conv_w, a_log, dt_bias, gdn_norm_w, q_norm_w,
           k_norm_w, lambda_q1, lambda_k1, lambda_q2, lambda_k2, subln_w, w_out):
    s, d = x2.shape
    nqk = GDN_HEADS * GDN_D
    ba_lo = 4 * nqk
    ba_hi = ba_lo + 2 * GDN_HEADS
    wt = jnp.concatenate([w_in[:, :ba_lo], w_in[:, ba_hi:]], axis=1).T.astype(BF16)
    wba = w_in[:, ba_lo:ba_hi].T.astype(BF16)
    qw = jnp.broadcast_to(q_norm_w[:, None], (DIFF_D, PROJ_TM)).astype(F32)
    kw = jnp.broadcast_to(k_norm_w[:, None], (DIFF_D, PROJ_TM)).astype(F32)

    gdn_t, ba_t, qb_t, kb, vb_t, zb_t = _proj_call(
        x2, w_norm[None, :].astype(F32), wt, wba, cos_t, sin_t, qw, kw)

    cw = jnp.broadcast_to(conv_w.astype(F32)[:, :, None], (CONV_K, 3 * nqk, LANES))
    zeros4 = jnp.zeros((GDN_HEADS, LANES), F32)
    alog8 = jnp.concatenate([zeros4, jnp.broadcast_to(a_log.astype(F32)[:, None], (GDN_HEADS, LANES))], 0)
    dt8 = jnp.concatenate([zeros4, jnp.broadcast_to(dt_bias.astype(F32)[:, None], (GDN_HEADS, LANES))], 0)
    gw = jnp.broadcast_to(gdn_norm_w.astype(F32)[:, None], (GDN_D, LANES))
    oa_t = _gdn_call(gdn_t, ba_t, cw, alog8, dt8, gw)

    lam_init = 0.8 - 0.6 * math.exp(-0.3 * l)
    lam = (jnp.exp(jnp.sum(lambda_q1.astype(F32) * lambda_k1.astype(F32)))
           - jnp.exp(jnp.sum(lambda_q2.astype(F32) * lambda_k2.astype(F32))) + lam_init)
    dv = 2 * DIFF_D
    sw = jnp.broadcast_to(subln_w.astype(F32)[:, None], (dv, ATT_TQ))
    ob_t = _attn_call(lam.reshape(1, 1).astype(F32), qb_t, kb, vb_t, zb_t, sw, lam_init)

    return _out_call(x2, oa_t, ob_t, w_out.astype(BF16))


def kernel(x, w_norm, w_in, conv_w, a_log, dt_bias, gdn_norm_w, q_norm_w, k_norm_w,
           lambda_q1, lambda_k1, lambda_q2, lambda_k2, subln_w, w_out):
    b, s, d = x.shape
    assert b == 1
    inv_freq = ROPE_THETA ** (-jnp.arange(0, DIFF_D, 2, dtype=jnp.float32) / DIFF_D)
    ang = jnp.arange(s, dtype=jnp.float32)[:, None] * inv_freq[None, :]
    cos_t, sin_t = jnp.cos(ang).T, jnp.sin(ang).T
    x2 = x[0]
    for l in range(w_norm.shape[0]):
        x2 = _layer(l, x2, cos_t, sin_t, w_norm[l], w_in[l], conv_w[l], a_log[l], dt_bias[l],
                    gdn_norm_w[l], q_norm_w[l], k_norm_w[l], lambda_q1[l], lambda_k1[l],
                    lambda_q2[l], lambda_k2[l], subln_w[l], w_out[l])
    return x2[None]
```

```python
import functools
import math

import jax
import jax.numpy as jnp
from jax import lax
from jax.experimental import pallas as pl
from jax.experimental.pallas import tpu as pltpu

F32 = jnp.float32
BF16 = jnp.bfloat16

GDN_HEADS = 4
GDN_D = 128
CONV_K = 4
DIFF_HEADS = 4
DIFF_D = 64
ROPE_THETA = 10000.0
EPS = 1e-6
SUBLN_EPS = 1e-5

LANES = 128
CHUNK = LANES
NEG_BIG = -1e30
LOG2E = 1.4426950408889634
FROZEN_MAX_GAP = 40.0
NORM_SLACK = 1.01

PROJ_TM = 512
GDN_TB = 256
ATT_TQ = 512
ATT_TK = 512
OUT_TM = 512
ONES_ROWS = 8
VMEM_LIMIT = 56 * 1024 * 1024

NT_DIMS = (((1,), (1,)), ((), ()))
TN_DIMS = (((0,), (0,)), ((), ()))


def _sigmoid(v):
    return 1.0 / (1.0 + jnp.exp(-v))


def _silu(v):
    return v * _sigmoid(v)


def _proj_kernel(x_ref, wn_ref, wt_ref, wba_ref, cos_ref, sin_ref, qw_ref, kw_ref,
                 gdn_ref, ba_ref, qb_ref, kb_ref, ks_ref, vb_ref, zb_ref):
    x = x_ref[...]
    ms = jnp.mean(x * x, axis=-1, keepdims=True)
    hn = (x * lax.rsqrt(ms + EPS) * wn_ref[...]).astype(BF16)

    def proj_t(lo, n):
        return lax.dot_general(wt_ref[lo:lo + n, :], hn, NT_DIMS, preferred_element_type=F32)

    blk = 512
    for b in range(4):
        gdn_ref[b * blk:(b + 1) * blk, :] = proj_t(b * blk, blk).astype(BF16)
    ba_ref[...] = lax.dot_general(wba_ref[...], hn, NT_DIMS, preferred_element_type=F32)

    cos = cos_ref[...]
    sin = sin_ref[...]
    half = DIFF_D // 2

    def norm_rope(t, w, scale):
        outs = []
        for g in range(t.shape[0] // DIFF_D):
            tg = t[g * DIFF_D:(g + 1) * DIFF_D, :]
            tg = tg * lax.rsqrt(jnp.mean(tg * tg, axis=0, keepdims=True) + EPS) * w
            t1, t2 = tg[:half, :], tg[half:, :]
            outs.append((t1 * cos - t2 * sin) * scale)
            outs.append((t2 * cos + t1 * sin) * scale)
        return jnp.concatenate(outs, axis=0)

    qb_ref[...] = norm_rope(proj_t(4 * blk, blk), qw_ref[...], LOG2E / math.sqrt(DIFF_D)).astype(BF16)
    kt = norm_rope(proj_t(5 * blk, blk), kw_ref[...], 1.0)
    for g in range(2 * DIFF_HEADS):
        kg = kt[g * DIFF_D:(g + 1) * DIFF_D, :]
        ks_ref[g // 2, g % 2:g % 2 + 1, :] = jnp.sum(kg * kg, axis=0, keepdims=True)
    kb_ref[...] = kt.T.astype(BF16)
    vb_ref[...] = proj_t(6 * blk, blk).astype(BF16)
    zb_ref[...] = proj_t(7 * blk, blk).astype(BF16)


def _proj_call(x2, w_norm, wt, wba, cos_t, sin_t, qw, kw):
    s, d = x2.shape
    tm = PROJ_TM
    const = lambda shape: pl.BlockSpec(shape, lambda i: (0,) * len(shape))
    cols = lambda rows: pl.BlockSpec((rows, tm), lambda i: (0, i))
    return pl.pallas_call(
        _proj_kernel,
        grid=(s // tm,),
        in_specs=[
            pl.BlockSpec((tm, d), lambda i: (i, 0)),
            const((1, d)),
            const(wt.shape),
            const(wba.shape),
            cols(DIFF_D // 2),
            cols(DIFF_D // 2),
            const((DIFF_D, tm)),
            const((DIFF_D, tm)),
        ],
        out_specs=[
            cols(2048),
            cols(8),
            cols(512),
            pl.BlockSpec((tm, 512), lambda i: (i, 0)),
            pl.BlockSpec((DIFF_HEADS, 2, tm), lambda i: (0, 0, i)),
            cols(512),
            cols(512),
        ],
        out_shape=[
            jax.ShapeDtypeStruct((2048, s), BF16),
            jax.ShapeDtypeStruct((8, s), F32),
            jax.ShapeDtypeStruct((512, s), BF16),
            jax.ShapeDtypeStruct((s, 512), BF16),
            jax.ShapeDtypeStruct((DIFF_HEADS, 2, s), F32),
            jax.ShapeDtypeStruct((512, s), BF16),
            jax.ShapeDtypeStruct((512, s), BF16),
        ],
        compiler_params=pltpu.CompilerParams(
            dimension_semantics=("arbitrary",), vmem_limit_bytes=VMEM_LIMIT),
        name="proj",
    )(x2, w_norm, wt, wba, cos_t, sin_t, qw, kw)


def _gdn_kernel(g_ref, ba_ref, cw_ref, alog_ref, dt_ref, gw_ref, o_ref, halo_ref, st_ref):
    nqkv = 3 * GDN_HEADS * GDN_D
    n_chunks = g_ref.shape[1] // CHUNK

    @pl.when(pl.program_id(0) == 0)
    def _():
        halo_ref[...] = jnp.zeros_like(halo_ref)
        st_ref[...] = jnp.zeros_like(st_ref)

    lane = lax.broadcasted_iota(jnp.int32, (CHUNK, CHUNK), 1)
    subl = lax.broadcasted_iota(jnp.int32, (CHUNK, CHUNK), 0)
    eye = (lane == subl).astype(F32)
    lane8 = lax.broadcasted_iota(jnp.int32, (8, CHUNK), 1)
    lane_c = lax.broadcasted_iota(jnp.int32, (nqkv, CHUNK), 1)

    def rolls(v):
        return [pltpu.roll(v, j, axis=1) for j in range(1, CONV_K)]

    prev_rolls = rolls(halo_ref[...])
    for c in range(n_chunks):
        lanes = slice(c * CHUNK, (c + 1) * CHUNK)
        cur = g_ref[0:nqkv, lanes].astype(F32)
        cur_rolls = rolls(cur)
        if c == n_chunks - 1:
            halo_ref[...] = cur
        y = cw_ref[CONV_K - 1] * cur
        for j in range(1, CONV_K):
            shifted = jnp.where(lane_c >= j, cur_rolls[j - 1], prev_rolls[j - 1])
            y = y + cw_ref[CONV_K - 1 - j] * shifted
        prev_rolls = cur_rolls
        y = _silu(y)

        ba = ba_ref[:, lanes]
        beta8 = _sigmoid(ba)
        sp = ba + dt_ref[...]
        softplus = jnp.maximum(sp, 0.0) + jnp.log(1.0 + jnp.exp(-jnp.abs(sp)))
        gc8 = -jnp.exp(alog_ref[...]) * softplus
        sh = 1
        while sh < CHUNK:
            gc8 = gc8 + jnp.where(lane8 >= sh, pltpu.roll(gc8, sh, axis=1), 0.0)
            sh *= 2

        for h in range(GDN_HEADS):
            rows = slice(h * GDN_D, (h + 1) * GDN_D)
            qt = y[h * GDN_D:(h + 1) * GDN_D, :]
            kt = y[(GDN_HEADS + h) * GDN_D:(GDN_HEADS + h + 1) * GDN_D, :]
            vt = y[(2 * GDN_HEADS + h) * GDN_D:(2 * GDN_HEADS + h + 1) * GDN_D, :]
            qt = qt * lax.rsqrt(jnp.sum(qt * qt, axis=0, keepdims=True) + EPS) * (GDN_D ** -0.5)
            kt = kt * lax.rsqrt(jnp.sum(kt * kt, axis=0, keepdims=True) + EPS)
            beta = beta8[h:h + 1, :]
            gc = gc8[GDN_HEADS + h:GDN_HEADS + h + 1, :]

            g_row = jnp.broadcast_to(gc, (CHUNK, CHUNK))
            g_col = g_row.T
            g_last = g_col[CHUNK - 1:CHUNK, :]
            decay_t = jnp.exp(jnp.where(lane >= subl, g_row - g_col, NEG_BIG))

            k_nat = kt.T
            gram = jnp.dot(k_nat.astype(BF16), jnp.concatenate([kt, qt], axis=1).astype(BF16),
                           preferred_element_type=F32)
            l_t = jnp.where(lane > subl, gram[:, :CHUNK] * decay_t * beta, 0.0)
            a_t = gram[:, CHUNK:] * decay_t

            n_mat = -l_t
            p_mat = eye + n_mat
            m = 2
            while m < CHUNK:
                nb = n_mat.astype(BF16)
                n_mat = jnp.dot(nb, nb, preferred_element_type=F32)
                p_mat = p_mat + jnp.dot(p_mat.astype(BF16), n_mat.astype(BF16),
                                        preferred_element_type=F32)
                m *= 2

            e_gc = jnp.exp(gc)
            rhs = jnp.concatenate([vt * beta, kt * (beta * e_gc)], axis=0)
            uw = jnp.dot(rhs.astype(BF16), p_mat.astype(BF16), preferred_element_type=F32)
            u_t, w_t = uw[:GDN_D, :], uw[GDN_D:, :]
            r1 = jnp.concatenate([w_t, qt * e_gc], axis=1).astype(BF16)
            k_dec = k_nat * jnp.exp(g_last - g_col)
            r2 = jnp.concatenate([k_dec, a_t], axis=1).astype(BF16)

            s_t = st_ref[h]
            x1 = jnp.dot(s_t.astype(BF16), r1, preferred_element_type=F32)
            vn_t = u_t - x1[:, :CHUNK]
            x2 = jnp.dot(vn_t.astype(BF16), r2, preferred_element_type=F32)
            o_t = x1[:, CHUNK:] + x2[:, GDN_D:]
            st_ref[h] = s_t * jnp.exp(g_last) + x2[:, :GDN_D]

            o_n = o_t * lax.rsqrt(jnp.mean(o_t * o_t, axis=0, keepdims=True) + EPS) * gw_ref[...]
            z_t = g_ref[nqkv + h * GDN_D:nqkv + (h + 1) * GDN_D, lanes].astype(F32)
            o_ref[rows, lanes] = (o_n * _silu(z_t)).astype(o_ref.dtype)


def _gdn_call(gdn_t, ba_t, cw, alog8, dt8, gw):
    s = gdn_t.shape[1]
    tb = GDN_TB
    nqkv = 3 * GDN_HEADS * GDN_D
    const = lambda shape: pl.BlockSpec(shape, lambda i: (0,) * len(shape))
    return pl.pallas_call(
        _gdn_kernel,
        grid=(s // tb,),
        in_specs=[
            pl.BlockSpec((gdn_t.shape[0], tb), lambda i: (0, i)),
            pl.BlockSpec((8, tb), lambda i: (0, i)),
            const(cw.shape),
            const(alog8.shape),
            const(dt8.shape),
            const(gw.shape),
        ],
        out_specs=pl.BlockSpec((GDN_HEADS * GDN_D, tb), lambda i: (0, i)),
        out_shape=jax.ShapeDtypeStruct((GDN_HEADS * GDN_D, s), BF16),
        scratch_shapes=[
            pltpu.VMEM((nqkv, CHUNK), F32),
            pltpu.VMEM((GDN_HEADS, GDN_D, GDN_D), F32),
        ],
        compiler_params=pltpu.CompilerParams(
            dimension_semantics=("arbitrary",), vmem_limit_bytes=VMEM_LIMIT),
        name="gdn",
    )(gdn_t, ba_t, cw, alog8, dt8, gw)


def _attn_kernel(lam_ref, q_ref, k_ref, v_ref, z_ref, ks_ref, sw_ref, o_ref,
                 acc_ref, m_ref, p_ref, *, lam_init):
    tq, tk = ATT_TQ, ATT_TK
    dv = 2 * DIFF_D
    i = pl.program_id(1)
    qt = q_ref[...]
    row = lax.broadcasted_iota(jnp.int32, qt.shape, 0)
    zero = jnp.zeros_like(qt)
    q_cat = jnp.concatenate(
        [jnp.where(row < DIFF_D, qt, zero), jnp.where(row >= DIFF_D, qt, zero)], axis=1)
    ones = jnp.ones((ONES_ROWS, tk), BF16)

    def scores(j):
        start = pl.multiple_of(j * tk, tk)
        return jnp.dot(k_ref[pl.ds(start, tk), :], q_cat, preferred_element_type=F32)

    def flush(j_pending, alpha):
        start = pl.multiple_of(j_pending * tk, tk)
        vt = jnp.concatenate([v_ref[:, pl.ds(start, tk)], ones], axis=0)
        for c in range(2):
            upd = acc_ref[c] + jnp.dot(vt, p_ref[:, c * tq:(c + 1) * tq],
                                       preferred_element_type=F32)
            acc_ref[c] = upd if alpha is None else upd * alpha[:, c * tq:(c + 1) * tq]

    def pending(j):
        return jnp.where(j == 0, i, j - 1)

    s_t = scores(i)
    keep = (lax.broadcasted_iota(jnp.int32, (tk, tq), 0)
            <= lax.broadcasted_iota(jnp.int32, (tk, tq), 1))
    s_t = jnp.concatenate([jnp.where(keep, s_t[:, :tq], NEG_BIG),
                           jnp.where(keep, s_t[:, tq:], NEG_BIG)], axis=1)
    m0 = jnp.max(s_t, axis=0, keepdims=True)
    m_ref[...] = m0
    p_ref[...] = jnp.exp2(s_t - m0).astype(BF16)
    acc_ref[...] = jnp.zeros_like(acc_ref)

    qf = qt.astype(F32)
    qq = qf * qf
    qsq = jnp.concatenate([jnp.sum(qq[:DIFF_D], axis=0, keepdims=True),
                           jnp.sum(qq[DIFF_D:], axis=0, keepdims=True)], axis=1)
    kmax = jnp.max(ks_ref[0], axis=1, keepdims=True)
    kmax = jnp.concatenate([jnp.broadcast_to(kmax[0:1], (1, tq)),
                            jnp.broadcast_to(kmax[1:2], (1, tq))], axis=1)
    bound = jnp.sqrt(qsq * kmax) * NORM_SLACK
    frozen_ok = jnp.max(bound - m0) <= FROZEN_MAX_GAP

    def frozen_body(j, carry):
        s_j = scores(j)
        flush(pending(j), None)
        p_ref[...] = jnp.exp2(s_j - m_ref[...]).astype(BF16)
        return carry

    def online_body(j, carry):
        s_j = scores(j)
        m_old = m_ref[...]
        m_new = jnp.maximum(m_old, jnp.max(s_j, axis=0, keepdims=True))
        flush(pending(j), jnp.exp2(m_old - m_new))
        m_ref[...] = m_new
        p_ref[...] = jnp.exp2(s_j - m_new).astype(BF16)
        return carry

    @pl.when(frozen_ok)
    def _():
        lax.fori_loop(0, i, frozen_body, 0)

    @pl.when(jnp.logical_not(frozen_ok))
    def _():
        lax.fori_loop(0, i, online_body, 0)

    flush(pending(i), None)

    lam = lam_ref[0, 0]
    o1 = acc_ref[0, :dv, :] / acc_ref[0, dv:dv + 1, :]
    o2 = acc_ref[1, :dv, :] / acc_ref[1, dv:dv + 1, :]
    o_t = o1 - lam * o2
    o_n = o_t * lax.rsqrt(jnp.mean(o_t * o_t, axis=0, keepdims=True) + SUBLN_EPS) * sw_ref[...]
    o_n = o_n * (1.0 - lam_init)
    o_ref[...] = (o_n * _silu(z_ref[...].astype(F32))).astype(o_ref.dtype)


def _attn_call(lam, q_t, k_nat, v_t, z_t, ksq, sw, lam_init):
    dv = 2 * DIFF_D
    s = k_nat.shape[0]
    h = k_nat.shape[1] // dv
    tq = ATT_TQ
    assert ATT_TQ == ATT_TK
    return pl.pallas_call(
        functools.partial(_attn_kernel, lam_init=lam_init),
        grid=(h, s // tq),
        in_specs=[
            pl.BlockSpec(memory_space=pltpu.SMEM),
            pl.BlockSpec((dv, tq), lambda hh, i: (hh, i)),
            pl.BlockSpec((s, dv), lambda hh, i: (0, hh)),
            pl.BlockSpec((dv, s), lambda hh, i: (hh, 0)),
            pl.BlockSpec((dv, tq), lambda hh, i: (hh, i)),
            pl.BlockSpec((1, 2, s), lambda hh, i: (hh, 0, 0)),
            pl.BlockSpec((dv, tq), lambda hh, i: (0, 0)),
        ],
        out_specs=pl.BlockSpec((dv, tq), lambda hh, i: (hh, i)),
        out_shape=jax.ShapeDtypeStruct((h * dv, s), BF16),
        scratch_shapes=[
            pltpu.VMEM((2, dv + ONES_ROWS, tq), F32),
            pltpu.VMEM((1, 2 * tq), F32),
            pltpu.VMEM((ATT_TK, 2 * tq), BF16),
        ],
        compiler_params=pltpu.CompilerParams(
            dimension_semantics=("arbitrary", "arbitrary"), vmem_limit_bytes=VMEM_LIMIT),
        name="attn",
    )(lam, q_t, k_nat, v_t, z_t, ksq, sw)


def _out_kernel(x_ref, oa_ref, ob_ref, w_ref, y_ref):
    o_t = jnp.concatenate([oa_ref[...], ob_ref[...]], axis=0)
    y = lax.dot_general(o_t, w_ref[...], TN_DIMS, preferred_element_type=F32)
    y_ref[...] = x_ref[...] + y


def _out_call(x2, oa_t, ob_t, w_out):
    s, d = x2.shape
    tm = OUT_TM
    return pl.pallas_call(
        _out_kernel,
        grid=(s // tm,),
        in_specs=[
            pl.BlockSpec((tm, d), lambda i: (i, 0)),
            pl.BlockSpec((oa_t.shape[0], tm), lambda i: (0, i)),
            pl.BlockSpec((ob_t.shape[0], tm), lambda i: (0, i)),
            pl.BlockSpec(w_out.shape, lambda i: (0, 0)),
        ],
        out_specs=pl.BlockSpec((tm, d), lambda i: (i, 0)),
        out_shape=jax.ShapeDtypeStruct((s, d), F32),
        compiler_params=pltpu.CompilerParams(
            dimension_semantics=("arbitrary",), vmem_limit_bytes=VMEM_LIMIT),
        name="out",
    )(x2, oa_t, ob_t, w_out)


def _layer(l, x2, cos_t, sin_t, w_norm, w_in, conv_w, a_log, dt_bias, gdn_norm_w, q_norm_w,
           k_norm_w, lambda_q1, lambda_k1, lambda_q2, lambda_k2, subln_w, w_out):
    s, d = x2.shape
    nqk = GDN_HEADS * GDN_D
    ba_lo = 4 * nqk
    ba_hi = ba_lo + 2 * GDN_HEADS
    wt = jnp.concatenate([w_in[:, :ba_lo], w_in[:, ba_hi:]], axis=1).T.astype(BF16)
    wba = w_in[:, ba_lo:ba_hi].T.astype(BF16)
    qw = jnp.broadcast_to(q_norm_w[:, None], (DIFF_D, PROJ_TM)).astype(F32)
    kw = jnp.broadcast_to(k_norm_w[:, None], (DIFF_D, PROJ_TM)).astype(F32)

    gdn_t, ba_t, qb_t, kb, ksq, vb_t, zb_t = _proj_call(
        x2, w_norm[None, :].astype(F32), wt, wba, cos_t, sin_t, qw, kw)

    cw = jnp.broadcast_to(conv_w.astype(F32)[:, :, None], (CONV_K, 3 * nqk, LANES))
    zeros4 = jnp.zeros((GDN_HEADS, LANES), F32)
    alog8 = jnp.concatenate([zeros4, jnp.broadcast_to(a_log.astype(F32)[:, None], (GDN_HEADS, LANES))], 0)
    dt8 = jnp.concatenate([zeros4, jnp.broadcast_to(dt_bias.astype(F32)[:, None], (GDN_HEADS, LANES))], 0)
    gw = jnp.broadcast_to(gdn_norm_w.astype(F32)[:, None], (GDN_D, LANES))
    oa_t = _gdn_call(gdn_t, ba_t, cw, alog8, dt8, gw)

    lam_init = 0.8 - 0.6 * math.exp(-0.3 * l)
    lam = (jnp.exp(jnp.sum(lambda_q1.astype(F32) * lambda_k1.astype(F32)))
           - jnp.exp(jnp.sum(lambda_q2.astype(F32) * lambda_k2.astype(F32))) + lam_init)
    dv = 2 * DIFF_D
    sw = jnp.broadcast_to(subln_w.astype(F32)[:, None], (dv, ATT_TQ))
    ob_t = _attn_call(lam.reshape(1, 1).astype(F32), qb_t, kb, vb_t, zb_t, ksq, sw, lam_init)

    return _out_call(x2, oa_t, ob_t, w_out.astype(BF16))


def kernel(x, w_norm, w_in, conv_w, a_log, dt_bias, gdn_norm_w, q_norm_w, k_norm_w,
           lambda_q1, lambda_k1, lambda_q2, lambda_k2, subln_w, w_out):
    b, s, d = x.shape
    assert b == 1
    inv_freq = ROPE_THETA ** (-jnp.arange(0, DIFF_D, 2, dtype=jnp.float32) / DIFF_D)
    ang = jnp.arange(s, dtype=jnp.float32)[:, None] * inv_freq[None, :]
    cos_t, sin_t = jnp.cos(ang).T, jnp.sin(ang).T
    x2 = x[0]
    for l in range(w_norm.shape[0]):
        x2 = _layer(l, x2, cos_t, sin_t, w_norm[l], w_in[l], conv_w[l], a_log[l], dt_bias[l],
                    gdn_norm_w[l], q_norm_w[l], k_norm_w[l], lambda_q1[l], lambda_k1[l],
                    lambda_q2[l], lambda_k2[l], subln_w[l], w_out[l])
    return x2[None]
```

```python
import functools
import math

import jax
import jax.numpy as jnp
from jax import lax
from jax.experimental import pallas as pl
from jax.experimental.pallas import tpu as pltpu

F32 = jnp.float32
BF16 = jnp.bfloat16

GDN_HEADS = 4
GDN_D = 128
CONV_K = 4
DIFF_HEADS = 4
DIFF_D = 64
ROPE_THETA = 10000.0
EPS = 1e-6
SUBLN_EPS = 1e-5

LANES = 128
CHUNK = LANES
NEG_BIG = -1e30
LOG2E = 1.4426950408889634
FROZEN_MAX_GAP = 40.0
NORM_SLACK = 1.01

PROJ_TM = 512
GDN_TB = 512
ATT_TQ = 512
ATT_TK = 512
OUT_TM = 512
ONES_ROWS = 8
VMEM_LIMIT = 56 * 1024 * 1024

NT_DIMS = (((1,), (1,)), ((), ()))
TN_DIMS = (((0,), (0,)), ((), ()))


def _sigmoid(v):
    return 1.0 / (1.0 + jnp.exp(-v))


def _silu(v):
    return v * _sigmoid(v)


def _proj_kernel(x_ref, wn_ref, wt_ref, wba_ref, cos_ref, sin_ref, qw_ref, kw_ref,
                 gdn_ref, ba_ref, qb_ref, kb_ref, ks_ref, vb_ref, zb_ref):
    x = x_ref[...]
    ms = jnp.mean(x * x, axis=-1, keepdims=True)
    hn = (x * lax.rsqrt(ms + EPS) * wn_ref[...]).astype(BF16)

    def proj_t(lo, n):
        return lax.dot_general(wt_ref[lo:lo + n, :], hn, NT_DIMS, preferred_element_type=F32)

    blk = 512
    for b in range(4):
        gdn_ref[b * blk:(b + 1) * blk, :] = proj_t(b * blk, blk).astype(BF16)
    ba_ref[...] = lax.dot_general(wba_ref[...], hn, NT_DIMS, preferred_element_type=F32)

    cos = cos_ref[...]
    sin = sin_ref[...]
    half = DIFF_D // 2

    def norm_rope(t, w, scale):
        outs = []
        for g in range(t.shape[0] // DIFF_D):
            tg = t[g * DIFF_D:(g + 1) * DIFF_D, :]
            tg = tg * lax.rsqrt(jnp.mean(tg * tg, axis=0, keepdims=True) + EPS) * w
            t1, t2 = tg[:half, :], tg[half:, :]
            outs.append((t1 * cos - t2 * sin) * scale)
            outs.append((t2 * cos + t1 * sin) * scale)
        return jnp.concatenate(outs, axis=0)

    qb_ref[...] = norm_rope(proj_t(4 * blk, blk), qw_ref[...], LOG2E / math.sqrt(DIFF_D)).astype(BF16)
    kt = norm_rope(proj_t(5 * blk, blk), kw_ref[...], 1.0)
    for g in range(2 * DIFF_HEADS):
        kg = kt[g * DIFF_D:(g + 1) * DIFF_D, :]
        ks_ref[g // 2, g % 2:g % 2 + 1, :] = jnp.sum(kg * kg, axis=0, keepdims=True)
    kb_ref[...] = kt.T.astype(BF16)
    vb_ref[...] = proj_t(6 * blk, blk).astype(BF16)
    zb_ref[...] = proj_t(7 * blk, blk).astype(BF16)


def _proj_call(x2, w_norm, wt, wba, cos_t, sin_t, qw, kw):
    s, d = x2.shape
    tm = PROJ_TM
    const = lambda shape: pl.BlockSpec(shape, lambda i: (0,) * len(shape))
    cols = lambda rows: pl.BlockSpec((rows, tm), lambda i: (0, i))
    return pl.pallas_call(
        _proj_kernel,
        grid=(s // tm,),
        in_specs=[
            pl.BlockSpec((tm, d), lambda i: (i, 0)),
            const((1, d)),
            const(wt.shape),
            const(wba.shape),
            cols(DIFF_D // 2),
            cols(DIFF_D // 2),
            const((DIFF_D, tm)),
            const((DIFF_D, tm)),
        ],
        out_specs=[
            cols(2048),
            cols(8),
            cols(512),
            pl.BlockSpec((tm, 512), lambda i: (i, 0)),
            pl.BlockSpec((DIFF_HEADS, 2, tm), lambda i: (0, 0, i)),
            cols(512),
            cols(512),
        ],
        out_shape=[
            jax.ShapeDtypeStruct((2048, s), BF16),
            jax.ShapeDtypeStruct((8, s), F32),
            jax.ShapeDtypeStruct((512, s), BF16),
            jax.ShapeDtypeStruct((s, 512), BF16),
            jax.ShapeDtypeStruct((DIFF_HEADS, 2, s), F32),
            jax.ShapeDtypeStruct((512, s), BF16),
            jax.ShapeDtypeStruct((512, s), BF16),
        ],
        compiler_params=pltpu.CompilerParams(
            dimension_semantics=("arbitrary",), vmem_limit_bytes=VMEM_LIMIT),
        name="proj",
    )(x2, w_norm, wt, wba, cos_t, sin_t, qw, kw)


def _gdn_kernel(g_ref, ba_ref, cw_ref, alog_ref, dt_ref, gw_ref, o_ref, halo_ref, st_ref):
    nqkv = 3 * GDN_HEADS * GDN_D
    n_chunks = g_ref.shape[1] // CHUNK

    @pl.when(pl.program_id(0) == 0)
    def _():
        halo_ref[...] = jnp.zeros_like(halo_ref)
        st_ref[...] = jnp.zeros_like(st_ref)

    lane = lax.broadcasted_iota(jnp.int32, (CHUNK, CHUNK), 1)
    subl = lax.broadcasted_iota(jnp.int32, (CHUNK, CHUNK), 0)
    eye = (lane == subl).astype(F32)
    lane8 = lax.broadcasted_iota(jnp.int32, (8, CHUNK), 1)
    lane_c = lax.broadcasted_iota(jnp.int32, (nqkv, CHUNK), 1)

    def rolls(v):
        return [pltpu.roll(v, j, axis=1) for j in range(1, CONV_K)]

    chains = []
    prev_rolls = rolls(halo_ref[...])
    for c in range(n_chunks):
        lanes = slice(c * CHUNK, (c + 1) * CHUNK)
        cur = g_ref[0:nqkv, lanes].astype(F32)
        cur_rolls = rolls(cur)
        if c == n_chunks - 1:
            halo_ref[...] = cur
        y = cw_ref[CONV_K - 1] * cur
        for j in range(1, CONV_K):
            shifted = jnp.where(lane_c >= j, cur_rolls[j - 1], prev_rolls[j - 1])
            y = y + cw_ref[CONV_K - 1 - j] * shifted
        prev_rolls = cur_rolls
        y = _silu(y)

        ba = ba_ref[:, lanes]
        beta8 = _sigmoid(ba)
        sp = ba + dt_ref[...]
        softplus = jnp.maximum(sp, 0.0) + jnp.log(1.0 + jnp.exp(-jnp.abs(sp)))
        gc8 = (-LOG2E) * jnp.exp(alog_ref[...]) * softplus
        sh = 1
        while sh < CHUNK:
            gc8 = gc8 + jnp.where(lane8 >= sh, pltpu.roll(gc8, sh, axis=1), 0.0)
            sh *= 2

        for h in range(GDN_HEADS):
            qt = y[h * GDN_D:(h + 1) * GDN_D, :]
            kt = y[(GDN_HEADS + h) * GDN_D:(GDN_HEADS + h + 1) * GDN_D, :]
            vt = y[(2 * GDN_HEADS + h) * GDN_D:(2 * GDN_HEADS + h + 1) * GDN_D, :]
            qt = qt * lax.rsqrt(jnp.sum(qt * qt, axis=0, keepdims=True) + EPS) * (GDN_D ** -0.5)
            kt = kt * lax.rsqrt(jnp.sum(kt * kt, axis=0, keepdims=True) + EPS)
            beta = beta8[h:h + 1, :]
            gc = gc8[GDN_HEADS + h:GDN_HEADS + h + 1, :]
            g_row = jnp.broadcast_to(gc, (CHUNK, CHUNK))
            g_col = g_row.T
            g_last = g_col[CHUNK - 1:CHUNK, :]
            e_gc = jnp.exp2(gc)
            chains.append(dict(
                c=c, h=h, qt=qt, kt=kt, beta=beta, g_last=g_last, g_col=g_col,
                decay_t=jnp.exp2(jnp.where(lane >= subl, g_row - g_col, NEG_BIG)),
                k_nat=kt.T,
                qg=qt * e_gc,
                rhs=jnp.concatenate([vt * beta, kt * (beta * e_gc)], axis=0).astype(BF16)))

    for ch in chains:
        gram = jnp.dot(ch["k_nat"].astype(BF16),
                       jnp.concatenate([ch["kt"], ch["qt"]], axis=1).astype(BF16),
                       preferred_element_type=F32)
        ch["a_t"] = gram[:, CHUNK:] * ch["decay_t"]
        ch["n"] = jnp.where(lane > subl, gram[:, :CHUNK] * ch["decay_t"] * (-ch["beta"]), 0.0)

    for ch in chains:
        nb = ch["n"].astype(BF16)
        ch["q"] = eye + ch["n"]
        ch["n"] = jnp.dot(nb, nb, preferred_element_type=F32)
    m = 2
    while 2 * m < CHUNK:
        for ch in chains:
            nb = ch["n"].astype(BF16)
            both = jnp.dot(jnp.concatenate([ch["q"].astype(BF16), nb], axis=0), nb,
                           preferred_element_type=F32)
            ch["q"] = ch["q"] + both[:CHUNK]
            ch["n"] = both[CHUNK:]
        m *= 2
    for ch in chains:
        ch["q"] = ch["q"] + jnp.dot(ch["q"].astype(BF16), ch["n"].astype(BF16),
                                    preferred_element_type=F32)

    for ch in chains:
        uw = jnp.dot(ch["rhs"], ch["q"].astype(BF16), preferred_element_type=F32)
        ch["u_t"] = uw[:GDN_D, :]
        ch["r1"] = jnp.concatenate([uw[GDN_D:, :], ch["qg"]], axis=1).astype(BF16)
        k_dec = ch["k_nat"] * jnp.exp2(ch["g_last"] - ch["g_col"])
        ch["r2"] = jnp.concatenate([k_dec, ch["a_t"]], axis=1).astype(BF16)

    states = [st_ref[h] for h in range(GDN_HEADS)]
    for c in range(n_chunks):
        lanes = slice(c * CHUNK, (c + 1) * CHUNK)
        row = chains[c * GDN_HEADS:(c + 1) * GDN_HEADS]
        x1 = [jnp.dot(states[h].astype(BF16), row[h]["r1"], preferred_element_type=F32)
              for h in range(GDN_HEADS)]
        x2 = [jnp.dot((row[h]["u_t"] - x1[h][:, :CHUNK]).astype(BF16), row[h]["r2"],
                      preferred_element_type=F32) for h in range(GDN_HEADS)]
        for h in range(GDN_HEADS):
            states[h] = states[h] * jnp.exp2(row[h]["g_last"]) + x2[h][:, :GDN_D]
            o_t = x1[h][:, CHUNK:] + x2[h][:, GDN_D:]
            o_n = o_t * lax.rsqrt(jnp.mean(o_t * o_t, axis=0, keepdims=True) + EPS) * gw_ref[...]
            z_t = g_ref[nqkv + h * GDN_D:nqkv + (h + 1) * GDN_D, lanes].astype(F32)
            o_ref[h * GDN_D:(h + 1) * GDN_D, lanes] = (o_n * _silu(z_t)).astype(o_ref.dtype)
    for h in range(GDN_HEADS):
        st_ref[h] = states[h]


def _gdn_call(gdn_t, ba_t, cw, alog8, dt8, gw):
    s = gdn_t.shape[1]
    tb = GDN_TB
    nqkv = 3 * GDN_HEADS * GDN_D
    const = lambda shape: pl.BlockSpec(shape, lambda i: (0,) * len(shape))
    return pl.pallas_call(
        _gdn_kernel,
        grid=(s // tb,),
        in_specs=[
            pl.BlockSpec((gdn_t.shape[0], tb), lambda i: (0, i)),
            pl.BlockSpec((8, tb), lambda i: (0, i)),
            const(cw.shape),
            const(alog8.shape),
            const(dt8.shape),
            const(gw.shape),
        ],
        out_specs=pl.BlockSpec((GDN_HEADS * GDN_D, tb), lambda i: (0, i)),
        out_shape=jax.ShapeDtypeStruct((GDN_HEADS * GDN_D, s), BF16),
        scratch_shapes=[
            pltpu.VMEM((nqkv, CHUNK), F32),
            pltpu.VMEM((GDN_HEADS, GDN_D, GDN_D), F32),
        ],
        compiler_params=pltpu.CompilerParams(
            dimension_semantics=("arbitrary",), vmem_limit_bytes=VMEM_LIMIT),
        name="gdn",
    )(gdn_t, ba_t, cw, alog8, dt8, gw)


def _attn_kernel(lam_ref, q_ref, k_ref, v_ref, z_ref, ks_ref, sw_ref, o_ref,
                 acc_ref, m_ref, p_ref, *, lam_init):
    tq, tk = ATT_TQ, ATT_TK
    dv = 2 * DIFF_D
    i = pl.program_id(1)
    qt = q_ref[...]
    row = lax.broadcasted_iota(jnp.int32, qt.shape, 0)
    zero = jnp.zeros_like(qt)
    q_cat = jnp.concatenate(
        [jnp.where(row < DIFF_D, qt, zero), jnp.where(row >= DIFF_D, qt, zero)], axis=1)
    ones = jnp.ones((ONES_ROWS, tk), BF16)

    def scores(j):
        start = pl.multiple_of(j * tk, tk)
        return jnp.dot(k_ref[pl.ds(start, tk), :], q_cat, preferred_element_type=F32)

    def flush(j_pending, alpha):
        start = pl.multiple_of(j_pending * tk, tk)
        vt = jnp.concatenate([v_ref[:, pl.ds(start, tk)], ones], axis=0)
        for c in range(2):
            upd = acc_ref[c] + jnp.dot(vt, p_ref[:, c * tq:(c + 1) * tq],
                                       preferred_element_type=F32)
            acc_ref[c] = upd if alpha is None else upd * alpha[:, c * tq:(c + 1) * tq]

    def pending(j):
        return jnp.where(j == 0, i, j - 1)

    s_t = scores(i)
    keep = (lax.broadcasted_iota(jnp.int32, (tk, tq), 0)
            <= lax.broadcasted_iota(jnp.int32, (tk, tq), 1))
    s_t = jnp.concatenate([jnp.where(keep, s_t[:, :tq], NEG_BIG),
                           jnp.where(keep, s_t[:, tq:], NEG_BIG)], axis=1)
    m0 = jnp.max(s_t, axis=0, keepdims=True)
    m_ref[...] = m0
    p_ref[...] = jnp.exp2(s_t - m0).astype(BF16)
    acc_ref[...] = jnp.zeros_like(acc_ref)

    qf = qt.astype(F32)
    qq = qf * qf
    qsq = jnp.concatenate([jnp.sum(qq[:DIFF_D], axis=0, keepdims=True),
                           jnp.sum(qq[DIFF_D:], axis=0, keepdims=True)], axis=1)
    kmax = jnp.max(ks_ref[0], axis=1, keepdims=True)
    kmax = jnp.concatenate([jnp.broadcast_to(kmax[0:1], (1, tq)),
                            jnp.broadcast_to(kmax[1:2], (1, tq))], axis=1)
    bound = jnp.sqrt(qsq * kmax) * NORM_SLACK
    frozen_ok = jnp.max(bound - m0) <= FROZEN_MAX_GAP

    def frozen_body(j, carry):
        s_j = scores(j)
        flush(pending(j), None)
        p_ref[...] = jnp.exp2(s_j - m_ref[...]).astype(BF16)
        return carry

    def online_body(j, carry):
        s_j = scores(j)
        m_old = m_ref[...]
        m_new = jnp.maximum(m_old, jnp.max(s_j, axis=0, keepdims=True))
        flush(pending(j), jnp.exp2(m_old - m_new))
        m_ref[...] = m_new
        p_ref[...] = jnp.exp2(s_j - m_new).astype(BF16)
        return carry

    @pl.when(frozen_ok)
    def _():
        lax.fori_loop(0, i, frozen_body, 0)

    @pl.when(jnp.logical_not(frozen_ok))
    def _():
        lax.fori_loop(0, i, online_body, 0)

    flush(pending(i), None)

    lam = lam_ref[0, 0]
    o1 = acc_ref[0, :dv, :] / acc_ref[0, dv:dv + 1, :]
    o2 = acc_ref[1, :dv, :] / acc_ref[1, dv:dv + 1, :]
    o_t = o1 - lam * o2
    o_n = o_t * lax.rsqrt(jnp.mean(o_t * o_t, axis=0, keepdims=True) + SUBLN_EPS) * sw_ref[...]
    o_n = o_n * (1.0 - lam_init)
    o_ref[...] = (o_n * _silu(z_ref[...].astype(F32))).astype(o_ref.dtype)


def _attn_call(lam, q_t, k_nat, v_t, z_t, ksq, sw, lam_init):
    dv = 2 * DIFF_D
    s = k_nat.shape[0]
    h = k_nat.shape[1] // dv
    tq = ATT_TQ
    assert ATT_TQ == ATT_TK
    return pl.pallas_call(
        functools.partial(_attn_kernel, lam_init=lam_init),
        grid=(h, s // tq),
        in_specs=[
            pl.BlockSpec(memory_space=pltpu.SMEM),
            pl.BlockSpec((dv, tq), lambda hh, i: (hh, i)),
            pl.BlockSpec((s, dv), lambda hh, i: (0, hh)),
            pl.BlockSpec((dv, s), lambda hh, i: (hh, 0)),
            pl.BlockSpec((dv, tq), lambda hh, i: (hh, i)),
            pl.BlockSpec((1, 2, s), lambda hh, i: (hh, 0, 0)),
            pl.BlockSpec((dv, tq), lambda hh, i: (0, 0)),
        ],
        out_specs=pl.BlockSpec((dv, tq), lambda hh, i: (hh, i)),
        out_shape=jax.ShapeDtypeStruct((h * dv, s), BF16),
        scratch_shapes=[
            pltpu.VMEM((2, dv + ONES_ROWS, tq), F32),
            pltpu.VMEM((1, 2 * tq), F32),
            pltpu.VMEM((ATT_TK, 2 * tq), BF16),
        ],
        compiler_params=pltpu.CompilerParams(
            dimension_semantics=("arbitrary", "arbitrary"), vmem_limit_bytes=VMEM_LIMIT),
        name="attn",
    )(lam, q_t, k_nat, v_t, z_t, ksq, sw)


def _out_kernel(x_ref, oa_ref, ob_ref, w_ref, y_ref):
    o_t = jnp.concatenate([oa_ref[...], ob_ref[...]], axis=0)
    y = lax.dot_general(o_t, w_ref[...], TN_DIMS, preferred_element_type=F32)
    y_ref[...] = x_ref[...] + y


def _out_call(x2, oa_t, ob_t, w_out):
    s, d = x2.shape
    tm = OUT_TM
    return pl.pallas_call(
        _out_kernel,
        grid=(s // tm,),
        in_specs=[
            pl.BlockSpec((tm, d), lambda i: (i, 0)),
            pl.BlockSpec((oa_t.shape[0], tm), lambda i: (0, i)),
            pl.BlockSpec((ob_t.shape[0], tm), lambda i: (0, i)),
            pl.BlockSpec(w_out.shape, lambda i: (0, 0)),
        ],
        out_specs=pl.BlockSpec((tm, d), lambda i: (i, 0)),
        out_shape=jax.ShapeDtypeStruct((s, d), F32),
        compiler_params=pltpu.CompilerParams(
            dimension_semantics=("arbitrary",), vmem_limit_bytes=VMEM_LIMIT),
        name="out",
    )(x2, oa_t, ob_t, w_out)


def _layer(l, x2, cos_t, sin_t, w_norm, w_in, conv_w, a_log, dt_bias, gdn_norm_w, q_norm_w,
           k_norm_w, lambda_q1, lambda_k1, lambda_q2, lambda_k2, subln_w, w_out):
    s, d = x2.shape
    nqk = GDN_HEADS * GDN_D
    ba_lo = 4 * nqk
    ba_hi = ba_lo + 2 * GDN_HEADS
    wt = jnp.concatenate([w_in[:, :ba_lo], w_in[:, ba_hi:]], axis=1).T.astype(BF16)
    wba = w_in[:, ba_lo:ba_hi].T.astype(BF16)
    qw = jnp.broadcast_to(q_norm_w[:, None], (DIFF_D, PROJ_TM)).astype(F32)
    kw = jnp.broadcast_to(k_norm_w[:, None], (DIFF_D, PROJ_TM)).astype(F32)

    gdn_t, ba_t, qb_t, kb, ksq, vb_t, zb_t = _proj_call(
        x2, w_norm[None, :].astype(F32), wt, wba, cos_t, sin_t, qw, kw)

    cw = jnp.broadcast_to(conv_w.astype(F32)[:, :, None], (CONV_K, 3 * nqk, LANES))
    zeros4 = jnp.zeros((GDN_HEADS, LANES), F32)
    alog8 = jnp.concatenate([zeros4, jnp.broadcast_to(a_log.astype(F32)[:, None], (GDN_HEADS, LANES))], 0)
    dt8 = jnp.concatenate([zeros4, jnp.broadcast_to(dt_bias.astype(F32)[:, None], (GDN_HEADS, LANES))], 0)
    gw = jnp.broadcast_to(gdn_norm_w.astype(F32)[:, None], (GDN_D, LANES))
    oa_t = _gdn_call(gdn_t, ba_t, cw, alog8, dt8, gw)

    lam_init = 0.8 - 0.6 * math.exp(-0.3 * l)
    lam = (jnp.exp(jnp.sum(lambda_q1.astype(F32) * lambda_k1.astype(F32)))
           - jnp.exp(jnp.sum(lambda_q2.astype(F32) * lambda_k2.astype(F32))) + lam_init)
    dv = 2 * DIFF_D
    sw = jnp.broadcast_to(subln_w.astype(F32)[:, None], (dv, ATT_TQ))
    ob_t = _attn_call(lam.reshape(1, 1).astype(F32), qb_t, kb, vb_t, zb_t, ksq, sw, lam_init)

    return _out_call(x2, oa_t, ob_t, w_out.astype(BF16))


def kernel(x, w_norm, w_in, conv_w, a_log, dt_bias, gdn_norm_w, q_norm_w, k_norm_w,
           lambda_q1, lambda_k1, lambda_q2, lambda_k2, subln_w, w_out):
    b, s, d = x.shape
    assert b == 1
    inv_freq = ROPE_THETA ** (-jnp.arange(0, DIFF_D, 2, dtype=jnp.float32) / DIFF_D)
    ang = jnp.arange(s, dtype=jnp.float32)[:, None] * inv_freq[None, :]
    cos_t, sin_t = jnp.cos(ang).T, jnp.sin(ang).T
    x2 = x[0]
    for l in range(w_norm.shape[0]):
        x2 = _layer(l, x2, cos_t, sin_t, w_norm[l], w_in[l], conv_w[l], a_log[l], dt_bias[l],
                    gdn_norm_w[l], q_norm_w[l], k_norm_w[l], lambda_q1[l], lambda_k1[l],
                    lambda_q2[l], lambda_k2[l], subln_w[l], w_out[l])
    return x2[None]
```

```python
import functools
import math

import jax
import jax.numpy as jnp
from jax import lax
from jax.experimental import pallas as pl
from jax.experimental.pallas import tpu as pltpu

F32 = jnp.float32
BF16 = jnp.bfloat16

GDN_HEADS = 4
GDN_D = 128
CONV_K = 4
DIFF_HEADS = 4
DIFF_D = 64
ROPE_THETA = 10000.0
EPS = 1e-6
SUBLN_EPS = 1e-5

LANES = 128
CHUNK = LANES
NEG_BIG = -1e30
LOG2E = 1.4426950408889634
FROZEN_MAX_GAP = 40.0
NORM_SLACK = 1.01

PROJ_TM = 512
GDN_TB = 512
ATT_TQ = 512
ATT_TK = 512
ATT_UNROLL = 4
OUT_TM = 512
VMEM_LIMIT = 56 * 1024 * 1024

NT_DIMS = (((1,), (1,)), ((), ()))
TN_DIMS = (((0,), (0,)), ((), ()))


def _sigmoid(v):
    return 1.0 / (1.0 + jnp.exp(-v))


def _silu(v):
    return v * _sigmoid(v)


def _proj_kernel(x_ref, wn_ref, wt_ref, wba_ref, cos_ref, sin_ref, qw_ref, kw_ref,
                 gdn_ref, ba_ref, qb_ref, kb_ref, ks_ref, vb_ref, zb_ref):
    x = x_ref[...]
    ms = jnp.mean(x * x, axis=-1, keepdims=True)
    hn = (x * lax.rsqrt(ms + EPS) * wn_ref[...]).astype(BF16)

    def proj_t(lo, n):
        return lax.dot_general(wt_ref[lo:lo + n, :], hn, NT_DIMS, preferred_element_type=F32)

    blk = 512
    for b in range(4):
        gdn_ref[b * blk:(b + 1) * blk, :] = proj_t(b * blk, blk).astype(BF16)
    ba_ref[...] = lax.dot_general(wba_ref[...], hn, NT_DIMS, preferred_element_type=F32)

    cos = cos_ref[...]
    sin = sin_ref[...]
    half = DIFF_D // 2

    def norm_rope(t, w, scale):
        outs = []
        for g in range(t.shape[0] // DIFF_D):
            tg = t[g * DIFF_D:(g + 1) * DIFF_D, :]
            tg = tg * lax.rsqrt(jnp.mean(tg * tg, axis=0, keepdims=True) + EPS) * w
            t1, t2 = tg[:half, :], tg[half:, :]
            outs.append((t1 * cos - t2 * sin) * scale)
            outs.append((t2 * cos + t1 * sin) * scale)
        return jnp.concatenate(outs, axis=0)

    qb_ref[...] = norm_rope(proj_t(4 * blk, blk), qw_ref[...], LOG2E / math.sqrt(DIFF_D)).astype(BF16)
    kt = norm_rope(proj_t(5 * blk, blk), kw_ref[...], 1.0)
    for g in range(2 * DIFF_HEADS):
        kg = kt[g * DIFF_D:(g + 1) * DIFF_D, :]
        ks_ref[g // 2, g % 2:g % 2 + 1, :] = jnp.sum(kg * kg, axis=0, keepdims=True)
    kb_ref[...] = kt.T.astype(BF16)
    vb_ref[...] = proj_t(6 * blk, blk).astype(BF16)
    zb_ref[...] = proj_t(7 * blk, blk).astype(BF16)


def _proj_call(x2, w_norm, wt, wba, cos_t, sin_t, qw, kw):
    s, d = x2.shape
    tm = PROJ_TM
    const = lambda shape: pl.BlockSpec(shape, lambda i: (0,) * len(shape))
    cols = lambda rows: pl.BlockSpec((rows, tm), lambda i: (0, i))
    return pl.pallas_call(
        _proj_kernel,
        grid=(s // tm,),
        in_specs=[
            pl.BlockSpec((tm, d), lambda i: (i, 0)),
            const((1, d)),
            const(wt.shape),
            const(wba.shape),
            cols(DIFF_D // 2),
            cols(DIFF_D // 2),
            const((DIFF_D, tm)),
            const((DIFF_D, tm)),
        ],
        out_specs=[
            cols(2048),
            cols(8),
            cols(512),
            pl.BlockSpec((tm, 512), lambda i: (i, 0)),
            pl.BlockSpec((DIFF_HEADS, 2, tm), lambda i: (0, 0, i)),
            cols(512),
            cols(512),
        ],
        out_shape=[
            jax.ShapeDtypeStruct((2048, s), BF16),
            jax.ShapeDtypeStruct((8, s), F32),
            jax.ShapeDtypeStruct((512, s), BF16),
            jax.ShapeDtypeStruct((s, 512), BF16),
            jax.ShapeDtypeStruct((DIFF_HEADS, 2, s), F32),
            jax.ShapeDtypeStruct((512, s), BF16),
            jax.ShapeDtypeStruct((512, s), BF16),
        ],
        compiler_params=pltpu.CompilerParams(
            dimension_semantics=("arbitrary",), vmem_limit_bytes=VMEM_LIMIT),
        name="proj",
    )(x2, w_norm, wt, wba, cos_t, sin_t, qw, kw)


def _gdn_kernel(g_ref, ba_ref, cw_ref, alog_ref, dt_ref, gw_ref, o_ref, halo_ref, st_ref):
    nqkv = 3 * GDN_HEADS * GDN_D
    n_chunks = g_ref.shape[1] // CHUNK

    @pl.when(pl.program_id(0) == 0)
    def _():
        halo_ref[...] = jnp.zeros_like(halo_ref)
        st_ref[...] = jnp.zeros_like(st_ref)

    lane = lax.broadcasted_iota(jnp.int32, (CHUNK, CHUNK), 1)
    subl = lax.broadcasted_iota(jnp.int32, (CHUNK, CHUNK), 0)
    eye = (lane == subl).astype(F32)
    lane8 = lax.broadcasted_iota(jnp.int32, (8, CHUNK), 1)
    lane_c = lax.broadcasted_iota(jnp.int32, (nqkv, CHUNK), 1)

    def rolls(v):
        return [pltpu.roll(v, j, axis=1) for j in range(1, CONV_K)]

    chains = []
    prev_rolls = rolls(halo_ref[...])
    for c in range(n_chunks):
        lanes = slice(c * CHUNK, (c + 1) * CHUNK)
        cur = g_ref[0:nqkv, lanes].astype(F32)
        cur_rolls = rolls(cur)
        if c == n_chunks - 1:
            halo_ref[...] = cur
        y = cw_ref[CONV_K - 1] * cur
        for j in range(1, CONV_K):
            shifted = jnp.where(lane_c >= j, cur_rolls[j - 1], prev_rolls[j - 1])
            y = y + cw_ref[CONV_K - 1 - j] * shifted
        prev_rolls = cur_rolls
        y = _silu(y)

        ba = ba_ref[:, lanes]
        beta8 = _sigmoid(ba)
        sp = ba + dt_ref[...]
        softplus = jnp.maximum(sp, 0.0) + jnp.log(1.0 + jnp.exp(-jnp.abs(sp)))
        gc8 = (-LOG2E) * jnp.exp(alog_ref[...]) * softplus
        sh = 1
        while sh < CHUNK:
            gc8 = gc8 + jnp.where(lane8 >= sh, pltpu.roll(gc8, sh, axis=1), 0.0)
            sh *= 2

        for h in range(GDN_HEADS):
            qt = y[h * GDN_D:(h + 1) * GDN_D, :]
            kt = y[(GDN_HEADS + h) * GDN_D:(GDN_HEADS + h + 1) * GDN_D, :]
            vt = y[(2 * GDN_HEADS + h) * GDN_D:(2 * GDN_HEADS + h + 1) * GDN_D, :]
            qt = qt * lax.rsqrt(jnp.sum(qt * qt, axis=0, keepdims=True) + EPS) * (GDN_D ** -0.5)
            kt = kt * lax.rsqrt(jnp.sum(kt * kt, axis=0, keepdims=True) + EPS)
            beta = beta8[h:h + 1, :]
            gc = gc8[GDN_HEADS + h:GDN_HEADS + h + 1, :]
            g_row = jnp.broadcast_to(gc, (CHUNK, CHUNK))
            g_col = g_row.T
            g_last = g_col[CHUNK - 1:CHUNK, :]
            e_gc = jnp.exp2(gc)
            chains.append(dict(
                c=c, h=h, qt=qt, kt=kt, beta=beta, g_last=g_last, g_col=g_col,
                decay_t=jnp.exp2(jnp.where(lane >= subl, g_row - g_col, NEG_BIG)),
                k_nat=kt.T,
                qg=qt * e_gc,
                rhs=jnp.concatenate([vt * beta, kt * (beta * e_gc)], axis=0).astype(BF16)))

    for ch in chains:
        gram = jnp.dot(ch["k_nat"].astype(BF16),
                       jnp.concatenate([ch["kt"], ch["qt"]], axis=1).astype(BF16),
                       preferred_element_type=F32)
        ch["a_t"] = gram[:, CHUNK:] * ch["decay_t"]
        ch["n"] = jnp.where(lane > subl, gram[:, :CHUNK] * ch["decay_t"] * (-ch["beta"]), 0.0)

    for ch in chains:
        nb = ch["n"].astype(BF16)
        ch["q"] = eye + ch["n"]
        ch["n"] = jnp.dot(nb, nb, preferred_element_type=F32)
    m = 2
    while 2 * m < CHUNK:
        for ch in chains:
            nb = ch["n"].astype(BF16)
            both = jnp.dot(jnp.concatenate([ch["q"].astype(BF16), nb], axis=0), nb,
                           preferred_element_type=F32)
            ch["q"] = ch["q"] + both[:CHUNK]
            ch["n"] = both[CHUNK:]
        m *= 2
    for ch in chains:
        ch["q"] = ch["q"] + jnp.dot(ch["q"].astype(BF16), ch["n"].astype(BF16),
                                    preferred_element_type=F32)

    for ch in chains:
        uw = jnp.dot(ch["rhs"], ch["q"].astype(BF16), preferred_element_type=F32)
        ch["u_t"] = uw[:GDN_D, :]
        ch["r1"] = jnp.concatenate([uw[GDN_D:, :], ch["qg"]], axis=1).astype(BF16)
        k_dec = ch["k_nat"] * jnp.exp2(ch["g_last"] - ch["g_col"])
        ch["r2"] = jnp.concatenate([k_dec, ch["a_t"]], axis=1).astype(BF16)

    states = [st_ref[h] for h in range(GDN_HEADS)]
    for c in range(n_chunks):
        lanes = slice(c * CHUNK, (c + 1) * CHUNK)
        row = chains[c * GDN_HEADS:(c + 1) * GDN_HEADS]
        x1 = [jnp.dot(states[h].astype(BF16), row[h]["r1"], preferred_element_type=F32)
              for h in range(GDN_HEADS)]
        x2 = [jnp.dot((row[h]["u_t"] - x1[h][:, :CHUNK]).astype(BF16), row[h]["r2"],
                      preferred_element_type=F32) for h in range(GDN_HEADS)]
        for h in range(GDN_HEADS):
            states[h] = states[h] * jnp.exp2(row[h]["g_last"]) + x2[h][:, :GDN_D]
            o_t = x1[h][:, CHUNK:] + x2[h][:, GDN_D:]
            o_n = o_t * lax.rsqrt(jnp.mean(o_t * o_t, axis=0, keepdims=True) + EPS) * gw_ref[...]
            z_t = g_ref[nqkv + h * GDN_D:nqkv + (h + 1) * GDN_D, lanes].astype(F32)
            o_ref[h * GDN_D:(h + 1) * GDN_D, lanes] = (o_n * _silu(z_t)).astype(o_ref.dtype)
    for h in range(GDN_HEADS):
        st_ref[h] = states[h]


def _gdn_call(gdn_t, ba_t, cw, alog8, dt8, gw):
    s = gdn_t.shape[1]
    tb = GDN_TB
    nqkv = 3 * GDN_HEADS * GDN_D
    const = lambda shape: pl.BlockSpec(shape, lambda i: (0,) * len(shape))
    return pl.pallas_call(
        _gdn_kernel,
        grid=(s // tb,),
        in_specs=[
            pl.BlockSpec((gdn_t.shape[0], tb), lambda i: (0, i)),
            pl.BlockSpec((8, tb), lambda i: (0, i)),
            const(cw.shape),
            const(alog8.shape),
            const(dt8.shape),
            const(gw.shape),
        ],
        out_specs=pl.BlockSpec((GDN_HEADS * GDN_D, tb), lambda i: (0, i)),
        out_shape=jax.ShapeDtypeStruct((GDN_HEADS * GDN_D, s), BF16),
        scratch_shapes=[
            pltpu.VMEM((nqkv, CHUNK), F32),
            pltpu.VMEM((GDN_HEADS, GDN_D, GDN_D), F32),
        ],
        compiler_params=pltpu.CompilerParams(
            dimension_semantics=("arbitrary",), vmem_limit_bytes=VMEM_LIMIT),
        name="gdn",
    )(gdn_t, ba_t, cw, alog8, dt8, gw)


def _attn_kernel(lam_ref, q_ref, k_ref, v_ref, z_ref, ks_ref, sw_ref, o_ref,
                 acc_ref, l_ref, m_ref, p_ref, *, lam_init):
    tq, tk = ATT_TQ, ATT_TK
    dv = 2 * DIFF_D
    i = pl.program_id(1)
    qt = q_ref[...]
    row = lax.broadcasted_iota(jnp.int32, qt.shape, 0)
    zero = jnp.zeros_like(qt)
    q_cat = jnp.concatenate(
        [jnp.where(row < DIFF_D, qt, zero), jnp.where(row >= DIFF_D, qt, zero)], axis=1)

    def scores(j):
        start = pl.multiple_of(j * tk, tk)
        return jnp.dot(k_ref[pl.ds(start, tk), :], q_cat, preferred_element_type=F32)

    def flush(j_pending, alpha):
        start = pl.multiple_of(j_pending * tk, tk)
        vt = v_ref[:, pl.ds(start, tk)]
        for c in range(2):
            upd = acc_ref[c] + jnp.dot(vt, p_ref[:, c * tq:(c + 1) * tq],
                                       preferred_element_type=F32)
            acc_ref[c] = upd if alpha is None else upd * alpha[:, c * tq:(c + 1) * tq]

    def exp_store(s_j, m):
        p = jnp.exp2(s_j - m)
        p_ref[...] = p.astype(BF16)
        return jnp.sum(p.reshape(tk // 8, 8, 2 * tq), axis=0)

    def pending(j):
        return jnp.where(j == 0, i, j - 1)

    qf = qt.astype(F32)
    qq = qf * qf
    qsq = jnp.concatenate([jnp.sum(qq[:DIFF_D], axis=0, keepdims=True),
                           jnp.sum(qq[DIFF_D:], axis=0, keepdims=True)], axis=1)
    kmax = jnp.max(ks_ref[0], axis=1, keepdims=True)
    kmax = jnp.concatenate([jnp.broadcast_to(kmax[0:1], (1, tq)),
                            jnp.broadcast_to(kmax[1:2], (1, tq))], axis=1)
    bound = jnp.sqrt(qsq * kmax) * NORM_SLACK

    s_t = scores(i)
    keep = (lax.broadcasted_iota(jnp.int32, (tk, tq), 0)
            <= lax.broadcasted_iota(jnp.int32, (tk, tq), 1))
    s_t = jnp.concatenate([jnp.where(keep, s_t[:, :tq], NEG_BIG),
                           jnp.where(keep, s_t[:, tq:], NEG_BIG)], axis=1)
    m0 = jnp.max(s_t, axis=0, keepdims=True)
    frozen_ok = jnp.max(bound - m0) <= FROZEN_MAX_GAP
    m_ref[...] = m0
    l_ref[...] = exp_store(s_t, m0)
    acc_ref[...] = jnp.zeros_like(acc_ref)

    def frozen_step(j, j_pending):
        s_j = scores(j)
        flush(j_pending, None)
        l_ref[...] += exp_store(s_j, m_ref[...])

    def frozen_steps(j0, n):
        for u in range(n):
            frozen_step(j0 + u, pending(j0) if u == 0 else j0 + u - 1)

    def frozen_group(t, carry):
        frozen_steps(ATT_UNROLL * t, ATT_UNROLL)
        return carry

    def online_body(j, carry):
        s_j = scores(j)
        m_old = m_ref[...]
        m_new = jnp.maximum(m_old, jnp.max(s_j, axis=0, keepdims=True))
        alpha = jnp.exp2(m_old - m_new)
        flush(pending(j), alpha)
        m_ref[...] = m_new
        l_ref[...] = l_ref[...] * alpha + exp_store(s_j, m_new)
        return carry

    @pl.when(frozen_ok)
    def _():
        groups = lax.shift_right_logical(i, ATT_UNROLL.bit_length() - 1)
        lax.fori_loop(0, groups, frozen_group, 0)
        j0 = groups * ATT_UNROLL
        n = ATT_UNROLL // 2
        while n >= 1:
            @pl.when((i & n) != 0)
            def _(j0=j0, n=n):
                frozen_steps(j0, n)
            j0 = j0 + (i & n)
            n //= 2

    @pl.when(jnp.logical_not(frozen_ok))
    def _():
        lax.fori_loop(0, i, online_body, 0)

    flush(pending(i), None)

    lam = lam_ref[0, 0]
    l = jnp.sum(l_ref[...], axis=0, keepdims=True)
    o_t = acc_ref[0] / l[:, :tq] - lam * (acc_ref[1] / l[:, tq:])
    o_n = o_t * lax.rsqrt(jnp.mean(o_t * o_t, axis=0, keepdims=True) + SUBLN_EPS) * sw_ref[...]
    o_n = o_n * (1.0 - lam_init)
    o_ref[...] = (o_n * _silu(z_ref[...].astype(F32))).astype(o_ref.dtype)


def _attn_call(lam, q_t, k_nat, v_t, z_t, ksq, sw, lam_init):
    dv = 2 * DIFF_D
    s = k_nat.shape[0]
    h = k_nat.shape[1] // dv
    tq = ATT_TQ
    assert ATT_TQ == ATT_TK
    return pl.pallas_call(
        functools.partial(_attn_kernel, lam_init=lam_init),
        grid=(h, s // tq),
        in_specs=[
            pl.BlockSpec(memory_space=pltpu.SMEM),
            pl.BlockSpec((dv, tq), lambda hh, i: (hh, i)),
            pl.BlockSpec((s, dv), lambda hh, i: (0, hh)),
            pl.BlockSpec((dv, s), lambda hh, i: (hh, 0)),
            pl.BlockSpec((dv, tq), lambda hh, i: (hh, i)),
            pl.BlockSpec((1, 2, s), lambda hh, i: (hh, 0, 0)),
            pl.BlockSpec((dv, tq), lambda hh, i: (0, 0)),
        ],
        out_specs=pl.BlockSpec((dv, tq), lambda hh, i: (hh, i)),
        out_shape=jax.ShapeDtypeStruct((h * dv, s), BF16),
        scratch_shapes=[
            pltpu.VMEM((2, dv, tq), F32),
            pltpu.VMEM((8, 2 * tq), F32),
            pltpu.VMEM((1, 2 * tq), F32),
            pltpu.VMEM((ATT_TK, 2 * tq), BF16),
        ],
        compiler_params=pltpu.CompilerParams(
            dimension_semantics=("arbitrary", "arbitrary"), vmem_limit_bytes=VMEM_LIMIT),
        name="attn",
    )(lam, q_t, k_nat, v_t, z_t, ksq, sw)


def _out_kernel(x_ref, oa_ref, ob_ref, w_ref, y_ref):
    o_t = jnp.concatenate([oa_ref[...], ob_ref[...]], axis=0)
    y = lax.dot_general(o_t, w_ref[...], TN_DIMS, preferred_element_type=F32)
    y_ref[...] = x_ref[...] + y


def _out_call(x2, oa_t, ob_t, w_out):
    s, d = x2.shape
    tm = OUT_TM
    return pl.pallas_call(
        _out_kernel,
        grid=(s // tm,),
        in_specs=[
            pl.BlockSpec((tm, d), lambda i: (i, 0)),
            pl.BlockSpec((oa_t.shape[0], tm), lambda i: (0, i)),
            pl.BlockSpec((ob_t.shape[0], tm), lambda i: (0, i)),
            pl.BlockSpec(w_out.shape, lambda i: (0, 0)),
        ],
        out_specs=pl.BlockSpec((tm, d), lambda i: (i, 0)),
        out_shape=jax.ShapeDtypeStruct((s, d), F32),
        compiler_params=pltpu.CompilerParams(
            dimension_semantics=("arbitrary",), vmem_limit_bytes=VMEM_LIMIT),
        name="out",
    )(x2, oa_t, ob_t, w_out)


def _layer(l, x2, cos_t, sin_t, w_norm, w_in, conv_w, a_log, dt_bias, gdn_norm_w, q_norm_w,
           k_norm_w, lambda_q1, lambda_k1, lambda_q2, lambda_k2, subln_w, w_out):
    s, d = x2.shape
    nqk = GDN_HEADS * GDN_D
    ba_lo = 4 * nqk
    ba_hi = ba_lo + 2 * GDN_HEADS
    wt = jnp.concatenate([w_in[:, :ba_lo], w_in[:, ba_hi:]], axis=1).T.astype(BF16)
    wba = w_in[:, ba_lo:ba_hi].T.astype(BF16)
    qw = jnp.broadcast_to(q_norm_w[:, None], (DIFF_D, PROJ_TM)).astype(F32)
    kw = jnp.broadcast_to(k_norm_w[:, None], (DIFF_D, PROJ_TM)).astype(F32)

    gdn_t, ba_t, qb_t, kb, ksq, vb_t, zb_t = _proj_call(
        x2, w_norm[None, :].astype(F32), wt, wba, cos_t, sin_t, qw, kw)

    cw = jnp.broadcast_to(conv_w.astype(F32)[:, :, None], (CONV_K, 3 * nqk, LANES))
    zeros4 = jnp.zeros((GDN_HEADS, LANES), F32)
    alog8 = jnp.concatenate([zeros4, jnp.broadcast_to(a_log.astype(F32)[:, None], (GDN_HEADS, LANES))], 0)
    dt8 = jnp.concatenate([zeros4, jnp.broadcast_to(dt_bias.astype(F32)[:, None], (GDN_HEADS, LANES))], 0)
    gw = jnp.broadcast_to(gdn_norm_w.astype(F32)[:, None], (GDN_D, LANES))
    oa_t = _gdn_call(gdn_t, ba_t, cw, alog8, dt8, gw)

    lam_init = 0.8 - 0.6 * math.exp(-0.3 * l)
    lam = (jnp.exp(jnp.sum(lambda_q1.astype(F32) * lambda_k1.astype(F32)))
           - jnp.exp(jnp.sum(lambda_q2.astype(F32) * lambda_k2.astype(F32))) + lam_init)
    dv = 2 * DIFF_D
    sw = jnp.broadcast_to(subln_w.astype(F32)[:, None], (dv, ATT_TQ))
    ob_t = _attn_call(lam.reshape(1, 1).astype(F32), qb_t, kb, vb_t, zb_t, ksq, sw, lam_init)

    return _out_call(x2, oa_t, ob_t, w_out.astype(BF16))


def kernel(x, w_norm, w_in, conv_w, a_log, dt_bias, gdn_norm_w, q_norm_w, k_norm_w,
           lambda_q1, lambda_k1, lambda_q2, lambda_k2, subln_w, w_out):
    b, s, d = x.shape
    assert b == 1
    inv_freq = ROPE_THETA ** (-jnp.arange(0, DIFF_D, 2, dtype=jnp.float32) / DIFF_D)
    ang = jnp.arange(s, dtype=jnp.float32)[:, None] * inv_freq[None, :]
    cos_t, sin_t = jnp.cos(ang).T, jnp.sin(ang).T
    x2 = x[0]
    for l in range(w_norm.shape[0]):
        x2 = _layer(l, x2, cos_t, sin_t, w_norm[l], w_in[l], conv_w[l], a_log[l], dt_bias[l],
                    gdn_norm_w[l], q_norm_w[l], k_norm_w[l], lambda_q1[l], lambda_k1[l],
                    lambda_q2[l], lambda_k2[l], subln_w[l], w_out[l])
    return x2[None]
```

```python
import functools
import math

import jax
import jax.numpy as jnp
from jax import lax
from jax.experimental import pallas as pl
from jax.experimental.pallas import tpu as pltpu

F32 = jnp.float32
BF16 = jnp.bfloat16

GDN_HEADS = 4
GDN_D = 128
CONV_K = 4
DIFF_HEADS = 4
DIFF_D = 64
ROPE_THETA = 10000.0
EPS = 1e-6
SUBLN_EPS = 1e-5

LANES = 128
CHUNK = LANES
NEG_BIG = -1e30
LOG2E = 1.4426950408889634
FROZEN_MAX_GAP = 40.0
NORM_SLACK = 1.01

PROJ_TM = 512
GDN_TB = 512
ATT_TQ = 512
ATT_TK = 512
ATT_UNROLL = 8
OUT_TM = 1024
VMEM_LIMIT = 56 * 1024 * 1024

NT_DIMS = (((1,), (1,)), ((), ()))
TN_DIMS = (((0,), (0,)), ((), ()))


def _sigmoid(v):
    return 1.0 / (1.0 + jnp.exp(-v))


def _silu(v):
    return v * _sigmoid(v)


def _proj_kernel(x_ref, wn_ref, wt_ref, wba_ref, cos_ref, sin_ref, qw_ref, kw_ref,
                 gdn_ref, ba_ref, qb_ref, kb_ref, ks_ref, vb_ref, zb_ref):
    x = x_ref[...]
    ms = jnp.mean(x * x, axis=-1, keepdims=True)
    hn = (x * lax.rsqrt(ms + EPS) * wn_ref[...]).astype(BF16)

    def proj_t(lo, n):
        return lax.dot_general(wt_ref[lo:lo + n, :], hn, NT_DIMS, preferred_element_type=F32)

    blk = 512
    q_raw = proj_t(4 * blk, blk)
    k_raw = proj_t(5 * blk, blk)

    cos = cos_ref[...]
    sin = sin_ref[...]
    half = DIFF_D // 2

    def norm_rope(t, w, scale):
        outs = []
        for g in range(t.shape[0] // DIFF_D):
            tg = t[g * DIFF_D:(g + 1) * DIFF_D, :]
            tg = tg * lax.rsqrt(jnp.mean(tg * tg, axis=0, keepdims=True) + EPS) * w
            t1, t2 = tg[:half, :], tg[half:, :]
            outs.append((t1 * cos - t2 * sin) * scale)
            outs.append((t2 * cos + t1 * sin) * scale)
        return jnp.concatenate(outs, axis=0)

    qb_ref[...] = norm_rope(q_raw, qw_ref[...], LOG2E / math.sqrt(DIFF_D)).astype(BF16)
    kt = norm_rope(k_raw, kw_ref[...], 1.0)
    for g in range(2 * DIFF_HEADS):
        kg = kt[g * DIFF_D:(g + 1) * DIFF_D, :]
        ks_ref[g // 2, g % 2:g % 2 + 1, :] = jnp.sum(kg * kg, axis=0, keepdims=True)
    kb_ref[...] = kt.T.astype(BF16)

    for b in range(4):
        gdn_ref[b * blk:(b + 1) * blk, :] = proj_t(b * blk, blk).astype(BF16)
    ba_ref[...] = lax.dot_general(wba_ref[...], hn, NT_DIMS, preferred_element_type=F32)
    vb_ref[...] = proj_t(6 * blk, blk).astype(BF16)
    zb_ref[...] = proj_t(7 * blk, blk).astype(BF16)


def _proj_call(x2, w_norm, wt, wba, cos_t, sin_t, qw, kw):
    s, d = x2.shape
    tm = PROJ_TM
    const = lambda shape: pl.BlockSpec(shape, lambda i: (0,) * len(shape))
    cols = lambda rows: pl.BlockSpec((rows, tm), lambda i: (0, i))
    return pl.pallas_call(
        _proj_kernel,
        grid=(s // tm,),
        in_specs=[
            pl.BlockSpec((tm, d), lambda i: (i, 0)),
            const((1, d)),
            const(wt.shape),
            const(wba.shape),
            cols(DIFF_D // 2),
            cols(DIFF_D // 2),
            const((DIFF_D, tm)),
            const((DIFF_D, tm)),
        ],
        out_specs=[
            cols(2048),
            cols(8),
            cols(512),
            pl.BlockSpec((tm, 512), lambda i: (i, 0)),
            pl.BlockSpec((DIFF_HEADS, 2, tm), lambda i: (0, 0, i)),
            cols(512),
            cols(512),
        ],
        out_shape=[
            jax.ShapeDtypeStruct((2048, s), BF16),
            jax.ShapeDtypeStruct((8, s), F32),
            jax.ShapeDtypeStruct((512, s), BF16),
            jax.ShapeDtypeStruct((s, 512), BF16),
            jax.ShapeDtypeStruct((DIFF_HEADS, 2, s), F32),
            jax.ShapeDtypeStruct((512, s), BF16),
            jax.ShapeDtypeStruct((512, s), BF16),
        ],
        compiler_params=pltpu.CompilerParams(
            dimension_semantics=("arbitrary",), vmem_limit_bytes=VMEM_LIMIT),
        name="proj",
    )(x2, w_norm, wt, wba, cos_t, sin_t, qw, kw)


def _gdn_kernel(g_ref, ba_ref, cw_ref, alog_ref, dt_ref, gw_ref, o_ref, halo_ref, st_ref):
    nqkv = 3 * GDN_HEADS * GDN_D
    n_chunks = g_ref.shape[1] // CHUNK

    @pl.when(pl.program_id(0) == 0)
    def _():
        halo_ref[...] = jnp.zeros_like(halo_ref)
        st_ref[...] = jnp.zeros_like(st_ref)

    lane = lax.broadcasted_iota(jnp.int32, (CHUNK, CHUNK), 1)
    subl = lax.broadcasted_iota(jnp.int32, (CHUNK, CHUNK), 0)
    eye = (lane == subl).astype(F32)
    lane8 = lax.broadcasted_iota(jnp.int32, (8, CHUNK), 1)
    lane_c = lax.broadcasted_iota(jnp.int32, (nqkv, CHUNK), 1)

    def rolls(v):
        return [pltpu.roll(v, j, axis=1) for j in range(1, CONV_K)]

    chains = []
    prev_rolls = rolls(halo_ref[...])
    for c in range(n_chunks):
        lanes = slice(c * CHUNK, (c + 1) * CHUNK)
        cur = g_ref[0:nqkv, lanes].astype(F32)
        cur_rolls = rolls(cur)
        if c == n_chunks - 1:
            halo_ref[...] = cur
        y = cw_ref[CONV_K - 1] * cur
        for j in range(1, CONV_K):
            shifted = jnp.where(lane_c >= j, cur_rolls[j - 1], prev_rolls[j - 1])
            y = y + cw_ref[CONV_K - 1 - j] * shifted
        prev_rolls = cur_rolls
        y = _silu(y)

        ba = ba_ref[:, lanes]
        beta8 = _sigmoid(ba)
        sp = ba + dt_ref[...]
        softplus = jnp.maximum(sp, 0.0) + jnp.log(1.0 + jnp.exp(-jnp.abs(sp)))
        gc8 = (-LOG2E) * jnp.exp(alog_ref[...]) * softplus
        sh = 1
        while sh < CHUNK:
            gc8 = gc8 + jnp.where(lane8 >= sh, pltpu.roll(gc8, sh, axis=1), 0.0)
            sh *= 2

        for h in range(GDN_HEADS):
            qt = y[h * GDN_D:(h + 1) * GDN_D, :]
            kt = y[(GDN_HEADS + h) * GDN_D:(GDN_HEADS + h + 1) * GDN_D, :]
            vt = y[(2 * GDN_HEADS + h) * GDN_D:(2 * GDN_HEADS + h + 1) * GDN_D, :]
            qt = qt * lax.rsqrt(jnp.sum(qt * qt, axis=0, keepdims=True) + EPS) * (GDN_D ** -0.5)
            kt = kt * lax.rsqrt(jnp.sum(kt * kt, axis=0, keepdims=True) + EPS)
            beta = beta8[h:h + 1, :]
            gc = gc8[GDN_HEADS + h:GDN_HEADS + h + 1, :]
            g_row = jnp.broadcast_to(gc, (CHUNK, CHUNK))
            g_col = g_row.T
            g_last = g_col[CHUNK - 1:CHUNK, :]
            e_gc = jnp.exp2(gc)
            chains.append(dict(
                c=c, h=h, qt=qt, kt=kt, beta=beta, g_last=g_last, g_col=g_col,
                decay_t=jnp.exp2(jnp.where(lane >= subl, g_row - g_col, NEG_BIG)),
                k_nat=kt.T,
                qg=qt * e_gc,
                rhs=jnp.concatenate([vt * beta, kt * (beta * e_gc)], axis=0).astype(BF16)))

    for ch in chains:
        gram = jnp.dot(ch["k_nat"].astype(BF16),
                       jnp.concatenate([ch["kt"], ch["qt"]], axis=1).astype(BF16),
                       preferred_element_type=F32)
        ch["a_t"] = gram[:, CHUNK:] * ch["decay_t"]
        ch["n"] = jnp.where(lane > subl, gram[:, :CHUNK] * ch["decay_t"] * (-ch["beta"]), 0.0)

    for ch in chains:
        nb = ch["n"].astype(BF16)
        ch["q"] = eye + ch["n"]
        ch["n"] = jnp.dot(nb, nb, preferred_element_type=F32)
    m = 2
    while 2 * m < CHUNK:
        for ch in chains:
            nb = ch["n"].astype(BF16)
            both = jnp.dot(jnp.concatenate([ch["q"].astype(BF16), nb], axis=0), nb,
                           preferred_element_type=F32)
            ch["q"] = ch["q"] + both[:CHUNK]
            ch["n"] = both[CHUNK:]
        m *= 2
    for ch in chains:
        ch["q"] = ch["q"] + jnp.dot(ch["q"].astype(BF16), ch["n"].astype(BF16),
                                    preferred_element_type=F32)

    for ch in chains:
        uw = jnp.dot(ch["rhs"], ch["q"].astype(BF16), preferred_element_type=F32)
        ch["u_t"] = uw[:GDN_D, :]
        ch["r1"] = jnp.concatenate([uw[GDN_D:, :], ch["qg"]], axis=1).astype(BF16)
        k_dec = ch["k_nat"] * jnp.exp2(ch["g_last"] - ch["g_col"])
        ch["r2"] = jnp.concatenate([k_dec, ch["a_t"]], axis=1).astype(BF16)

    states = [st_ref[h] for h in range(GDN_HEADS)]
    for c in range(n_chunks):
        lanes = slice(c * CHUNK, (c + 1) * CHUNK)
        row = chains[c * GDN_HEADS:(c + 1) * GDN_HEADS]
        x1 = [jnp.dot(states[h].astype(BF16), row[h]["r1"], preferred_element_type=F32)
              for h in range(GDN_HEADS)]
        x2 = [jnp.dot((row[h]["u_t"] - x1[h][:, :CHUNK]).astype(BF16), row[h]["r2"],
                      preferred_element_type=F32) for h in range(GDN_HEADS)]
        for h in range(GDN_HEADS):
            states[h] = states[h] * jnp.exp2(row[h]["g_last"]) + x2[h][:, :GDN_D]
            o_t = x1[h][:, CHUNK:] + x2[h][:, GDN_D:]
            o_n = o_t * lax.rsqrt(jnp.mean(o_t * o_t, axis=0, keepdims=True) + EPS) * gw_ref[...]
            z_t = g_ref[nqkv + h * GDN_D:nqkv + (h + 1) * GDN_D, lanes].astype(F32)
            o_ref[h * GDN_D:(h + 1) * GDN_D, lanes] = (o_n * _silu(z_t)).astype(o_ref.dtype)
    for h in range(GDN_HEADS):
        st_ref[h] = states[h]


def _gdn_call(gdn_t, ba_t, cw, alog8, dt8, gw):
    s = gdn_t.shape[1]
    tb = GDN_TB
    nqkv = 3 * GDN_HEADS * GDN_D
    const = lambda shape: pl.BlockSpec(shape, lambda i: (0,) * len(shape))
    return pl.pallas_call(
        _gdn_kernel,
        grid=(s // tb,),
        in_specs=[
            pl.BlockSpec((gdn_t.shape[0], tb), lambda i: (0, i)),
            pl.BlockSpec((8, tb), lambda i: (0, i)),
            const(cw.shape),
            const(alog8.shape),
            const(dt8.shape),
            const(gw.shape),
        ],
        out_specs=pl.BlockSpec((GDN_HEADS * GDN_D, tb), lambda i: (0, i)),
        out_shape=jax.ShapeDtypeStruct((GDN_HEADS * GDN_D, s), BF16),
        scratch_shapes=[
            pltpu.VMEM((nqkv, CHUNK), F32),
            pltpu.VMEM((GDN_HEADS, GDN_D, GDN_D), F32),
        ],
        compiler_params=pltpu.CompilerParams(
            dimension_semantics=("arbitrary",), vmem_limit_bytes=VMEM_LIMIT),
        name="gdn",
    )(gdn_t, ba_t, cw, alog8, dt8, gw)


def _attn_kernel(lam_ref, q_ref, k_ref, v_ref, z_ref, ks_ref, sw_ref, o_ref,
                 acc_ref, l_ref, m_ref, p_ref, *, lam_init):
    tq, tk = ATT_TQ, ATT_TK
    dv = 2 * DIFF_D
    i = pl.program_id(1)
    qt = q_ref[...]
    row = lax.broadcasted_iota(jnp.int32, qt.shape, 0)
    zero = jnp.zeros_like(qt)
    q_cat = jnp.concatenate(
        [jnp.where(row < DIFF_D, qt, zero), jnp.where(row >= DIFF_D, qt, zero)], axis=1)

    def scores(j):
        start = pl.multiple_of(j * tk, tk)
        return jnp.dot(k_ref[pl.ds(start, tk), :], q_cat, preferred_element_type=F32)

    def flush(j_pending, alpha):
        start = pl.multiple_of(j_pending * tk, tk)
        vt = v_ref[:, pl.ds(start, tk)]
        for c in range(2):
            upd = acc_ref[c] + jnp.dot(vt, p_ref[:, c * tq:(c + 1) * tq],
                                       preferred_element_type=F32)
            acc_ref[c] = upd if alpha is None else upd * alpha[:, c * tq:(c + 1) * tq]

    def exp_store(s_j, m):
        p = jnp.exp2(s_j - m)
        p_ref[...] = p.astype(BF16)
        return jnp.sum(p.reshape(tk // 8, 8, 2 * tq), axis=0)

    def pending(j):
        return jnp.where(j == 0, i, j - 1)

    qf = qt.astype(F32)
    qq = qf * qf
    qsq = jnp.concatenate([jnp.sum(qq[:DIFF_D], axis=0, keepdims=True),
                           jnp.sum(qq[DIFF_D:], axis=0, keepdims=True)], axis=1)
    kmax = jnp.max(ks_ref[0], axis=1, keepdims=True)
    kmax = jnp.concatenate([jnp.broadcast_to(kmax[0:1], (1, tq)),
                            jnp.broadcast_to(kmax[1:2], (1, tq))], axis=1)
    bound = jnp.sqrt(qsq * kmax) * NORM_SLACK

    s_t = scores(i)
    keep = (lax.broadcasted_iota(jnp.int32, (tk, tq), 0)
            <= lax.broadcasted_iota(jnp.int32, (tk, tq), 1))
    s_t = jnp.concatenate([jnp.where(keep, s_t[:, :tq], NEG_BIG),
                           jnp.where(keep, s_t[:, tq:], NEG_BIG)], axis=1)
    m0 = jnp.max(s_t, axis=0, keepdims=True)
    frozen_ok = jnp.max(bound - m0) <= FROZEN_MAX_GAP
    m_ref[...] = m0
    l_ref[...] = exp_store(s_t, m0)
    acc_ref[...] = jnp.zeros_like(acc_ref)

    def frozen_step(j, j_pending):
        s_j = scores(j)
        flush(j_pending, None)
        l_ref[...] += exp_store(s_j, m_ref[...])

    def frozen_steps(j0, n):
        for u in range(n):
            frozen_step(j0 + u, pending(j0) if u == 0 else j0 + u - 1)

    def frozen_group(t, carry):
        frozen_steps(ATT_UNROLL * t, ATT_UNROLL)
        return carry

    def online_body(j, carry):
        s_j = scores(j)
        m_old = m_ref[...]
        m_new = jnp.maximum(m_old, jnp.max(s_j, axis=0, keepdims=True))
        alpha = jnp.exp2(m_old - m_new)
        flush(pending(j), alpha)
        m_ref[...] = m_new
        l_ref[...] = l_ref[...] * alpha + exp_store(s_j, m_new)
        return carry

    @pl.when(frozen_ok)
    def _():
        groups = lax.shift_right_logical(i, ATT_UNROLL.bit_length() - 1)
        lax.fori_loop(0, groups, frozen_group, 0)
        j0 = groups * ATT_UNROLL
        n = ATT_UNROLL // 2
        while n >= 1:
            @pl.when((i & n) != 0)
            def _(j0=j0, n=n):
                frozen_steps(j0, n)
            j0 = j0 + (i & n)
            n //= 2

    @pl.when(jnp.logical_not(frozen_ok))
    def _():
        lax.fori_loop(0, i, online_body, 0)

    flush(pending(i), None)

    lam = lam_ref[0, 0]
    l = jnp.sum(l_ref[...], axis=0, keepdims=True)
    o_t = acc_ref[0] / l[:, :tq] - lam * (acc_ref[1] / l[:, tq:])
    o_n = o_t * lax.rsqrt(jnp.mean(o_t * o_t, axis=0, keepdims=True) + SUBLN_EPS) * sw_ref[...]
    o_n = o_n * (1.0 - lam_init)
    o_ref[...] = (o_n * _silu(z_ref[...].astype(F32))).astype(o_ref.dtype)


def _attn_call(lam, q_t, k_nat, v_t, z_t, ksq, sw, lam_init):
    dv = 2 * DIFF_D
    s = k_nat.shape[0]
    h = k_nat.shape[1] // dv
    tq = ATT_TQ
    assert ATT_TQ == ATT_TK
    return pl.pallas_call(
        functools.partial(_attn_kernel, lam_init=lam_init),
        grid=(h, s // tq),
        in_specs=[
            pl.BlockSpec(memory_space=pltpu.SMEM),
            pl.BlockSpec((dv, tq), lambda hh, i: (hh, i)),
            pl.BlockSpec((s, dv), lambda hh, i: (0, hh)),
            pl.BlockSpec((dv, s), lambda hh, i: (hh, 0)),
            pl.BlockSpec((dv, tq), lambda hh, i: (hh, i)),
            pl.BlockSpec((1, 2, s), lambda hh, i: (hh, 0, 0)),
            pl.BlockSpec((dv, tq), lambda hh, i: (0, 0)),
        ],
        out_specs=pl.BlockSpec((dv, tq), lambda hh, i: (hh, i)),
        out_shape=jax.ShapeDtypeStruct((h * dv, s), BF16),
        scratch_shapes=[
            pltpu.VMEM((2, dv, tq), F32),
            pltpu.VMEM((8, 2 * tq), F32),
            pltpu.VMEM((1, 2 * tq), F32),
            pltpu.VMEM((ATT_TK, 2 * tq), BF16),
        ],
        compiler_params=pltpu.CompilerParams(
            dimension_semantics=("arbitrary", "arbitrary"), vmem_limit_bytes=VMEM_LIMIT),
        name="attn",
    )(lam, q_t, k_nat, v_t, z_t, ksq, sw)


def _out_kernel(x_ref, oa_ref, ob_ref, w_ref, y_ref):
    o_t = jnp.concatenate([oa_ref[...], ob_ref[...]], axis=0)
    y = lax.dot_general(o_t, w_ref[...], TN_DIMS, preferred_element_type=F32)
    y_ref[...] = x_ref[...] + y


def _out_call(x2, oa_t, ob_t, w_out):
    s, d = x2.shape
    tm = OUT_TM
    return pl.pallas_call(
        _out_kernel,
        grid=(s // tm,),
        in_specs=[
            pl.BlockSpec((tm, d), lambda i: (i, 0)),
            pl.BlockSpec((oa_t.shape[0], tm), lambda i: (0, i)),
            pl.BlockSpec((ob_t.shape[0], tm), lambda i: (0, i)),
            pl.BlockSpec(w_out.shape, lambda i: (0, 0)),
        ],
        out_specs=pl.BlockSpec((tm, d), lambda i: (i, 0)),
        out_shape=jax.ShapeDtypeStruct((s, d), F32),
        compiler_params=pltpu.CompilerParams(
            dimension_semantics=("arbitrary",), vmem_limit_bytes=VMEM_LIMIT),
        name="out",
    )(x2, oa_t, ob_t, w_out)


def _layer(l, x2, cos_t, sin_t, w_norm, w_in, conv_w, a_log, dt_bias, gdn_norm_w, q_norm_w,
           k_norm_w, lambda_q1, lambda_k1, lambda_q2, lambda_k2, subln_w, w_out):
    s, d = x2.shape
    nqk = GDN_HEADS * GDN_D
    ba_lo = 4 * nqk
    ba_hi = ba_lo + 2 * GDN_HEADS
    wt = jnp.concatenate([w_in[:, :ba_lo], w_in[:, ba_hi:]], axis=1).T.astype(BF16)
    wba = w_in[:, ba_lo:ba_hi].T.astype(BF16)
    qw = jnp.broadcast_to(q_norm_w[:, None], (DIFF_D, PROJ_TM)).astype(F32)
    kw = jnp.broadcast_to(k_norm_w[:, None], (DIFF_D, PROJ_TM)).astype(F32)

    gdn_t, ba_t, qb_t, kb, ksq, vb_t, zb_t = _proj_call(
        x2, w_norm[None, :].astype(F32), wt, wba, cos_t, sin_t, qw, kw)

    cw = jnp.broadcast_to(conv_w.astype(F32)[:, :, None], (CONV_K, 3 * nqk, LANES))
    zeros4 = jnp.zeros((GDN_HEADS, LANES), F32)
    alog8 = jnp.concatenate([zeros4, jnp.broadcast_to(a_log.astype(F32)[:, None], (GDN_HEADS, LANES))], 0)
    dt8 = jnp.concatenate([zeros4, jnp.broadcast_to(dt_bias.astype(F32)[:, None], (GDN_HEADS, LANES))], 0)
    gw = jnp.broadcast_to(gdn_norm_w.astype(F32)[:, None], (GDN_D, LANES))
    oa_t = _gdn_call(gdn_t, ba_t, cw, alog8, dt8, gw)

    lam_init = 0.8 - 0.6 * math.exp(-0.3 * l)
    lam = (jnp.exp(jnp.sum(lambda_q1.astype(F32) * lambda_k1.astype(F32)))
           - jnp.exp(jnp.sum(lambda_q2.astype(F32) * lambda_k2.astype(F32))) + lam_init)
    dv = 2 * DIFF_D
    sw = jnp.broadcast_to(subln_w.astype(F32)[:, None], (dv, ATT_TQ))
    ob_t = _attn_call(lam.reshape(1, 1).astype(F32), qb_t, kb, vb_t, zb_t, ksq, sw, lam_init)

    return _out_call(x2, oa_t, ob_t, w_out.astype(BF16))


def kernel(x, w_norm, w_in, conv_w, a_log, dt_bias, gdn_norm_w, q_norm_w, k_norm_w,
           lambda_q1, lambda_k1, lambda_q2, lambda_k2, subln_w, w_out):
    b, s, d = x.shape
    assert b == 1
    inv_freq = ROPE_THETA ** (-jnp.arange(0, DIFF_D, 2, dtype=jnp.float32) / DIFF_D)
    ang = jnp.arange(s, dtype=jnp.float32)[:, None] * inv_freq[None, :]
    cos_t, sin_t = jnp.cos(ang).T, jnp.sin(ang).T
    x2 = x[0]
    for l in range(w_norm.shape[0]):
        x2 = _layer(l, x2, cos_t, sin_t, w_norm[l], w_in[l], conv_w[l], a_log[l], dt_bias[l],
                    gdn_norm_w[l], q_norm_w[l], k_norm_w[l], lambda_q1[l], lambda_k1[l],
                    lambda_q2[l], lambda_k2[l], subln_w[l], w_out[l])
    return x2[None]
```

```python
import functools
import math

import jax
import jax.numpy as jnp
from jax import lax
from jax.experimental import pallas as pl
from jax.experimental.pallas import tpu as pltpu

F32 = jnp.float32
BF16 = jnp.bfloat16

GDN_HEADS = 4
GDN_D = 128
CONV_K = 4
DIFF_HEADS = 4
DIFF_D = 64
ROPE_THETA = 10000.0
EPS = 1e-6
SUBLN_EPS = 1e-5

LANES = 128
CHUNK = LANES
NEG_BIG = -1e30
LOG2E = 1.4426950408889634
FROZEN_MAX_GAP = 40.0
NORM_SLACK = 1.01

PROJ_TM = 512
GDN_TB = 512
ATT_TQ = 512
ATT_TK = 512
ATT_UNROLL = 8
ATT_SUB = 2
OUT_TM = 1024
VMEM_LIMIT = 56 * 1024 * 1024

NT_DIMS = (((1,), (1,)), ((), ()))
TN_DIMS = (((0,), (0,)), ((), ()))


def _sigmoid(v):
    return 1.0 / (1.0 + jnp.exp(-v))


def _silu(v):
    return v * _sigmoid(v)


def _proj_kernel(x_ref, wn_ref, wt_ref, wba_ref, cos_ref, sin_ref, qw_ref, kw_ref,
                 gdn_ref, ba_ref, qb_ref, kb_ref, ks_ref, vb_ref, zb_ref):
    x = x_ref[...]
    ms = jnp.mean(x * x, axis=-1, keepdims=True)
    hn = (x * lax.rsqrt(ms + EPS) * wn_ref[...]).astype(BF16)

    def proj_t(lo, n):
        return lax.dot_general(wt_ref[lo:lo + n, :], hn, NT_DIMS, preferred_element_type=F32)

    blk = 512
    q_raw = proj_t(4 * blk, blk)
    k_raw = proj_t(5 * blk, blk)

    cos = cos_ref[...]
    sin = sin_ref[...]
    half = DIFF_D // 2

    def norm_rope(t, w, scale):
        outs = []
        for g in range(t.shape[0] // DIFF_D):
            tg = t[g * DIFF_D:(g + 1) * DIFF_D, :]
            tg = tg * lax.rsqrt(jnp.mean(tg * tg, axis=0, keepdims=True) + EPS) * w
            t1, t2 = tg[:half, :], tg[half:, :]
            outs.append((t1 * cos - t2 * sin) * scale)
            outs.append((t2 * cos + t1 * sin) * scale)
        return jnp.concatenate(outs, axis=0)

    qb_ref[...] = norm_rope(q_raw, qw_ref[...], LOG2E / math.sqrt(DIFF_D)).astype(BF16)
    kt = norm_rope(k_raw, kw_ref[...], 1.0)
    for g in range(2 * DIFF_HEADS):
        kg = kt[g * DIFF_D:(g + 1) * DIFF_D, :]
        ks_ref[g // 2, g % 2:g % 2 + 1, :] = jnp.sum(kg * kg, axis=0, keepdims=True)
    kb_ref[...] = kt.T.astype(BF16)

    for b in range(4):
        gdn_ref[b * blk:(b + 1) * blk, :] = proj_t(b * blk, blk).astype(BF16)
    ba_ref[...] = lax.dot_general(wba_ref[...], hn, NT_DIMS, preferred_element_type=F32)
    vb_ref[...] = proj_t(6 * blk, blk).astype(BF16)
    zb_ref[...] = proj_t(7 * blk, blk).astype(BF16)


def _proj_call(x2, w_norm, wt, wba, cos_t, sin_t, qw, kw):
    s, d = x2.shape
    tm = PROJ_TM
    const = lambda shape: pl.BlockSpec(shape, lambda i: (0,) * len(shape))
    cols = lambda rows: pl.BlockSpec((rows, tm), lambda i: (0, i))
    return pl.pallas_call(
        _proj_kernel,
        grid=(s // tm,),
        in_specs=[
            pl.BlockSpec((tm, d), lambda i: (i, 0)),
            const((1, d)),
            const(wt.shape),
            const(wba.shape),
            cols(DIFF_D // 2),
            cols(DIFF_D // 2),
            const((DIFF_D, tm)),
            const((DIFF_D, tm)),
        ],
        out_specs=[
            cols(2048),
            cols(8),
            cols(512),
            pl.BlockSpec((tm, 512), lambda i: (i, 0)),
            pl.BlockSpec((DIFF_HEADS, 2, tm), lambda i: (0, 0, i)),
            cols(512),
            cols(512),
        ],
        out_shape=[
            jax.ShapeDtypeStruct((2048, s), BF16),
            jax.ShapeDtypeStruct((8, s), F32),
            jax.ShapeDtypeStruct((512, s), BF16),
            jax.ShapeDtypeStruct((s, 512), BF16),
            jax.ShapeDtypeStruct((DIFF_HEADS, 2, s), F32),
            jax.ShapeDtypeStruct((512, s), BF16),
            jax.ShapeDtypeStruct((512, s), BF16),
        ],
        compiler_params=pltpu.CompilerParams(
            dimension_semantics=("arbitrary",), vmem_limit_bytes=VMEM_LIMIT),
        name="proj",
    )(x2, w_norm, wt, wba, cos_t, sin_t, qw, kw)


def _gdn_kernel(g_ref, ba_ref, cw_ref, alog_ref, dt_ref, gw_ref, o_ref, halo_ref, st_ref):
    nqkv = 3 * GDN_HEADS * GDN_D
    n_chunks = g_ref.shape[1] // CHUNK

    @pl.when(pl.program_id(0) == 0)
    def _():
        halo_ref[...] = jnp.zeros_like(halo_ref)
        st_ref[...] = jnp.zeros_like(st_ref)

    lane = lax.broadcasted_iota(jnp.int32, (CHUNK, CHUNK), 1)
    subl = lax.broadcasted_iota(jnp.int32, (CHUNK, CHUNK), 0)
    eye = (lane == subl).astype(F32)
    lane8 = lax.broadcasted_iota(jnp.int32, (8, CHUNK), 1)
    lane_c = lax.broadcasted_iota(jnp.int32, (nqkv, CHUNK), 1)

    def rolls(v):
        return [pltpu.roll(v, j, axis=1) for j in range(1, CONV_K)]

    chains = []
    prev_rolls = rolls(halo_ref[...])
    for c in range(n_chunks):
        lanes = slice(c * CHUNK, (c + 1) * CHUNK)
        cur = g_ref[0:nqkv, lanes].astype(F32)
        cur_rolls = rolls(cur)
        if c == n_chunks - 1:
            halo_ref[...] = cur
        y = cw_ref[CONV_K - 1] * cur
        for j in range(1, CONV_K):
            shifted = jnp.where(lane_c >= j, cur_rolls[j - 1], prev_rolls[j - 1])
            y = y + cw_ref[CONV_K - 1 - j] * shifted
        prev_rolls = cur_rolls
        y = _silu(y)

        ba = ba_ref[:, lanes]
        beta8 = _sigmoid(ba)
        sp = ba + dt_ref[...]
        softplus = jnp.maximum(sp, 0.0) + jnp.log(1.0 + jnp.exp(-jnp.abs(sp)))
        gc8 = (-LOG2E) * jnp.exp(alog_ref[...]) * softplus
        sh = 1
        while sh < CHUNK:
            gc8 = gc8 + jnp.where(lane8 >= sh, pltpu.roll(gc8, sh, axis=1), 0.0)
            sh *= 2

        for h in range(GDN_HEADS):
            qt = y[h * GDN_D:(h + 1) * GDN_D, :]
            kt = y[(GDN_HEADS + h) * GDN_D:(GDN_HEADS + h + 1) * GDN_D, :]
            vt = y[(2 * GDN_HEADS + h) * GDN_D:(2 * GDN_HEADS + h + 1) * GDN_D, :]
            qt = qt * lax.rsqrt(jnp.sum(qt * qt, axis=0, keepdims=True) + EPS) * (GDN_D ** -0.5)
            kt = kt * lax.rsqrt(jnp.sum(kt * kt, axis=0, keepdims=True) + EPS)
            beta = beta8[h:h + 1, :]
            gc = gc8[GDN_HEADS + h:GDN_HEADS + h + 1, :]
            g_row = jnp.broadcast_to(gc, (CHUNK, CHUNK))
            g_col = g_row.T
            g_last = g_col[CHUNK - 1:CHUNK, :]
            e_gc = jnp.exp2(gc)
            chains.append(dict(
                c=c, h=h, qt=qt, kt=kt, beta=beta, g_last=g_last, g_col=g_col,
                decay_t=jnp.exp2(jnp.where(lane >= subl, g_row - g_col, NEG_BIG)),
                k_nat=kt.T,
                qg=qt * e_gc,
                rhs=jnp.concatenate([vt * beta, kt * (beta * e_gc)], axis=0).astype(BF16)))

    for ch in chains:
        gram = jnp.dot(ch["k_nat"].astype(BF16),
                       jnp.concatenate([ch["kt"], ch["qt"]], axis=1).astype(BF16),
                       preferred_element_type=F32)
        ch["a_t"] = gram[:, CHUNK:] * ch["decay_t"]
        ch["n"] = jnp.where(lane > subl, gram[:, :CHUNK] * ch["decay_t"] * (-ch["beta"]), 0.0)

    for ch in chains:
        nb = ch["n"].astype(BF16)
        ch["q"] = eye + ch["n"]
        ch["n"] = jnp.dot(nb, nb, preferred_element_type=F32)
    m = 2
    while 2 * m < CHUNK:
        for ch in chains:
            nb = ch["n"].astype(BF16)
            both = jnp.dot(jnp.concatenate([ch["q"].astype(BF16), nb], axis=0), nb,
                           preferred_element_type=F32)
            ch["q"] = ch["q"] + both[:CHUNK]
            ch["n"] = both[CHUNK:]
        m *= 2
    for ch in chains:
        ch["q"] = ch["q"] + jnp.dot(ch["q"].astype(BF16), ch["n"].astype(BF16),
                                    preferred_element_type=F32)

    for ch in chains:
        uw = jnp.dot(ch["rhs"], ch["q"].astype(BF16), preferred_element_type=F32)
        ch["u_t"] = uw[:GDN_D, :]
        ch["r1"] = jnp.concatenate([uw[GDN_D:, :], ch["qg"]], axis=1).astype(BF16)
        k_dec = ch["k_nat"] * jnp.exp2(ch["g_last"] - ch["g_col"])
        ch["r2"] = jnp.concatenate([k_dec, ch["a_t"]], axis=1).astype(BF16)

    states = [st_ref[h] for h in range(GDN_HEADS)]
    for c in range(n_chunks):
        lanes = slice(c * CHUNK, (c + 1) * CHUNK)
        row = chains[c * GDN_HEADS:(c + 1) * GDN_HEADS]
        x1 = [jnp.dot(states[h].astype(BF16), row[h]["r1"], preferred_element_type=F32)
              for h in range(GDN_HEADS)]
        x2 = [jnp.dot((row[h]["u_t"] - x1[h][:, :CHUNK]).astype(BF16), row[h]["r2"],
                      preferred_element_type=F32) for h in range(GDN_HEADS)]
        for h in range(GDN_HEADS):
            states[h] = states[h] * jnp.exp2(row[h]["g_last"]) + x2[h][:, :GDN_D]
            o_t = x1[h][:, CHUNK:] + x2[h][:, GDN_D:]
            o_n = o_t * lax.rsqrt(jnp.mean(o_t * o_t, axis=0, keepdims=True) + EPS) * gw_ref[...]
            z_t = g_ref[nqkv + h * GDN_D:nqkv + (h + 1) * GDN_D, lanes].astype(F32)
            o_ref[h * GDN_D:(h + 1) * GDN_D, lanes] = (o_n * _silu(z_t)).astype(o_ref.dtype)
    for h in range(GDN_HEADS):
        st_ref[h] = states[h]


def _gdn_call(gdn_t, ba_t, cw, alog8, dt8, gw):
    s = gdn_t.shape[1]
    tb = GDN_TB
    nqkv = 3 * GDN_HEADS * GDN_D
    const = lambda shape: pl.BlockSpec(shape, lambda i: (0,) * len(shape))
    return pl.pallas_call(
        _gdn_kernel,
        grid=(s // tb,),
        in_specs=[
            pl.BlockSpec((gdn_t.shape[0], tb), lambda i: (0, i)),
            pl.BlockSpec((8, tb), lambda i: (0, i)),
            const(cw.shape),
            const(alog8.shape),
            const(dt8.shape),
            const(gw.shape),
        ],
        out_specs=pl.BlockSpec((GDN_HEADS * GDN_D, tb), lambda i: (0, i)),
        out_shape=jax.ShapeDtypeStruct((GDN_HEADS * GDN_D, s), BF16),
        scratch_shapes=[
            pltpu.VMEM((nqkv, CHUNK), F32),
            pltpu.VMEM((GDN_HEADS, GDN_D, GDN_D), F32),
        ],
        compiler_params=pltpu.CompilerParams(
            dimension_semantics=("arbitrary",), vmem_limit_bytes=VMEM_LIMIT),
        name="gdn",
    )(gdn_t, ba_t, cw, alog8, dt8, gw)


def _attn_kernel(lam_ref, q_ref, k_ref, v_ref, z_ref, ks_ref, sw_ref, o_ref,
                 acc_ref, l_ref, m_ref, p_ref, *, lam_init):
    for sub in range(ATT_SUB):
        _attn_tile(sub, lam_ref, q_ref, k_ref, v_ref, z_ref, ks_ref, sw_ref, o_ref,
                   acc_ref.at[sub], l_ref.at[sub], m_ref, p_ref, lam_init)


def _attn_tile(sub, lam_ref, q_ref, k_ref, v_ref, z_ref, ks_ref, sw_ref, o_ref,
               acc_ref, l_ref, m_ref, p_ref, lam_init):
    tq, tk = ATT_TQ, ATT_TK
    dv = 2 * DIFF_D
    cols = slice(sub * tq, (sub + 1) * tq)
    i = pl.program_id(1) * ATT_SUB + sub
    qt = q_ref[:, cols]
    row = lax.broadcasted_iota(jnp.int32, qt.shape, 0)
    zero = jnp.zeros_like(qt)
    q_cat = jnp.concatenate(
        [jnp.where(row < DIFF_D, qt, zero), jnp.where(row >= DIFF_D, qt, zero)], axis=1)

    def scores(j):
        start = pl.multiple_of(j * tk, tk)
        return jnp.dot(k_ref[pl.ds(start, tk), :], q_cat, preferred_element_type=F32)

    def flush(j_pending, alpha):
        start = pl.multiple_of(j_pending * tk, tk)
        vt = v_ref[:, pl.ds(start, tk)]
        for c in range(2):
            upd = acc_ref[c] + jnp.dot(vt, p_ref[:, c * tq:(c + 1) * tq],
                                       preferred_element_type=F32)
            acc_ref[c] = upd if alpha is None else upd * alpha[:, c * tq:(c + 1) * tq]

    def exp_store(s_j, m):
        p = jnp.exp2(s_j - m)
        p_ref[...] = p.astype(BF16)
        return jnp.sum(p.reshape(tk // 8, 8, 2 * tq), axis=0)

    def pending(j):
        return jnp.where(j == 0, i, j - 1)

    qf = qt.astype(F32)
    qq = qf * qf
    qsq = jnp.concatenate([jnp.sum(qq[:DIFF_D], axis=0, keepdims=True),
                           jnp.sum(qq[DIFF_D:], axis=0, keepdims=True)], axis=1)
    kmax = jnp.max(ks_ref[0], axis=1, keepdims=True)
    kmax = jnp.concatenate([jnp.broadcast_to(kmax[0:1], (1, tq)),
                            jnp.broadcast_to(kmax[1:2], (1, tq))], axis=1)
    bound = jnp.sqrt(qsq * kmax) * NORM_SLACK

    s_t = scores(i)
    keep = (lax.broadcasted_iota(jnp.int32, (tk, tq), 0)
            <= lax.broadcasted_iota(jnp.int32, (tk, tq), 1))
    s_t = jnp.concatenate([jnp.where(keep, s_t[:, :tq], NEG_BIG),
                           jnp.where(keep, s_t[:, tq:], NEG_BIG)], axis=1)
    m0 = jnp.max(s_t, axis=0, keepdims=True)
    frozen_ok = jnp.max(bound - m0) <= FROZEN_MAX_GAP
    m_ref[...] = m0
    l_ref[...] = exp_store(s_t, m0)
    acc_ref[...] = jnp.zeros_like(acc_ref)

    def frozen_step(j, j_pending):
        s_j = scores(j)
        flush(j_pending, None)
        l_ref[...] += exp_store(s_j, m_ref[...])

    def frozen_steps(j0, n):
        for u in range(n):
            frozen_step(j0 + u, pending(j0) if u == 0 else j0 + u - 1)

    def frozen_group(t, carry):
        frozen_steps(ATT_UNROLL * t, ATT_UNROLL)
        return carry

    def online_body(j, carry):
        s_j = scores(j)
        m_old = m_ref[...]
        m_new = jnp.maximum(m_old, jnp.max(s_j, axis=0, keepdims=True))
        alpha = jnp.exp2(m_old - m_new)
        flush(pending(j), alpha)
        m_ref[...] = m_new
        l_ref[...] = l_ref[...] * alpha + exp_store(s_j, m_new)
        return carry

    @pl.when(frozen_ok)
    def _():
        groups = lax.shift_right_logical(i, ATT_UNROLL.bit_length() - 1)
        lax.fori_loop(0, groups, frozen_group, 0)
        j0 = groups * ATT_UNROLL
        n = ATT_UNROLL // 2
        while n >= 1:
            @pl.when((i & n) != 0)
            def _(j0=j0, n=n):
                frozen_steps(j0, n)
            j0 = j0 + (i & n)
            n //= 2

    @pl.when(jnp.logical_not(frozen_ok))
    def _():
        lax.fori_loop(0, i, online_body, 0)

    flush(pending(i), None)

    lam = lam_ref[0, 0]
    l = jnp.sum(l_ref[...], axis=0, keepdims=True)
    o_t = acc_ref[0] / l[:, :tq] - lam * (acc_ref[1] / l[:, tq:])
    o_n = o_t * lax.rsqrt(jnp.mean(o_t * o_t, axis=0, keepdims=True) + SUBLN_EPS) * sw_ref[...]
    o_n = o_n * (1.0 - lam_init)
    o_ref[:, cols] = (o_n * _silu(z_ref[:, cols].astype(F32))).astype(o_ref.dtype)


def _attn_call(lam, q_t, k_nat, v_t, z_t, ksq, sw, lam_init):
    dv = 2 * DIFF_D
    s = k_nat.shape[0]
    h = k_nat.shape[1] // dv
    tq = ATT_TQ
    tg = ATT_SUB * tq
    assert ATT_TQ == ATT_TK
    return pl.pallas_call(
        functools.partial(_attn_kernel, lam_init=lam_init),
        grid=(h, s // tg),
        in_specs=[
            pl.BlockSpec(memory_space=pltpu.SMEM),
            pl.BlockSpec((dv, tg), lambda hh, i: (hh, i)),
            pl.BlockSpec((s, dv), lambda hh, i: (0, hh)),
            pl.BlockSpec((dv, s), lambda hh, i: (hh, 0)),
            pl.BlockSpec((dv, tg), lambda hh, i: (hh, i)),
            pl.BlockSpec((1, 2, s), lambda hh, i: (hh, 0, 0)),
            pl.BlockSpec((dv, tq), lambda hh, i: (0, 0)),
        ],
        out_specs=pl.BlockSpec((dv, tg), lambda hh, i: (hh, i)),
        out_shape=jax.ShapeDtypeStruct((h * dv, s), BF16),
        scratch_shapes=[
            pltpu.VMEM((ATT_SUB, 2, dv, tq), F32),
            pltpu.VMEM((ATT_SUB, 8, 2 * tq), F32),
            pltpu.VMEM((1, 2 * tq), F32),
            pltpu.VMEM((ATT_TK, 2 * tq), BF16),
        ],
        compiler_params=pltpu.CompilerParams(
            dimension_semantics=("arbitrary", "arbitrary"), vmem_limit_bytes=VMEM_LIMIT),
        name="attn",
    )(lam, q_t, k_nat, v_t, z_t, ksq, sw)


def _out_kernel(x_ref, oa_ref, ob_ref, w_ref, y_ref):
    o_t = jnp.concatenate([oa_ref[...], ob_ref[...]], axis=0)
    y = lax.dot_general(o_t, w_ref[...], TN_DIMS, preferred_element_type=F32)
    y_ref[...] = x_ref[...] + y


def _out_call(x2, oa_t, ob_t, w_out):
    s, d = x2.shape
    tm = OUT_TM
    return pl.pallas_call(
        _out_kernel,
        grid=(s // tm,),
        in_specs=[
            pl.BlockSpec((tm, d), lambda i: (i, 0)),
            pl.BlockSpec((oa_t.shape[0], tm), lambda i: (0, i)),
            pl.BlockSpec((ob_t.shape[0], tm), lambda i: (0, i)),
            pl.BlockSpec(w_out.shape, lambda i: (0, 0)),
        ],
        out_specs=pl.BlockSpec((tm, d), lambda i: (i, 0)),
        out_shape=jax.ShapeDtypeStruct((s, d), F32),
        compiler_params=pltpu.CompilerParams(
            dimension_semantics=("arbitrary",), vmem_limit_bytes=VMEM_LIMIT),
        name="out",
    )(x2, oa_t, ob_t, w_out)


def _layer(l, x2, cos_t, sin_t, w_norm, w_in, conv_w, a_log, dt_bias, gdn_norm_w, q_norm_w,
           k_norm_w, lambda_q1, lambda_k1, lambda_q2, lambda_k2, subln_w, w_out):
    s, d = x2.shape
    nqk = GDN_HEADS * GDN_D
    ba_lo = 4 * nqk
    ba_hi = ba_lo + 2 * GDN_HEADS
    wt = jnp.concatenate([w_in[:, :ba_lo], w_in[:, ba_hi:]], axis=1).T.astype(BF16)
    wba = w_in[:, ba_lo:ba_hi].T.astype(BF16)
    qw = jnp.broadcast_to(q_norm_w[:, None], (DIFF_D, PROJ_TM)).astype(F32)
    kw = jnp.broadcast_to(k_norm_w[:, None], (DIFF_D, PROJ_TM)).astype(F32)

    gdn_t, ba_t, qb_t, kb, ksq, vb_t, zb_t = _proj_call(
        x2, w_norm[None, :].astype(F32), wt, wba, cos_t, sin_t, qw, kw)

    cw = jnp.broadcast_to(conv_w.astype(F32)[:, :, None], (CONV_K, 3 * nqk, LANES))
    zeros4 = jnp.zeros((GDN_HEADS, LANES), F32)
    alog8 = jnp.concatenate([zeros4, jnp.broadcast_to(a_log.astype(F32)[:, None], (GDN_HEADS, LANES))], 0)
    dt8 = jnp.concatenate([zeros4, jnp.broadcast_to(dt_bias.astype(F32)[:, None], (GDN_HEADS, LANES))], 0)
    gw = jnp.broadcast_to(gdn_norm_w.astype(F32)[:, None], (GDN_D, LANES))
    oa_t = _gdn_call(gdn_t, ba_t, cw, alog8, dt8, gw)

    lam_init = 0.8 - 0.6 * math.exp(-0.3 * l)
    lam = (jnp.exp(jnp.sum(lambda_q1.astype(F32) * lambda_k1.astype(F32)))
           - jnp.exp(jnp.sum(lambda_q2.astype(F32) * lambda_k2.astype(F32))) + lam_init)
    dv = 2 * DIFF_D
    sw = jnp.broadcast_to(subln_w.astype(F32)[:, None], (dv, ATT_TQ))
    ob_t = _attn_call(lam.reshape(1, 1).astype(F32), qb_t, kb, vb_t, zb_t, ksq, sw, lam_init)

    return _out_call(x2, oa_t, ob_t, w_out.astype(BF16))


def kernel(x, w_norm, w_in, conv_w, a_log, dt_bias, gdn_norm_w, q_norm_w, k_norm_w,
           lambda_q1, lambda_k1, lambda_q2, lambda_k2, subln_w, w_out):
    b, s, d = x.shape
    assert b == 1
    inv_freq = ROPE_THETA ** (-jnp.arange(0, DIFF_D, 2, dtype=jnp.float32) / DIFF_D)
    ang = jnp.arange(s, dtype=jnp.float32)[:, None] * inv_freq[None, :]
    cos_t, sin_t = jnp.cos(ang).T, jnp.sin(ang).T
    x2 = x[0]
    for l in range(w_norm.shape[0]):
        x2 = _layer(l, x2, cos_t, sin_t, w_norm[l], w_in[l], conv_w[l], a_log[l], dt_bias[l],
                    gdn_norm_w[l], q_norm_w[l], k_norm_w[l], lambda_q1[l], lambda_k1[l],
                    lambda_q2[l], lambda_k2[l], subln_w[l], w_out[l])
    return x2[None]
```

```python
import functools
import math

import jax
import jax.numpy as jnp
from jax import lax
from jax.experimental import pallas as pl
from jax.experimental.pallas import tpu as pltpu

F32 = jnp.float32
BF16 = jnp.bfloat16

GDN_HEADS = 4
GDN_D = 128
CONV_K = 4
DIFF_HEADS = 4
DIFF_D = 64
ROPE_THETA = 10000.0
EPS = 1e-6
SUBLN_EPS = 1e-5

LANES = 128
CHUNK = LANES
NEG_BIG = -1e30
LOG2E = 1.4426950408889634
FROZEN_MAX_GAP = 40.0
NORM_SLACK = 1.01

PROJ_TM = 512
GDN_TB = 512
ATT_TQ = 512
ATT_TK = 512
ATT_UNROLL = 8
ATT_SUB = 2
OUT_PIECE = 256
VMEM_LIMIT = 56 * 1024 * 1024

NT_DIMS = (((1,), (1,)), ((), ()))
TN_DIMS = (((0,), (0,)), ((), ()))


def _sigmoid(v):
    return 1.0 / (1.0 + jnp.exp(-v))


def _silu(v):
    return v * _sigmoid(v)


def _proj_kernel(x_ref, wn_ref, wt_ref, wba_ref, cos_ref, sin_ref, qw_ref, kw_ref,
                 gdn_ref, ba_ref, qb_ref, kb_ref, ks_ref, vb_ref, zb_ref):
    x = x_ref[...]
    ms = jnp.mean(x * x, axis=-1, keepdims=True)
    hn = (x * lax.rsqrt(ms + EPS) * wn_ref[...]).astype(BF16)

    def proj_t(lo, n):
        return lax.dot_general(wt_ref[lo:lo + n, :], hn, NT_DIMS, preferred_element_type=F32)

    blk = 512
    q_raw = proj_t(4 * blk, blk)
    k_raw = proj_t(5 * blk, blk)

    cos = cos_ref[...]
    sin = sin_ref[...]
    half = DIFF_D // 2

    def norm_rope(t, w, scale):
        outs = []
        for g in range(t.shape[0] // DIFF_D):
            tg = t[g * DIFF_D:(g + 1) * DIFF_D, :]
            tg = tg * lax.rsqrt(jnp.mean(tg * tg, axis=0, keepdims=True) + EPS) * w
            t1, t2 = tg[:half, :], tg[half:, :]
            outs.append((t1 * cos - t2 * sin) * scale)
            outs.append((t2 * cos + t1 * sin) * scale)
        return jnp.concatenate(outs, axis=0)

    qb_ref[...] = norm_rope(q_raw, qw_ref[...], LOG2E / math.sqrt(DIFF_D)).astype(BF16)
    kt = norm_rope(k_raw, kw_ref[...], 1.0)
    for g in range(2 * DIFF_HEADS):
        kg = kt[g * DIFF_D:(g + 1) * DIFF_D, :]
        ks_ref[g // 2, g % 2:g % 2 + 1, :] = jnp.sum(kg * kg, axis=0, keepdims=True)
    kb_ref[...] = kt.T.astype(BF16)

    for b in range(4):
        gdn_ref[b * blk:(b + 1) * blk, :] = proj_t(b * blk, blk).astype(BF16)
    ba_ref[...] = lax.dot_general(wba_ref[...], hn, NT_DIMS, preferred_element_type=F32)
    vb_ref[...] = proj_t(6 * blk, blk).astype(BF16)
    zb_ref[...] = proj_t(7 * blk, blk).astype(BF16)


def _proj_call(x2, w_norm, wt, wba, cos_t, sin_t, qw, kw):
    s, d = x2.shape
    tm = PROJ_TM
    const = lambda shape: pl.BlockSpec(shape, lambda i: (0,) * len(shape))
    cols = lambda rows: pl.BlockSpec((rows, tm), lambda i: (0, i))
    return pl.pallas_call(
        _proj_kernel,
        grid=(s // tm,),
        in_specs=[
            pl.BlockSpec((tm, d), lambda i: (i, 0)),
            const((1, d)),
            const(wt.shape),
            const(wba.shape),
            cols(DIFF_D // 2),
            cols(DIFF_D // 2),
            const((DIFF_D, tm)),
            const((DIFF_D, tm)),
        ],
        out_specs=[
            cols(2048),
            cols(8),
            cols(512),
            pl.BlockSpec((tm, 512), lambda i: (i, 0)),
            pl.BlockSpec((DIFF_HEADS, 2, tm), lambda i: (0, 0, i)),
            cols(512),
            cols(512),
        ],
        out_shape=[
            jax.ShapeDtypeStruct((2048, s), BF16),
            jax.ShapeDtypeStruct((8, s), F32),
            jax.ShapeDtypeStruct((512, s), BF16),
            jax.ShapeDtypeStruct((s, 512), BF16),
            jax.ShapeDtypeStruct((DIFF_HEADS, 2, s), F32),
            jax.ShapeDtypeStruct((512, s), BF16),
            jax.ShapeDtypeStruct((512, s), BF16),
        ],
        compiler_params=pltpu.CompilerParams(
            dimension_semantics=("arbitrary",), vmem_limit_bytes=VMEM_LIMIT),
        name="proj",
    )(x2, w_norm, wt, wba, cos_t, sin_t, qw, kw)


def _gdn_kernel(g_ref, ba_ref, cw_ref, alog_ref, dt_ref, gw_ref, x_ref, ob_ref, wo_ref,
                y_ref, halo_ref, st_ref):
    nqkv = 3 * GDN_HEADS * GDN_D
    n_chunks = g_ref.shape[1] // CHUNK
    n_a = GDN_HEADS * GDN_D

    @pl.when(pl.program_id(0) == 0)
    def _():
        halo_ref[...] = jnp.zeros_like(halo_ref)
        st_ref[...] = jnp.zeros_like(st_ref)

    for lo in range(0, y_ref.shape[1], OUT_PIECE):
        _out_b_piece(x_ref, ob_ref, wo_ref, y_ref, n_a, lo)

    lane = lax.broadcasted_iota(jnp.int32, (CHUNK, CHUNK), 1)
    subl = lax.broadcasted_iota(jnp.int32, (CHUNK, CHUNK), 0)
    eye = (lane == subl).astype(F32)
    lane8 = lax.broadcasted_iota(jnp.int32, (8, CHUNK), 1)
    lane_c = lax.broadcasted_iota(jnp.int32, (nqkv, CHUNK), 1)

    def rolls(v):
        return [pltpu.roll(v, j, axis=1) for j in range(1, CONV_K)]

    chains = []
    prev_rolls = rolls(halo_ref[...])
    for c in range(n_chunks):
        lanes = slice(c * CHUNK, (c + 1) * CHUNK)
        cur = g_ref[0:nqkv, lanes].astype(F32)
        cur_rolls = rolls(cur)
        if c == n_chunks - 1:
            halo_ref[...] = cur
        y = cw_ref[CONV_K - 1] * cur
        for j in range(1, CONV_K):
            shifted = jnp.where(lane_c >= j, cur_rolls[j - 1], prev_rolls[j - 1])
            y = y + cw_ref[CONV_K - 1 - j] * shifted
        prev_rolls = cur_rolls
        y = _silu(y)

        ba = ba_ref[:, lanes]
        beta8 = _sigmoid(ba)
        sp = ba + dt_ref[...]
        softplus = jnp.maximum(sp, 0.0) + jnp.log(1.0 + jnp.exp(-jnp.abs(sp)))
        gc8 = (-LOG2E) * jnp.exp(alog_ref[...]) * softplus
        sh = 1
        while sh < CHUNK:
            gc8 = gc8 + jnp.where(lane8 >= sh, pltpu.roll(gc8, sh, axis=1), 0.0)
            sh *= 2

        for h in range(GDN_HEADS):
            qt = y[h * GDN_D:(h + 1) * GDN_D, :]
            kt = y[(GDN_HEADS + h) * GDN_D:(GDN_HEADS + h + 1) * GDN_D, :]
            vt = y[(2 * GDN_HEADS + h) * GDN_D:(2 * GDN_HEADS + h + 1) * GDN_D, :]
            qt = qt * lax.rsqrt(jnp.sum(qt * qt, axis=0, keepdims=True) + EPS) * (GDN_D ** -0.5)
            kt = kt * lax.rsqrt(jnp.sum(kt * kt, axis=0, keepdims=True) + EPS)
            beta = beta8[h:h + 1, :]
            gc = gc8[GDN_HEADS + h:GDN_HEADS + h + 1, :]
            g_row = jnp.broadcast_to(gc, (CHUNK, CHUNK))
            g_col = g_row.T
            g_last = g_col[CHUNK - 1:CHUNK, :]
            e_gc = jnp.exp2(gc)
            chains.append(dict(
                c=c, h=h, qt=qt, kt=kt, beta=beta, g_last=g_last, g_col=g_col,
                decay_t=jnp.exp2(jnp.where(lane >= subl, g_row - g_col, NEG_BIG)),
                k_nat=kt.T,
                qg=qt * e_gc,
                rhs=jnp.concatenate([vt * beta, kt * (beta * e_gc)], axis=0).astype(BF16)))

    for ch in chains:
        gram = jnp.dot(ch["k_nat"].astype(BF16),
                       jnp.concatenate([ch["kt"], ch["qt"]], axis=1).astype(BF16),
                       preferred_element_type=F32)
        ch["a_t"] = gram[:, CHUNK:] * ch["decay_t"]
        ch["n"] = jnp.where(lane > subl, gram[:, :CHUNK] * ch["decay_t"] * (-ch["beta"]), 0.0)

    for ch in chains:
        nb = ch["n"].astype(BF16)
        ch["q"] = eye + ch["n"]
        ch["n"] = jnp.dot(nb, nb, preferred_element_type=F32)
    m = 2
    while 2 * m < CHUNK:
        for ch in chains:
            nb = ch["n"].astype(BF16)
            both = jnp.dot(jnp.concatenate([ch["q"].astype(BF16), nb], axis=0), nb,
                           preferred_element_type=F32)
            ch["q"] = ch["q"] + both[:CHUNK]
            ch["n"] = both[CHUNK:]
        m *= 2
    for ch in chains:
        ch["q"] = ch["q"] + jnp.dot(ch["q"].astype(BF16), ch["n"].astype(BF16),
                                    preferred_element_type=F32)

    for ch in chains:
        uw = jnp.dot(ch["rhs"], ch["q"].astype(BF16), preferred_element_type=F32)
        ch["u_t"] = uw[:GDN_D, :]
        ch["r1"] = jnp.concatenate([uw[GDN_D:, :], ch["qg"]], axis=1).astype(BF16)
        k_dec = ch["k_nat"] * jnp.exp2(ch["g_last"] - ch["g_col"])
        ch["r2"] = jnp.concatenate([k_dec, ch["a_t"]], axis=1).astype(BF16)

    def out_a_piece(c, oa_chunk):
        rows = slice(c * CHUNK, (c + 1) * CHUNK)
        y_ref[rows, :] += lax.dot_general(oa_chunk, wo_ref[0:n_a, :], TN_DIMS,
                                          preferred_element_type=F32)

    states = [st_ref[h] for h in range(GDN_HEADS)]
    ready = None
    for c in range(n_chunks):
        lanes = slice(c * CHUNK, (c + 1) * CHUNK)
        row = chains[c * GDN_HEADS:(c + 1) * GDN_HEADS]
        x1 = [jnp.dot(states[h].astype(BF16), row[h]["r1"], preferred_element_type=F32)
              for h in range(GDN_HEADS)]
        if ready is not None:
            out_a_piece(*ready)
        x2 = [jnp.dot((row[h]["u_t"] - x1[h][:, :CHUNK]).astype(BF16), row[h]["r2"],
                      preferred_element_type=F32) for h in range(GDN_HEADS)]
        oa = []
        for h in range(GDN_HEADS):
            states[h] = states[h] * jnp.exp2(row[h]["g_last"]) + x2[h][:, :GDN_D]
            o_t = x1[h][:, CHUNK:] + x2[h][:, GDN_D:]
            o_n = o_t * lax.rsqrt(jnp.mean(o_t * o_t, axis=0, keepdims=True) + EPS) * gw_ref[...]
            z_t = g_ref[nqkv + h * GDN_D:nqkv + (h + 1) * GDN_D, lanes].astype(F32)
            oa.append((o_n * _silu(z_t)).astype(BF16))
        ready = (c, jnp.concatenate(oa, axis=0))
    out_a_piece(*ready)
    for h in range(GDN_HEADS):
        st_ref[h] = states[h]


def _out_b_piece(x_ref, ob_ref, wo_ref, y_ref, n_a, lo):
    cols = slice(lo, lo + OUT_PIECE)
    y_ref[:, cols] = x_ref[:, cols] + jnp.dot(
        ob_ref[...], wo_ref[n_a:, cols], preferred_element_type=F32)


def _gdn_call(gdn_t, ba_t, cw, alog8, dt8, gw, x2, ob, w_out):
    s, d = x2.shape
    tb = GDN_TB
    nqkv = 3 * GDN_HEADS * GDN_D
    const = lambda shape: pl.BlockSpec(shape, lambda i: (0,) * len(shape))
    return pl.pallas_call(
        _gdn_kernel,
        grid=(s // tb,),
        in_specs=[
            pl.BlockSpec((gdn_t.shape[0], tb), lambda i: (0, i)),
            pl.BlockSpec((8, tb), lambda i: (0, i)),
            const(cw.shape),
            const(alog8.shape),
            const(dt8.shape),
            const(gw.shape),
            pl.BlockSpec((tb, d), lambda i: (i, 0)),
            pl.BlockSpec((tb, ob.shape[1]), lambda i: (i, 0)),
            const(w_out.shape),
        ],
        out_specs=pl.BlockSpec((tb, d), lambda i: (i, 0)),
        out_shape=jax.ShapeDtypeStruct((s, d), F32),
        scratch_shapes=[
            pltpu.VMEM((nqkv, CHUNK), F32),
            pltpu.VMEM((GDN_HEADS, GDN_D, GDN_D), F32),
        ],
        compiler_params=pltpu.CompilerParams(
            dimension_semantics=("arbitrary",), vmem_limit_bytes=VMEM_LIMIT),
        name="gdn",
    )(gdn_t, ba_t, cw, alog8, dt8, gw, x2, ob, w_out)


def _attn_kernel(lam_ref, q_ref, k_ref, v_ref, z_ref, ks_ref, sw_ref, o_ref,
                 acc_ref, l_ref, m_ref, p_ref, *, lam_init):
    for sub in range(ATT_SUB):
        _attn_tile(sub, lam_ref, q_ref, k_ref, v_ref, z_ref, ks_ref, sw_ref, o_ref,
                   acc_ref.at[sub], l_ref.at[sub], m_ref, p_ref, lam_init)


def _attn_tile(sub, lam_ref, q_ref, k_ref, v_ref, z_ref, ks_ref, sw_ref, o_ref,
               acc_ref, l_ref, m_ref, p_ref, lam_init):
    tq, tk = ATT_TQ, ATT_TK
    dv = 2 * DIFF_D
    cols = slice(sub * tq, (sub + 1) * tq)
    i = pl.program_id(1) * ATT_SUB + sub
    qt = q_ref[:, cols]
    row = lax.broadcasted_iota(jnp.int32, qt.shape, 0)
    zero = jnp.zeros_like(qt)
    q_cat = jnp.concatenate(
        [jnp.where(row < DIFF_D, qt, zero), jnp.where(row >= DIFF_D, qt, zero)], axis=1)

    def scores(j):
        start = pl.multiple_of(j * tk, tk)
        return jnp.dot(k_ref[pl.ds(start, tk), :], q_cat, preferred_element_type=F32)

    def flush(j_pending, alpha):
        start = pl.multiple_of(j_pending * tk, tk)
        vt = v_ref[:, pl.ds(start, tk)]
        for c in range(2):
            upd = acc_ref[c] + jnp.dot(vt, p_ref[:, c * tq:(c + 1) * tq],
                                       preferred_element_type=F32)
            acc_ref[c] = upd if alpha is None else upd * alpha[:, c * tq:(c + 1) * tq]

    def exp_store(s_j, m):
        p = jnp.exp2(s_j - m)
        p_ref[...] = p.astype(BF16)
        return jnp.sum(p.reshape(tk // 8, 8, 2 * tq), axis=0)

    def pending(j):
        return jnp.where(j == 0, i, j - 1)

    qf = qt.astype(F32)
    qq = qf * qf
    qsq = jnp.concatenate([jnp.sum(qq[:DIFF_D], axis=0, keepdims=True),
                           jnp.sum(qq[DIFF_D:], axis=0, keepdims=True)], axis=1)
    kmax = jnp.max(ks_ref[0], axis=1, keepdims=True)
    kmax = jnp.concatenate([jnp.broadcast_to(kmax[0:1], (1, tq)),
                            jnp.broadcast_to(kmax[1:2], (1, tq))], axis=1)
    bound = jnp.sqrt(qsq * kmax) * NORM_SLACK

    s_t = scores(i)
    keep = (lax.broadcasted_iota(jnp.int32, (tk, tq), 0)
            <= lax.broadcasted_iota(jnp.int32, (tk, tq), 1))
    s_t = jnp.concatenate([jnp.where(keep, s_t[:, :tq], NEG_BIG),
                           jnp.where(keep, s_t[:, tq:], NEG_BIG)], axis=1)
    m0 = jnp.max(s_t, axis=0, keepdims=True)
    frozen_ok = jnp.max(bound - m0) <= FROZEN_MAX_GAP
    m_ref[...] = m0
    l_ref[...] = exp_store(s_t, m0)
    acc_ref[...] = jnp.zeros_like(acc_ref)

    def frozen_step(j, j_pending):
        s_j = scores(j)
        flush(j_pending, None)
        l_ref[...] += exp_store(s_j, m_ref[...])

    def frozen_steps(j0, n):
        for u in range(n):
            frozen_step(j0 + u, pending(j0) if u == 0 else j0 + u - 1)

    def frozen_group(t, carry):
        frozen_steps(ATT_UNROLL * t, ATT_UNROLL)
        return carry

    def online_body(j, carry):
        s_j = scores(j)
        m_old = m_ref[...]
        m_new = jnp.maximum(m_old, jnp.max(s_j, axis=0, keepdims=True))
        alpha = jnp.exp2(m_old - m_new)
        flush(pending(j), alpha)
        m_ref[...] = m_new
        l_ref[...] = l_ref[...] * alpha + exp_store(s_j, m_new)
        return carry

    @pl.when(frozen_ok)
    def _():
        groups = lax.shift_right_logical(i, ATT_UNROLL.bit_length() - 1)
        lax.fori_loop(0, groups, frozen_group, 0)
        j0 = groups * ATT_UNROLL
        n = ATT_UNROLL // 2
        while n >= 1:
            @pl.when((i & n) != 0)
            def _(j0=j0, n=n):
                frozen_steps(j0, n)
            j0 = j0 + (i & n)
            n //= 2

    @pl.when(jnp.logical_not(frozen_ok))
    def _():
        lax.fori_loop(0, i, online_body, 0)

    flush(pending(i), None)

    lam = lam_ref[0, 0]
    l = jnp.sum(l_ref[...], axis=0, keepdims=True)
    o_t = acc_ref[0] / l[:, :tq] - lam * (acc_ref[1] / l[:, tq:])
    o_n = o_t * lax.rsqrt(jnp.mean(o_t * o_t, axis=0, keepdims=True) + SUBLN_EPS) * sw_ref[...]
    o_n = o_n * (1.0 - lam_init)
    o_ref[cols, :] = (o_n * _silu(z_ref[:, cols].astype(F32))).T.astype(o_ref.dtype)


def _attn_call(lam, q_t, k_nat, v_t, z_t, ksq, sw, lam_init):
    dv = 2 * DIFF_D
    s = k_nat.shape[0]
    h = k_nat.shape[1] // dv
    tq = ATT_TQ
    tg = ATT_SUB * tq
    assert ATT_TQ == ATT_TK
    return pl.pallas_call(
        functools.partial(_attn_kernel, lam_init=lam_init),
        grid=(h, s // tg),
        in_specs=[
            pl.BlockSpec(memory_space=pltpu.SMEM),
            pl.BlockSpec((dv, tg), lambda hh, i: (hh, i)),
            pl.BlockSpec((s, dv), lambda hh, i: (0, hh)),
            pl.BlockSpec((dv, s), lambda hh, i: (hh, 0)),
            pl.BlockSpec((dv, tg), lambda hh, i: (hh, i)),
            pl.BlockSpec((1, 2, s), lambda hh, i: (hh, 0, 0)),
            pl.BlockSpec((dv, tq), lambda hh, i: (0, 0)),
        ],
        out_specs=pl.BlockSpec((tg, dv), lambda hh, i: (i, hh)),
        out_shape=jax.ShapeDtypeStruct((s, h * dv), BF16),
        scratch_shapes=[
            pltpu.VMEM((ATT_SUB, 2, dv, tq), F32),
            pltpu.VMEM((ATT_SUB, 8, 2 * tq), F32),
            pltpu.VMEM((1, 2 * tq), F32),
            pltpu.VMEM((ATT_TK, 2 * tq), BF16),
        ],
        compiler_params=pltpu.CompilerParams(
            dimension_semantics=("arbitrary", "arbitrary"), vmem_limit_bytes=VMEM_LIMIT),
        name="attn",
    )(lam, q_t, k_nat, v_t, z_t, ksq, sw)


def _layer(l, x2, cos_t, sin_t, w_norm, w_in, conv_w, a_log, dt_bias, gdn_norm_w, q_norm_w,
           k_norm_w, lambda_q1, lambda_k1, lambda_q2, lambda_k2, subln_w, w_out):
    s, d = x2.shape
    nqk = GDN_HEADS * GDN_D
    ba_lo = 4 * nqk
    ba_hi = ba_lo + 2 * GDN_HEADS
    wt = jnp.concatenate([w_in[:, :ba_lo], w_in[:, ba_hi:]], axis=1).T.astype(BF16)
    wba = w_in[:, ba_lo:ba_hi].T.astype(BF16)
    qw = jnp.broadcast_to(q_norm_w[:, None], (DIFF_D, PROJ_TM)).astype(F32)
    kw = jnp.broadcast_to(k_norm_w[:, None], (DIFF_D, PROJ_TM)).astype(F32)

    gdn_t, ba_t, qb_t, kb, ksq, vb_t, zb_t = _proj_call(
        x2, w_norm[None, :].astype(F32), wt, wba, cos_t, sin_t, qw, kw)

    cw = jnp.broadcast_to(conv_w.astype(F32)[:, :, None], (CONV_K, 3 * nqk, LANES))
    zeros4 = jnp.zeros((GDN_HEADS, LANES), F32)
    alog8 = jnp.concatenate([zeros4, jnp.broadcast_to(a_log.astype(F32)[:, None], (GDN_HEADS, LANES))], 0)
    dt8 = jnp.concatenate([zeros4, jnp.broadcast_to(dt_bias.astype(F32)[:, None], (GDN_HEADS, LANES))], 0)
    gw = jnp.broadcast_to(gdn_norm_w.astype(F32)[:, None], (GDN_D, LANES))

    lam_init = 0.8 - 0.6 * math.exp(-0.3 * l)
    lam = (jnp.exp(jnp.sum(lambda_q1.astype(F32) * lambda_k1.astype(F32)))
           - jnp.exp(jnp.sum(lambda_q2.astype(F32) * lambda_k2.astype(F32))) + lam_init)
    dv = 2 * DIFF_D
    sw = jnp.broadcast_to(subln_w.astype(F32)[:, None], (dv, ATT_TQ))
    ob = _attn_call(lam.reshape(1, 1).astype(F32), qb_t, kb, vb_t, zb_t, ksq, sw, lam_init)

    return _gdn_call(gdn_t, ba_t, cw, alog8, dt8, gw, x2, ob, w_out.astype(BF16))


def kernel(x, w_norm, w_in, conv_w, a_log, dt_bias, gdn_norm_w, q_norm_w, k_norm_w,
           lambda_q1, lambda_k1, lambda_q2, lambda_k2, subln_w, w_out):
    b, s, d = x.shape
    assert b == 1
    inv_freq = ROPE_THETA ** (-jnp.arange(0, DIFF_D, 2, dtype=jnp.float32) / DIFF_D)
    ang = jnp.arange(s, dtype=jnp.float32)[:, None] * inv_freq[None, :]
    cos_t, sin_t = jnp.cos(ang).T, jnp.sin(ang).T
    x2 = x[0]
    for l in range(w_norm.shape[0]):
        x2 = _layer(l, x2, cos_t, sin_t, w_norm[l], w_in[l], conv_w[l], a_log[l], dt_bias[l],
                    gdn_norm_w[l], q_norm_w[l], k_norm_w[l], lambda_q1[l], lambda_k1[l],
                    lambda_q2[l], lambda_k2[l], subln_w[l], w_out[l])
    return x2[None]
```

```python
import functools
import math

import jax
import jax.numpy as jnp
from jax import lax
from jax.experimental import pallas as pl
from jax.experimental.pallas import tpu as pltpu

F32 = jnp.float32
BF16 = jnp.bfloat16

GDN_HEADS = 4
GDN_D = 128
CONV_K = 4
DIFF_HEADS = 4
DIFF_D = 64
ROPE_THETA = 10000.0
EPS = 1e-6
SUBLN_EPS = 1e-5

LANES = 128
CHUNK = LANES
NEG_BIG = -1e30
LOG2E = 1.4426950408889634
FROZEN_MAX_GAP = 40.0
NORM_SLACK = 1.01

PROJ_TM = 512
GDN_TB = 512
ATT_TQ = 512
ATT_TK = 512
ATT_UNROLL = 8
ATT_SUB = 2
OUT_PIECE = 256
VMEM_LIMIT = 56 * 1024 * 1024

NT_DIMS = (((1,), (1,)), ((), ()))
TN_DIMS = (((0,), (0,)), ((), ()))


def _sigmoid(v):
    return 1.0 / (1.0 + jnp.exp(-v))


def _silu(v):
    return v * _sigmoid(v)


def _proj_kernel(x_ref, wn_ref, wt_ref, wba_ref, cos_ref, sin_ref, qw_ref, kw_ref, cw_ref,
                 gdn_ref, ba_ref, qb_ref, kb_ref, ks_ref, vb_ref, zb_ref, halo_ref):
    @pl.when(pl.program_id(0) == 0)
    def _():
        halo_ref[...] = jnp.zeros_like(halo_ref)

    x = x_ref[...]
    ms = jnp.mean(x * x, axis=-1, keepdims=True)
    hn = (x * lax.rsqrt(ms + EPS) * wn_ref[...]).astype(BF16)

    def proj_t(lo, n):
        return lax.dot_general(wt_ref[lo:lo + n, :], hn, NT_DIMS, preferred_element_type=F32)

    blk = 512
    q_raw = proj_t(4 * blk, blk)
    k_raw = proj_t(5 * blk, blk)

    cos = cos_ref[...]
    sin = sin_ref[...]
    half = DIFF_D // 2

    def norm_rope(t, w, scale):
        outs = []
        for g in range(t.shape[0] // DIFF_D):
            tg = t[g * DIFF_D:(g + 1) * DIFF_D, :]
            tg = tg * lax.rsqrt(jnp.mean(tg * tg, axis=0, keepdims=True) + EPS) * w
            t1, t2 = tg[:half, :], tg[half:, :]
            outs.append((t1 * cos - t2 * sin) * scale)
            outs.append((t2 * cos + t1 * sin) * scale)
        return jnp.concatenate(outs, axis=0)

    qb_ref[...] = norm_rope(q_raw, qw_ref[...], LOG2E / math.sqrt(DIFF_D)).astype(BF16)
    kt = norm_rope(k_raw, kw_ref[...], 1.0)
    for g in range(2 * DIFF_HEADS):
        kg = kt[g * DIFF_D:(g + 1) * DIFF_D, :]
        ks_ref[g // 2, g % 2:g % 2 + 1, :] = jnp.sum(kg * kg, axis=0, keepdims=True)
    kb_ref[...] = kt.T.astype(BF16)

    tm = x.shape[0]
    lane_t = lax.broadcasted_iota(jnp.int32, (blk, LANES), 1)
    for b in range(3):
        rows = slice(b * blk, (b + 1) * blk)
        raw = proj_t(b * blk, blk)
        tail = halo_ref[rows, :]
        halo_ref[rows, :] = raw[:, tm - LANES:]
        y = jnp.concatenate([cw_ref[CONV_K - 1, rows, :]] * (tm // LANES), axis=1) * raw
        for j in range(1, CONV_K):
            rolled = pltpu.roll(raw, j, axis=1)
            first = jnp.where(lane_t >= j, rolled[:, :LANES], pltpu.roll(tail, j, axis=1))
            shifted = jnp.concatenate([first, rolled[:, LANES:]], axis=1)
            y = y + jnp.concatenate([cw_ref[CONV_K - 1 - j, rows, :]] * (tm // LANES), axis=1) * shifted
        gdn_ref[rows, :] = _silu(y).astype(BF16)
    gdn_ref[3 * blk:4 * blk, :] = _silu(proj_t(3 * blk, blk)).astype(BF16)
    zb_ref[...] = _silu(proj_t(7 * blk, blk)).astype(BF16)
    ba_ref[...] = lax.dot_general(wba_ref[...], hn, NT_DIMS, preferred_element_type=F32)
    vb_ref[...] = proj_t(6 * blk, blk).astype(BF16)


def _proj_call(x2, w_norm, wt, wba, cos_t, sin_t, qw, kw, cw):
    s, d = x2.shape
    tm = PROJ_TM
    const = lambda shape: pl.BlockSpec(shape, lambda i: (0,) * len(shape))
    cols = lambda rows: pl.BlockSpec((rows, tm), lambda i: (0, i))
    return pl.pallas_call(
        _proj_kernel,
        grid=(s // tm,),
        in_specs=[
            pl.BlockSpec((tm, d), lambda i: (i, 0)),
            const((1, d)),
            const(wt.shape),
            const(wba.shape),
            cols(DIFF_D // 2),
            cols(DIFF_D // 2),
            const((DIFF_D, tm)),
            const((DIFF_D, tm)),
            const(cw.shape),
        ],
        out_specs=[
            cols(2048),
            cols(8),
            cols(512),
            pl.BlockSpec((tm, 512), lambda i: (i, 0)),
            pl.BlockSpec((DIFF_HEADS, 2, tm), lambda i: (0, 0, i)),
            cols(512),
            cols(512),
        ],
        out_shape=[
            jax.ShapeDtypeStruct((2048, s), BF16),
            jax.ShapeDtypeStruct((8, s), F32),
            jax.ShapeDtypeStruct((512, s), BF16),
            jax.ShapeDtypeStruct((s, 512), BF16),
            jax.ShapeDtypeStruct((DIFF_HEADS, 2, s), F32),
            jax.ShapeDtypeStruct((512, s), BF16),
            jax.ShapeDtypeStruct((512, s), BF16),
        ],
        scratch_shapes=[
            pltpu.VMEM((cw.shape[1], LANES), F32),
        ],
        compiler_params=pltpu.CompilerParams(
            dimension_semantics=("arbitrary",), vmem_limit_bytes=VMEM_LIMIT),
        name="proj",
    )(x2, w_norm, wt, wba, cos_t, sin_t, qw, kw, cw)


def _gdn_kernel(g_ref, ba_ref, alog_ref, dt_ref, gw_ref, x_ref, ob_ref, wo_ref,
                y_ref, st_ref):
    nqkv = 3 * GDN_HEADS * GDN_D
    n_chunks = g_ref.shape[1] // CHUNK
    n_a = GDN_HEADS * GDN_D

    @pl.when(pl.program_id(0) == 0)
    def _():
        st_ref[...] = jnp.zeros_like(st_ref)

    for lo in range(0, y_ref.shape[1], OUT_PIECE):
        _out_b_piece(x_ref, ob_ref, wo_ref, y_ref, n_a, lo)

    lane = lax.broadcasted_iota(jnp.int32, (CHUNK, CHUNK), 1)
    subl = lax.broadcasted_iota(jnp.int32, (CHUNK, CHUNK), 0)
    eye = (lane == subl).astype(F32)
    lane8 = lax.broadcasted_iota(jnp.int32, (8, CHUNK), 1)

    chains = []
    for c in range(n_chunks):
        lanes = slice(c * CHUNK, (c + 1) * CHUNK)
        y = g_ref[0:nqkv, lanes].astype(F32)

        ba = ba_ref[:, lanes]
        beta8 = _sigmoid(ba)
        sp = ba + dt_ref[...]
        softplus = jnp.maximum(sp, 0.0) + jnp.log(1.0 + jnp.exp(-jnp.abs(sp)))
        gc8 = (-LOG2E) * jnp.exp(alog_ref[...]) * softplus
        sh = 1
        while sh < CHUNK:
            gc8 = gc8 + jnp.where(lane8 >= sh, pltpu.roll(gc8, sh, axis=1), 0.0)
            sh *= 2

        for h in range(GDN_HEADS):
            qt = y[h * GDN_D:(h + 1) * GDN_D, :]
            kt = y[(GDN_HEADS + h) * GDN_D:(GDN_HEADS + h + 1) * GDN_D, :]
            vt = y[(2 * GDN_HEADS + h) * GDN_D:(2 * GDN_HEADS + h + 1) * GDN_D, :]
            qt = qt * lax.rsqrt(jnp.sum(qt * qt, axis=0, keepdims=True) + EPS) * (GDN_D ** -0.5)
            kt = kt * lax.rsqrt(jnp.sum(kt * kt, axis=0, keepdims=True) + EPS)
            beta = beta8[h:h + 1, :]
            gc = gc8[GDN_HEADS + h:GDN_HEADS + h + 1, :]
            g_row = jnp.broadcast_to(gc, (CHUNK, CHUNK))
            g_col = g_row.T
            g_last = g_col[CHUNK - 1:CHUNK, :]
            e_gc = jnp.exp2(gc)
            chains.append(dict(
                c=c, h=h, qt=qt, kt=kt, beta=beta, g_last=g_last, g_col=g_col,
                decay_t=jnp.exp2(jnp.where(lane >= subl, g_row - g_col, NEG_BIG)),
                k_nat=kt.T,
                qg=qt * e_gc,
                rhs=jnp.concatenate([vt * beta, kt * (beta * e_gc)], axis=0).astype(BF16)))

    for ch in chains:
        gram = jnp.dot(ch["k_nat"].astype(BF16),
                       jnp.concatenate([ch["kt"], ch["qt"]], axis=1).astype(BF16),
                       preferred_element_type=F32)
        ch["a_t"] = gram[:, CHUNK:] * ch["decay_t"]
        ch["n"] = jnp.where(lane > subl, gram[:, :CHUNK] * ch["decay_t"] * (-ch["beta"]), 0.0)

    for ch in chains:
        nb = ch["n"].astype(BF16)
        ch["q"] = eye + ch["n"]
        ch["n"] = jnp.dot(nb, nb, preferred_element_type=F32)
    m = 2
    while 2 * m < CHUNK:
        for ch in chains:
            nb = ch["n"].astype(BF16)
            both = jnp.dot(jnp.concatenate([ch["q"].astype(BF16), nb], axis=0), nb,
                           preferred_element_type=F32)
            ch["q"] = ch["q"] + both[:CHUNK]
            ch["n"] = both[CHUNK:]
        m *= 2
    for ch in chains:
        ch["q"] = ch["q"] + jnp.dot(ch["q"].astype(BF16), ch["n"].astype(BF16),
                                    preferred_element_type=F32)

    for ch in chains:
        uw = jnp.dot(ch["rhs"], ch["q"].astype(BF16), preferred_element_type=F32)
        ch["u_t"] = uw[:GDN_D, :]
        ch["r1"] = jnp.concatenate([uw[GDN_D:, :], ch["qg"]], axis=1).astype(BF16)
        k_dec = ch["k_nat"] * jnp.exp2(ch["g_last"] - ch["g_col"])
        ch["r2"] = jnp.concatenate([k_dec, ch["a_t"]], axis=1).astype(BF16)

    def out_a_piece(c, oa_chunk):
        rows = slice(c * CHUNK, (c + 1) * CHUNK)
        y_ref[rows, :] += lax.dot_general(oa_chunk, wo_ref[0:n_a, :], TN_DIMS,
                                          preferred_element_type=F32)

    states = [st_ref[h] for h in range(GDN_HEADS)]
    ready = None
    for c in range(n_chunks):
        lanes = slice(c * CHUNK, (c + 1) * CHUNK)
        row = chains[c * GDN_HEADS:(c + 1) * GDN_HEADS]
        x1 = [jnp.dot(states[h].astype(BF16), row[h]["r1"], preferred_element_type=F32)
              for h in range(GDN_HEADS)]
        if ready is not None:
            out_a_piece(*ready)
        x2 = [jnp.dot((row[h]["u_t"] - x1[h][:, :CHUNK]).astype(BF16), row[h]["r2"],
                      preferred_element_type=F32) for h in range(GDN_HEADS)]
        oa = []
        for h in range(GDN_HEADS):
            states[h] = states[h] * jnp.exp2(row[h]["g_last"]) + x2[h][:, :GDN_D]
            o_t = x1[h][:, CHUNK:] + x2[h][:, GDN_D:]
            o_n = o_t * lax.rsqrt(jnp.mean(o_t * o_t, axis=0, keepdims=True) + EPS) * gw_ref[...]
            gate = g_ref[nqkv + h * GDN_D:nqkv + (h + 1) * GDN_D, lanes].astype(F32)
            oa.append((o_n * gate).astype(BF16))
        ready = (c, jnp.concatenate(oa, axis=0))
    out_a_piece(*ready)
    for h in range(GDN_HEADS):
        st_ref[h] = states[h]


def _out_b_piece(x_ref, ob_ref, wo_ref, y_ref, n_a, lo):
    cols = slice(lo, lo + OUT_PIECE)
    y_ref[:, cols] = x_ref[:, cols] + jnp.dot(
        ob_ref[...], wo_ref[n_a:, cols], preferred_element_type=F32)


def _gdn_call(gdn_t, ba_t, alog8, dt8, gw, x2, ob, w_out):
    s, d = x2.shape
    tb = GDN_TB
    const = lambda shape: pl.BlockSpec(shape, lambda i: (0,) * len(shape))
    return pl.pallas_call(
        _gdn_kernel,
        grid=(s // tb,),
        in_specs=[
            pl.BlockSpec((gdn_t.shape[0], tb), lambda i: (0, i)),
            pl.BlockSpec((8, tb), lambda i: (0, i)),
            const(alog8.shape),
            const(dt8.shape),
            const(gw.shape),
            pl.BlockSpec((tb, d), lambda i: (i, 0)),
            pl.BlockSpec((tb, ob.shape[1]), lambda i: (i, 0)),
            const(w_out.shape),
        ],
        out_specs=pl.BlockSpec((tb, d), lambda i: (i, 0)),
        out_shape=jax.ShapeDtypeStruct((s, d), F32),
        scratch_shapes=[
            pltpu.VMEM((GDN_HEADS, GDN_D, GDN_D), F32),
        ],
        compiler_params=pltpu.CompilerParams(
            dimension_semantics=("arbitrary",), vmem_limit_bytes=VMEM_LIMIT),
        name="gdn",
    )(gdn_t, ba_t, alog8, dt8, gw, x2, ob, w_out)


def _attn_kernel(lam_ref, q_ref, k_ref, v_ref, z_ref, ks_ref, sw_ref, o_ref,
                 acc_ref, l_ref, m_ref, p_ref, *, lam_init):
    for sub in range(ATT_SUB):
        _attn_tile(sub, lam_ref, q_ref, k_ref, v_ref, z_ref, ks_ref, sw_ref, o_ref,
                   acc_ref.at[sub], l_ref.at[sub], m_ref, p_ref, lam_init)


def _attn_tile(sub, lam_ref, q_ref, k_ref, v_ref, z_ref, ks_ref, sw_ref, o_ref,
               acc_ref, l_ref, m_ref, p_ref, lam_init):
    tq, tk = ATT_TQ, ATT_TK
    dv = 2 * DIFF_D
    cols = slice(sub * tq, (sub + 1) * tq)
    i = pl.program_id(1) * ATT_SUB + sub
    qt = q_ref[:, cols]
    row = lax.broadcasted_iota(jnp.int32, qt.shape, 0)
    zero = jnp.zeros_like(qt)
    q_cat = jnp.concatenate(
        [jnp.where(row < DIFF_D, qt, zero), jnp.where(row >= DIFF_D, qt, zero)], axis=1)

    def scores(j):
        start = pl.multiple_of(j * tk, tk)
        return jnp.dot(k_ref[pl.ds(start, tk), :], q_cat, preferred_element_type=F32)

    def flush(j_pending, alpha):
        start = pl.multiple_of(j_pending * tk, tk)
        vt = v_ref[:, pl.ds(start, tk)]
        for c in range(2):
            upd = acc_ref[c] + jnp.dot(vt, p_ref[:, c * tq:(c + 1) * tq],
                                       preferred_element_type=F32)
            acc_ref[c] = upd if alpha is None else upd * alpha[:, c * tq:(c + 1) * tq]

    def exp_store(s_j, m):
        p = jnp.exp2(s_j - m)
        p_ref[...] = p.astype(BF16)
        return jnp.sum(p.reshape(tk // 8, 8, 2 * tq), axis=0)

    def pending(j):
        return jnp.where(j == 0, i, j - 1)

    qf = qt.astype(F32)
    qq = qf * qf
    qsq = jnp.concatenate([jnp.sum(qq[:DIFF_D], axis=0, keepdims=True),
                           jnp.sum(qq[DIFF_D:], axis=0, keepdims=True)], axis=1)
    kmax = jnp.max(ks_ref[0], axis=1, keepdims=True)
    kmax = jnp.concatenate([jnp.broadcast_to(kmax[0:1], (1, tq)),
                            jnp.broadcast_to(kmax[1:2], (1, tq))], axis=1)
    bound = jnp.sqrt(qsq * kmax) * NORM_SLACK

    s_t = scores(i)
    keep = (lax.broadcasted_iota(jnp.int32, (tk, tq), 0)
            <= lax.broadcasted_iota(jnp.int32, (tk, tq), 1))
    s_t = jnp.concatenate([jnp.where(keep, s_t[:, :tq], NEG_BIG),
                           jnp.where(keep, s_t[:, tq:], NEG_BIG)], axis=1)
    m0 = jnp.max(s_t, axis=0, keepdims=True)
    frozen_ok = jnp.max(bound - m0) <= FROZEN_MAX_GAP
    m_ref[...] = m0
    l_ref[...] = exp_store(s_t, m0)
    acc_ref[...] = jnp.zeros_like(acc_ref)

    def frozen_step(j, j_pending):
        s_j = scores(j)
        flush(j_pending, None)
        l_ref[...] += exp_store(s_j, m_ref[...])

    def frozen_steps(j0, n):
        for u in range(n):
            frozen_step(j0 + u, pending(j0) if u == 0 else j0 + u - 1)

    def frozen_group(t, carry):
        frozen_steps(ATT_UNROLL * t, ATT_UNROLL)
        return carry

    def online_body(j, carry):
        s_j = scores(j)
        m_old = m_ref[...]
        m_new = jnp.maximum(m_old, jnp.max(s_j, axis=0, keepdims=True))
        alpha = jnp.exp2(m_old - m_new)
        flush(pending(j), alpha)
        m_ref[...] = m_new
        l_ref[...] = l_ref[...] * alpha + exp_store(s_j, m_new)
        return carry

    @pl.when(frozen_ok)
    def _():
        groups = lax.shift_right_logical(i, ATT_UNROLL.bit_length() - 1)
        lax.fori_loop(0, groups, frozen_group, 0)
        j0 = groups * ATT_UNROLL
        n = ATT_UNROLL // 2
        while n >= 1:
            @pl.when((i & n) != 0)
            def _(j0=j0, n=n):
                frozen_steps(j0, n)
            j0 = j0 + (i & n)
            n //= 2

    @pl.when(jnp.logical_not(frozen_ok))
    def _():
        lax.fori_loop(0, i, online_body, 0)

    flush(pending(i), None)

    lam = lam_ref[0, 0]
    l = jnp.sum(l_ref[...], axis=0, keepdims=True)
    o_t = acc_ref[0] / l[:, :tq] - lam * (acc_ref[1] / l[:, tq:])
    o_n = o_t * lax.rsqrt(jnp.mean(o_t * o_t, axis=0, keepdims=True) + SUBLN_EPS) * sw_ref[...]
    o_n = o_n * (1.0 - lam_init)
    o_ref[cols, :] = (o_n * z_ref[:, cols].astype(F32)).T.astype(o_ref.dtype)


def _attn_call(lam, q_t, k_nat, v_t, z_t, ksq, sw, lam_init):
    dv = 2 * DIFF_D
    s = k_nat.shape[0]
    h = k_nat.shape[1] // dv
    tq = ATT_TQ
    tg = ATT_SUB * tq
    assert ATT_TQ == ATT_TK
    return pl.pallas_call(
        functools.partial(_attn_kernel, lam_init=lam_init),
        grid=(h, s // tg),
        in_specs=[
            pl.BlockSpec(memory_space=pltpu.SMEM),
            pl.BlockSpec((dv, tg), lambda hh, i: (hh, i)),
            pl.BlockSpec((s, dv), lambda hh, i: (0, hh)),
            pl.BlockSpec((dv, s), lambda hh, i: (hh, 0)),
            pl.BlockSpec((dv, tg), lambda hh, i: (hh, i)),
            pl.BlockSpec((1, 2, s), lambda hh, i: (hh, 0, 0)),
            pl.BlockSpec((dv, tq), lambda hh, i: (0, 0)),
        ],
        out_specs=pl.BlockSpec((tg, dv), lambda hh, i: (i, hh)),
        out_shape=jax.ShapeDtypeStruct((s, h * dv), BF16),
        scratch_shapes=[
            pltpu.VMEM((ATT_SUB, 2, dv, tq), F32),
            pltpu.VMEM((ATT_SUB, 8, 2 * tq), F32),
            pltpu.VMEM((1, 2 * tq), F32),
            pltpu.VMEM((ATT_TK, 2 * tq), BF16),
        ],
        compiler_params=pltpu.CompilerParams(
            dimension_semantics=("arbitrary", "arbitrary"), vmem_limit_bytes=VMEM_LIMIT),
        name="attn",
    )(lam, q_t, k_nat, v_t, z_t, ksq, sw)


def _layer(l, x2, cos_t, sin_t, w_norm, w_in, conv_w, a_log, dt_bias, gdn_norm_w, q_norm_w,
           k_norm_w, lambda_q1, lambda_k1, lambda_q2, lambda_k2, subln_w, w_out):
    s, d = x2.shape
    nqk = GDN_HEADS * GDN_D
    ba_lo = 4 * nqk
    ba_hi = ba_lo + 2 * GDN_HEADS
    wt = jnp.concatenate([w_in[:, :ba_lo], w_in[:, ba_hi:]], axis=1).T.astype(BF16)
    wba = w_in[:, ba_lo:ba_hi].T.astype(BF16)
    qw = jnp.broadcast_to(q_norm_w[:, None], (DIFF_D, PROJ_TM)).astype(F32)
    kw = jnp.broadcast_to(k_norm_w[:, None], (DIFF_D, PROJ_TM)).astype(F32)

    cw = jnp.broadcast_to(conv_w.astype(F32)[:, :, None], (CONV_K, 3 * nqk, LANES))
    gdn_t, ba_t, qb_t, kb, ksq, vb_t, zb_t = _proj_call(
        x2, w_norm[None, :].astype(F32), wt, wba, cos_t, sin_t, qw, kw, cw)

    zeros4 = jnp.zeros((GDN_HEADS, LANES), F32)
    alog8 = jnp.concatenate([zeros4, jnp.broadcast_to(a_log.astype(F32)[:, None], (GDN_HEADS, LANES))], 0)
    dt8 = jnp.concatenate([zeros4, jnp.broadcast_to(dt_bias.astype(F32)[:, None], (GDN_HEADS, LANES))], 0)
    gw = jnp.broadcast_to(gdn_norm_w.astype(F32)[:, None], (GDN_D, LANES))

    lam_init = 0.8 - 0.6 * math.exp(-0.3 * l)
    lam = (jnp.exp(jnp.sum(lambda_q1.astype(F32) * lambda_k1.astype(F32)))
           - jnp.exp(jnp.sum(lambda_q2.astype(F32) * lambda_k2.astype(F32))) + lam_init)
    dv = 2 * DIFF_D
    sw = jnp.broadcast_to(subln_w.astype(F32)[:, None], (dv, ATT_TQ))
    ob = _attn_call(lam.reshape(1, 1).astype(F32), qb_t, kb, vb_t, zb_t, ksq, sw, lam_init)

    return _gdn_call(gdn_t, ba_t, alog8, dt8, gw, x2, ob, w_out.astype(BF16))


def kernel(x, w_norm, w_in, conv_w, a_log, dt_bias, gdn_norm_w, q_norm_w, k_norm_w,
           lambda_q1, lambda_k1, lambda_q2, lambda_k2, subln_w, w_out):
    b, s, d = x.shape
    assert b == 1
    inv_freq = ROPE_THETA ** (-jnp.arange(0, DIFF_D, 2, dtype=jnp.float32) / DIFF_D)
    ang = jnp.arange(s, dtype=jnp.float32)[:, None] * inv_freq[None, :]
    cos_t, sin_t = jnp.cos(ang).T, jnp.sin(ang).T
    x2 = x[0]
    for l in range(w_norm.shape[0]):
        x2 = _layer(l, x2, cos_t, sin_t, w_norm[l], w_in[l], conv_w[l], a_log[l], dt_bias[l],
                    gdn_norm_w[l], q_norm_w[l], k_norm_w[l], lambda_q1[l], lambda_k1[l],
                    lambda_q2[l], lambda_k2[l], subln_w[l], w_out[l])
    return x2[None]
```

```python
import functools
import math

import jax
import jax.numpy as jnp
from jax import lax
from jax.experimental import pallas as pl
from jax.experimental.pallas import tpu as pltpu

F32 = jnp.float32
BF16 = jnp.bfloat16

GDN_HEADS = 4
GDN_D = 128
CONV_K = 4
DIFF_HEADS = 4
DIFF_D = 64
ROPE_THETA = 10000.0
EPS = 1e-6
SUBLN_EPS = 1e-5

LANES = 128
CHUNK = LANES
NEG_BIG = -1e30
LOG2E = 1.4426950408889634
FROZEN_MAX_GAP = 40.0
NORM_SLACK = 1.01

PROJ_TM = 512
GDN_TB = 512
ATT_TQ = 512
ATT_TK = 512
ATT_UNROLL = 8
ATT_SUB = 4
OUT_PIECE = 256
VMEM_LIMIT = 56 * 1024 * 1024

NT_DIMS = (((1,), (1,)), ((), ()))
TN_DIMS = (((0,), (0,)), ((), ()))


def _sigmoid(v):
    return 1.0 / (1.0 + jnp.exp(-v))


def _silu(v):
    return v * _sigmoid(v)


def _proj_kernel(x_ref, wn_ref, wt_ref, wba_ref, cos_ref, sin_ref, qw_ref, kw_ref, cw_ref,
                 gdn_ref, ba_ref, qb_ref, kb_ref, ks_ref, vb_ref, zb_ref, halo_ref):
    @pl.when(pl.program_id(0) == 0)
    def _():
        halo_ref[...] = jnp.zeros_like(halo_ref)

    x = x_ref[...]
    ms = jnp.mean(x * x, axis=-1, keepdims=True)
    hn = (x * lax.rsqrt(ms + EPS) * wn_ref[...]).astype(BF16)

    def proj_t(lo, n):
        return lax.dot_general(wt_ref[lo:lo + n, :], hn, NT_DIMS, preferred_element_type=F32)

    blk = 512
    q_raw = proj_t(4 * blk, blk)
    k_raw = proj_t(5 * blk, blk)

    cos = cos_ref[...]
    sin = sin_ref[...]
    half = DIFF_D // 2

    def norm_rope(t, w, scale):
        outs = []
        for g in range(t.shape[0] // DIFF_D):
            tg = t[g * DIFF_D:(g + 1) * DIFF_D, :]
            tg = tg * lax.rsqrt(jnp.mean(tg * tg, axis=0, keepdims=True) + EPS) * w
            t1, t2 = tg[:half, :], tg[half:, :]
            outs.append((t1 * cos - t2 * sin) * scale)
            outs.append((t2 * cos + t1 * sin) * scale)
        return jnp.concatenate(outs, axis=0)

    qb_ref[...] = norm_rope(q_raw, qw_ref[...], LOG2E / math.sqrt(DIFF_D)).astype(BF16)
    kt = norm_rope(k_raw, kw_ref[...], 1.0)
    for g in range(2 * DIFF_HEADS):
        kg = kt[g * DIFF_D:(g + 1) * DIFF_D, :]
        ks_ref[g // 2, g % 2:g % 2 + 1, :] = jnp.sum(kg * kg, axis=0, keepdims=True)
    kb_ref[...] = kt.T.astype(BF16)

    tm = x.shape[0]
    lane_t = lax.broadcasted_iota(jnp.int32, (blk, LANES), 1)
    for b in range(3):
        rows = slice(b * blk, (b + 1) * blk)
        raw = proj_t(b * blk, blk)
        tail = halo_ref[rows, :]
        halo_ref[rows, :] = raw[:, tm - LANES:]
        y = jnp.concatenate([cw_ref[CONV_K - 1, rows, :]] * (tm // LANES), axis=1) * raw
        for j in range(1, CONV_K):
            rolled = pltpu.roll(raw, j, axis=1)
            first = jnp.where(lane_t >= j, rolled[:, :LANES], pltpu.roll(tail, j, axis=1))
            shifted = jnp.concatenate([first, rolled[:, LANES:]], axis=1)
            y = y + jnp.concatenate([cw_ref[CONV_K - 1 - j, rows, :]] * (tm // LANES), axis=1) * shifted
        gdn_ref[rows, :] = _silu(y).astype(BF16)
    gdn_ref[3 * blk:4 * blk, :] = _silu(proj_t(3 * blk, blk)).astype(BF16)
    zb_ref[...] = _silu(proj_t(7 * blk, blk)).astype(BF16)
    ba_ref[...] = lax.dot_general(wba_ref[...], hn, NT_DIMS, preferred_element_type=F32)
    vb_ref[...] = proj_t(6 * blk, blk).astype(BF16)


def _proj_call(x2, w_norm, wt, wba, cos_t, sin_t, qw, kw, cw):
    s, d = x2.shape
    tm = PROJ_TM
    const = lambda shape: pl.BlockSpec(shape, lambda i: (0,) * len(shape))
    cols = lambda rows: pl.BlockSpec((rows, tm), lambda i: (0, i))
    return pl.pallas_call(
        _proj_kernel,
        grid=(s // tm,),
        in_specs=[
            pl.BlockSpec((tm, d), lambda i: (i, 0)),
            const((1, d)),
            const(wt.shape),
            const(wba.shape),
            cols(DIFF_D // 2),
            cols(DIFF_D // 2),
            const((DIFF_D, tm)),
            const((DIFF_D, tm)),
            const(cw.shape),
        ],
        out_specs=[
            cols(2048),
            cols(8),
            cols(512),
            pl.BlockSpec((tm, 512), lambda i: (i, 0)),
            pl.BlockSpec((DIFF_HEADS, 2, tm), lambda i: (0, 0, i)),
            cols(512),
            cols(512),
        ],
        out_shape=[
            jax.ShapeDtypeStruct((2048, s), BF16),
            jax.ShapeDtypeStruct((8, s), F32),
            jax.ShapeDtypeStruct((512, s), BF16),
            jax.ShapeDtypeStruct((s, 512), BF16),
            jax.ShapeDtypeStruct((DIFF_HEADS, 2, s), F32),
            jax.ShapeDtypeStruct((512, s), BF16),
            jax.ShapeDtypeStruct((512, s), BF16),
        ],
        scratch_shapes=[
            pltpu.VMEM((cw.shape[1], LANES), F32),
        ],
        compiler_params=pltpu.CompilerParams(
            dimension_semantics=("arbitrary",), vmem_limit_bytes=VMEM_LIMIT),
        name="proj",
    )(x2, w_norm, wt, wba, cos_t, sin_t, qw, kw, cw)


def _gdn_kernel(g_ref, ba_ref, alog_ref, dt_ref, gw_ref, x_ref, ob_ref, wo_ref,
                y_ref, st_ref):
    nqkv = 3 * GDN_HEADS * GDN_D
    n_chunks = g_ref.shape[1] // CHUNK
    n_a = GDN_HEADS * GDN_D

    @pl.when(pl.program_id(0) == 0)
    def _():
        st_ref[...] = jnp.zeros_like(st_ref)

    for lo in range(0, y_ref.shape[1], OUT_PIECE):
        _out_b_piece(x_ref, ob_ref, wo_ref, y_ref, n_a, lo)

    lane = lax.broadcasted_iota(jnp.int32, (CHUNK, CHUNK), 1)
    subl = lax.broadcasted_iota(jnp.int32, (CHUNK, CHUNK), 0)
    eye = (lane == subl).astype(F32)
    lane8 = lax.broadcasted_iota(jnp.int32, (8, CHUNK), 1)

    chains = []
    for c in range(n_chunks):
        lanes = slice(c * CHUNK, (c + 1) * CHUNK)
        y = g_ref[0:nqkv, lanes].astype(F32)

        ba = ba_ref[:, lanes]
        beta8 = _sigmoid(ba)
        sp = ba + dt_ref[...]
        softplus = jnp.maximum(sp, 0.0) + jnp.log(1.0 + jnp.exp(-jnp.abs(sp)))
        gc8 = (-LOG2E) * jnp.exp(alog_ref[...]) * softplus
        sh = 1
        while sh < CHUNK:
            gc8 = gc8 + jnp.where(lane8 >= sh, pltpu.roll(gc8, sh, axis=1), 0.0)
            sh *= 2

        for h in range(GDN_HEADS):
            qt = y[h * GDN_D:(h + 1) * GDN_D, :]
            kt = y[(GDN_HEADS + h) * GDN_D:(GDN_HEADS + h + 1) * GDN_D, :]
            vt = y[(2 * GDN_HEADS + h) * GDN_D:(2 * GDN_HEADS + h + 1) * GDN_D, :]
            qt = qt * lax.rsqrt(jnp.sum(qt * qt, axis=0, keepdims=True) + EPS) * (GDN_D ** -0.5)
            kt = kt * lax.rsqrt(jnp.sum(kt * kt, axis=0, keepdims=True) + EPS)
            beta = beta8[h:h + 1, :]
            gc = gc8[GDN_HEADS + h:GDN_HEADS + h + 1, :]
            g_row = jnp.broadcast_to(gc, (CHUNK, CHUNK))
            g_col = g_row.T
            g_last = g_col[CHUNK - 1:CHUNK, :]
            e_gc = jnp.exp2(gc)
            chains.append(dict(
                c=c, h=h, qt=qt, kt=kt, beta=beta, g_last=g_last, g_col=g_col,
                decay_t=jnp.exp2(jnp.where(lane >= subl, g_row - g_col, NEG_BIG)),
                k_nat=kt.T,
                qg=qt * e_gc,
                rhs=jnp.concatenate([vt * beta, kt * (beta * e_gc)], axis=0).astype(BF16)))

    for ch in chains:
        gram = jnp.dot(ch["k_nat"].astype(BF16),
                       jnp.concatenate([ch["kt"], ch["qt"]], axis=1).astype(BF16),
                       preferred_element_type=F32)
        ch["a_t"] = gram[:, CHUNK:] * ch["decay_t"]
        ch["n"] = jnp.where(lane > subl, gram[:, :CHUNK] * ch["decay_t"] * (-ch["beta"]), 0.0)

    for ch in chains:
        nb = ch["n"].astype(BF16)
        ch["q"] = eye + ch["n"]
        ch["n"] = jnp.dot(nb, nb, preferred_element_type=F32)
    m = 2
    while 2 * m < CHUNK:
        for ch in chains:
            nb = ch["n"].astype(BF16)
            both = jnp.dot(jnp.concatenate([ch["q"].astype(BF16), nb], axis=0), nb,
                           preferred_element_type=F32)
            ch["q"] = ch["q"] + both[:CHUNK]
            ch["n"] = both[CHUNK:]
        m *= 2
    for ch in chains:
        ch["q"] = ch["q"] + jnp.dot(ch["q"].astype(BF16), ch["n"].astype(BF16),
                                    preferred_element_type=F32)

    for ch in chains:
        uw = jnp.dot(ch["rhs"], ch["q"].astype(BF16), preferred_element_type=F32)
        ch["u_t"] = uw[:GDN_D, :]
        ch["r1"] = jnp.concatenate([uw[GDN_D:, :], ch["qg"]], axis=1).astype(BF16)
        k_dec = ch["k_nat"] * jnp.exp2(ch["g_last"] - ch["g_col"])
        ch["r2"] = jnp.concatenate([k_dec, ch["a_t"]], axis=1).astype(BF16)

    def out_a_piece(c, oa_chunk):
        rows = slice(c * CHUNK, (c + 1) * CHUNK)
        y_ref[rows, :] += lax.dot_general(oa_chunk, wo_ref[0:n_a, :], TN_DIMS,
                                          preferred_element_type=F32)

    states = [st_ref[h] for h in range(GDN_HEADS)]
    ready = None
    for c in range(n_chunks):
        lanes = slice(c * CHUNK, (c + 1) * CHUNK)
        row = chains[c * GDN_HEADS:(c + 1) * GDN_HEADS]
        x1 = [jnp.dot(states[h].astype(BF16), row[h]["r1"], preferred_element_type=F32)
              for h in range(GDN_HEADS)]
        if ready is not None:
            out_a_piece(*ready)
        x2 = [jnp.dot((row[h]["u_t"] - x1[h][:, :CHUNK]).astype(BF16), row[h]["r2"],
                      preferred_element_type=F32) for h in range(GDN_HEADS)]
        oa = []
        for h in range(GDN_HEADS):
            states[h] = states[h] * jnp.exp2(row[h]["g_last"]) + x2[h][:, :GDN_D]
            o_t = x1[h][:, CHUNK:] + x2[h][:, GDN_D:]
            o_n = o_t * lax.rsqrt(jnp.mean(o_t * o_t, axis=0, keepdims=True) + EPS) * gw_ref[...]
            gate = g_ref[nqkv + h * GDN_D:nqkv + (h + 1) * GDN_D, lanes].astype(F32)
            oa.append((o_n * gate).astype(BF16))
        ready = (c, jnp.concatenate(oa, axis=0))
    out_a_piece(*ready)
    for h in range(GDN_HEADS):
        st_ref[h] = states[h]


def _out_b_piece(x_ref, ob_ref, wo_ref, y_ref, n_a, lo):
    cols = slice(lo, lo + OUT_PIECE)
    y_ref[:, cols] = x_ref[:, cols] + jnp.dot(
        ob_ref[...], wo_ref[n_a:, cols], preferred_element_type=F32)


def _gdn_call(gdn_t, ba_t, alog8, dt8, gw, x2, ob, w_out):
    s, d = x2.shape
    tb = GDN_TB
    const = lambda shape: pl.BlockSpec(shape, lambda i: (0,) * len(shape))
    return pl.pallas_call(
        _gdn_kernel,
        grid=(s // tb,),
        in_specs=[
            pl.BlockSpec((gdn_t.shape[0], tb), lambda i: (0, i)),
            pl.BlockSpec((8, tb), lambda i: (0, i)),
            const(alog8.shape),
            const(dt8.shape),
            const(gw.shape),
            pl.BlockSpec((tb, d), lambda i: (i, 0)),
            pl.BlockSpec((tb, ob.shape[1]), lambda i: (i, 0)),
            const(w_out.shape),
        ],
        out_specs=pl.BlockSpec((tb, d), lambda i: (i, 0)),
        out_shape=jax.ShapeDtypeStruct((s, d), F32),
        scratch_shapes=[
            pltpu.VMEM((GDN_HEADS, GDN_D, GDN_D), F32),
        ],
        compiler_params=pltpu.CompilerParams(
            dimension_semantics=("arbitrary",), vmem_limit_bytes=VMEM_LIMIT),
        name="gdn",
    )(gdn_t, ba_t, alog8, dt8, gw, x2, ob, w_out)


def _attn_kernel(lam_ref, q_ref, k_ref, v_ref, z_ref, ks_ref, sw_ref, o_ref,
                 acc_ref, l_ref, m_ref, p_ref, *, lam_init):
    for sub in range(ATT_SUB):
        _attn_tile(sub, lam_ref, q_ref, k_ref, v_ref, z_ref, ks_ref, sw_ref, o_ref,
                   acc_ref.at[sub], l_ref.at[sub], m_ref, p_ref, lam_init)


def _attn_tile(sub, lam_ref, q_ref, k_ref, v_ref, z_ref, ks_ref, sw_ref, o_ref,
               acc_ref, l_ref, m_ref, p_ref, lam_init):
    tq, tk = ATT_TQ, ATT_TK
    dv = 2 * DIFF_D
    cols = slice(sub * tq, (sub + 1) * tq)
    i = pl.program_id(1) * ATT_SUB + sub
    qt = q_ref[:, cols]
    row = lax.broadcasted_iota(jnp.int32, qt.shape, 0)
    zero = jnp.zeros_like(qt)
    q_cat = jnp.concatenate(
        [jnp.where(row < DIFF_D, qt, zero), jnp.where(row >= DIFF_D, qt, zero)], axis=1)

    def scores(j):
        start = pl.multiple_of(j * tk, tk)
        return jnp.dot(k_ref[pl.ds(start, tk), :], q_cat, preferred_element_type=F32)

    def flush(j_pending, alpha):
        start = pl.multiple_of(j_pending * tk, tk)
        vt = v_ref[:, pl.ds(start, tk)]
        for c in range(2):
            upd = acc_ref[c] + jnp.dot(vt, p_ref[:, c * tq:(c + 1) * tq],
                                       preferred_element_type=F32)
            acc_ref[c] = upd if alpha is None else upd * alpha[:, c * tq:(c + 1) * tq]

    def exp_store(s_j, m):
        p = jnp.exp2(s_j - m)
        p_ref[...] = p.astype(BF16)
        return jnp.sum(p.reshape(tk // 8, 8, 2 * tq), axis=0)

    def pending(j):
        return jnp.where(j == 0, i, j - 1)

    qf = qt.astype(F32)
    qq = qf * qf
    qsq = jnp.concatenate([jnp.sum(qq[:DIFF_D], axis=0, keepdims=True),
                           jnp.sum(qq[DIFF_D:], axis=0, keepdims=True)], axis=1)
    kmax = jnp.max(ks_ref[0], axis=1, keepdims=True)
    kmax = jnp.concatenate([jnp.broadcast_to(kmax[0:1], (1, tq)),
                            jnp.broadcast_to(kmax[1:2], (1, tq))], axis=1)
    bound = jnp.sqrt(qsq * kmax) * NORM_SLACK

    s_t = scores(i)
    keep = (lax.broadcasted_iota(jnp.int32, (tk, tq), 0)
            <= lax.broadcasted_iota(jnp.int32, (tk, tq), 1))
    s_t = jnp.concatenate([jnp.where(keep, s_t[:, :tq], NEG_BIG),
                           jnp.where(keep, s_t[:, tq:], NEG_BIG)], axis=1)
    m0 = jnp.max(s_t, axis=0, keepdims=True)
    frozen_ok = jnp.max(bound - m0) <= FROZEN_MAX_GAP
    m_ref[...] = m0
    l_ref[...] = exp_store(s_t, m0)
    acc_ref[...] = jnp.zeros_like(acc_ref)

    def frozen_step(j, j_pending):
        s_j = scores(j)
        flush(j_pending, None)
        l_ref[...] += exp_store(s_j, m_ref[...])

    def frozen_steps(j0, n):
        for u in range(n):
            frozen_step(j0 + u, pending(j0) if u == 0 else j0 + u - 1)

    def frozen_group(t, carry):
        frozen_steps(ATT_UNROLL * t, ATT_UNROLL)
        return carry

    def online_body(j, carry):
        s_j = scores(j)
        m_old = m_ref[...]
        m_new = jnp.maximum(m_old, jnp.max(s_j, axis=0, keepdims=True))
        alpha = jnp.exp2(m_old - m_new)
        flush(pending(j), alpha)
        m_ref[...] = m_new
        l_ref[...] = l_ref[...] * alpha + exp_store(s_j, m_new)
        return carry

    @pl.when(frozen_ok)
    def _():
        groups = lax.shift_right_logical(i, ATT_UNROLL.bit_length() - 1)
        lax.fori_loop(0, groups, frozen_group, 0)
        j0 = groups * ATT_UNROLL
        n = ATT_UNROLL // 2
        while n >= 1:
            @pl.when((i & n) != 0)
            def _(j0=j0, n=n):
                frozen_steps(j0, n)
            j0 = j0 + (i & n)
            n //= 2

    @pl.when(jnp.logical_not(frozen_ok))
    def _():
        lax.fori_loop(0, i, online_body, 0)

    flush(pending(i), None)

    lam = lam_ref[0, 0]
    l = jnp.sum(l_ref[...], axis=0, keepdims=True)
    o_t = acc_ref[0] / l[:, :tq] - lam * (acc_ref[1] / l[:, tq:])
    o_n = o_t * lax.rsqrt(jnp.mean(o_t * o_t, axis=0, keepdims=True) + SUBLN_EPS) * sw_ref[...]
    o_n = o_n * (1.0 - lam_init)
    o_ref[cols, :] = (o_n * z_ref[:, cols].astype(F32)).T.astype(o_ref.dtype)


def _attn_call(lam, q_t, k_nat, v_t, z_t, ksq, sw, lam_init):
    dv = 2 * DIFF_D
    s = k_nat.shape[0]
    h = k_nat.shape[1] // dv
    tq = ATT_TQ
    tg = ATT_SUB * tq
    assert ATT_TQ == ATT_TK
    return pl.pallas_call(
        functools.partial(_attn_kernel, lam_init=lam_init),
        grid=(h, s // tg),
        in_specs=[
            pl.BlockSpec(memory_space=pltpu.SMEM),
            pl.BlockSpec((dv, tg), lambda hh, i: (hh, i)),
            pl.BlockSpec((s, dv), lambda hh, i: (0, hh)),
            pl.BlockSpec((dv, s), lambda hh, i: (hh, 0)),
            pl.BlockSpec((dv, tg), lambda hh, i: (hh, i)),
            pl.BlockSpec((1, 2, s), lambda hh, i: (hh, 0, 0)),
            pl.BlockSpec((dv, tq), lambda hh, i: (0, 0)),
        ],
        out_specs=pl.BlockSpec((tg, dv), lambda hh, i: (i, hh)),
        out_shape=jax.ShapeDtypeStruct((s, h * dv), BF16),
        scratch_shapes=[
            pltpu.VMEM((ATT_SUB, 2, dv, tq), F32),
            pltpu.VMEM((ATT_SUB, 8, 2 * tq), F32),
            pltpu.VMEM((1, 2 * tq), F32),
            pltpu.VMEM((ATT_TK, 2 * tq), BF16),
        ],
        compiler_params=pltpu.CompilerParams(
            dimension_semantics=("arbitrary", "arbitrary"), vmem_limit_bytes=VMEM_LIMIT),
        name="attn",
    )(lam, q_t, k_nat, v_t, z_t, ksq, sw)


def _layer(l, x2, cos_t, sin_t, w_norm, w_in, conv_w, a_log, dt_bias, gdn_norm_w, q_norm_w,
           k_norm_w, lambda_q1, lambda_k1, lambda_q2, lambda_k2, subln_w, w_out):
    s, d = x2.shape
    nqk = GDN_HEADS * GDN_D
    ba_lo = 4 * nqk
    ba_hi = ba_lo + 2 * GDN_HEADS
    wt = jnp.concatenate([w_in[:, :ba_lo], w_in[:, ba_hi:]], axis=1).T.astype(BF16)
    wba = w_in[:, ba_lo:ba_hi].T.astype(BF16)
    qw = jnp.broadcast_to(q_norm_w[:, None], (DIFF_D, PROJ_TM)).astype(F32)
    kw = jnp.broadcast_to(k_norm_w[:, None], (DIFF_D, PROJ_TM)).astype(F32)

    cw = jnp.broadcast_to(conv_w.astype(F32)[:, :, None], (CONV_K, 3 * nqk, LANES))
    gdn_t, ba_t, qb_t, kb, ksq, vb_t, zb_t = _proj_call(
        x2, w_norm[None, :].astype(F32), wt, wba, cos_t, sin_t, qw, kw, cw)

    zeros4 = jnp.zeros((GDN_HEADS, LANES), F32)
    alog8 = jnp.concatenate([zeros4, jnp.broadcast_to(a_log.astype(F32)[:, None], (GDN_HEADS, LANES))], 0)
    dt8 = jnp.concatenate([zeros4, jnp.broadcast_to(dt_bias.astype(F32)[:, None], (GDN_HEADS, LANES))], 0)
    gw = jnp.broadcast_to(gdn_norm_w.astype(F32)[:, None], (GDN_D, LANES))

    lam_init = 0.8 - 0.6 * math.exp(-0.3 * l)
    lam = (jnp.exp(jnp.sum(lambda_q1.astype(F32) * lambda_k1.astype(F32)))
           - jnp.exp(jnp.sum(lambda_q2.astype(F32) * lambda_k2.astype(F32))) + lam_init)
    dv = 2 * DIFF_D
    sw = jnp.broadcast_to(subln_w.astype(F32)[:, None], (dv, ATT_TQ))
    ob = _attn_call(lam.reshape(1, 1).astype(F32), qb_t, kb, vb_t, zb_t, ksq, sw, lam_init)

    return _gdn_call(gdn_t, ba_t, alog8, dt8, gw, x2, ob, w_out.astype(BF16))


def kernel(x, w_norm, w_in, conv_w, a_log, dt_bias, gdn_norm_w, q_norm_w, k_norm_w,
           lambda_q1, lambda_k1, lambda_q2, lambda_k2, subln_w, w_out):
    b, s, d = x.shape
    assert b == 1
    inv_freq = ROPE_THETA ** (-jnp.arange(0, DIFF_D, 2, dtype=jnp.float32) / DIFF_D)
    ang = jnp.arange(s, dtype=jnp.float32)[:, None] * inv_freq[None, :]
    cos_t, sin_t = jnp.cos(ang).T, jnp.sin(ang).T
    x2 = x[0]
    for l in range(w_norm.shape[0]):
        x2 = _layer(l, x2, cos_t, sin_t, w_norm[l], w_in[l], conv_w[l], a_log[l], dt_bias[l],
                    gdn_norm_w[l], q_norm_w[l], k_norm_w[l], lambda_q1[l], lambda_k1[l],
                    lambda_q2[l], lambda_k2[l], subln_w[l], w_out[l])
    return x2[None]
```

```python
import functools
import math

import jax
import jax.numpy as jnp
from jax import lax
from jax.experimental import pallas as pl
from jax.experimental.pallas import tpu as pltpu

F32 = jnp.float32
BF16 = jnp.bfloat16

GDN_HEADS = 4
GDN_D = 128
CONV_K = 4
DIFF_HEADS = 4
DIFF_D = 64
ROPE_THETA = 10000.0
EPS = 1e-6
SUBLN_EPS = 1e-5

LANES = 128
CHUNK = LANES
NEG_BIG = -1e30
LOG2E = 1.4426950408889634
FROZEN_MAX_BOUND = 40.0
NORM_SLACK = 1.01

PROJ_TM = 512
GDN_TB = 512
ATT_TQ = 512
ATT_TK = 512
ATT_UNROLL = 8
ATT_SUB = 2
OUT_PIECE = 256
VMEM_LIMIT = 56 * 1024 * 1024

NT_DIMS = (((1,), (1,)), ((), ()))
TN_DIMS = (((0,), (0,)), ((), ()))


def _sigmoid(v):
    return 1.0 / (1.0 + jnp.exp(-v))


def _silu(v):
    return v * _sigmoid(v)


def _proj_kernel(x_ref, wn_ref, wt_ref, wba_ref, cos_ref, sin_ref, qw_ref, kw_ref, cw_ref,
                 gdn_ref, ba_ref, qb_ref, kb_ref, ks_ref, vb_ref, zb_ref, halo_ref):
    @pl.when(pl.program_id(0) == 0)
    def _():
        halo_ref[...] = jnp.zeros_like(halo_ref)

    x = x_ref[...]
    ms = jnp.mean(x * x, axis=-1, keepdims=True)
    hn = (x * lax.rsqrt(ms + EPS) * wn_ref[...]).astype(BF16)

    def proj_t(lo, n):
        return lax.dot_general(wt_ref[lo:lo + n, :], hn, NT_DIMS, preferred_element_type=F32)

    blk = 512
    q_raw = proj_t(4 * blk, blk)
    k_raw = proj_t(5 * blk, blk)

    cos = cos_ref[...]
    sin = sin_ref[...]
    half = DIFF_D // 2

    def norm_rope(t, w, scale):
        outs = []
        for g in range(t.shape[0] // DIFF_D):
            tg = t[g * DIFF_D:(g + 1) * DIFF_D, :]
            tg = tg * lax.rsqrt(jnp.mean(tg * tg, axis=0, keepdims=True) + EPS) * w
            t1, t2 = tg[:half, :], tg[half:, :]
            outs.append((t1 * cos - t2 * sin) * scale)
            outs.append((t2 * cos + t1 * sin) * scale)
        return jnp.concatenate(outs, axis=0)

    qb_ref[...] = norm_rope(q_raw, qw_ref[...], LOG2E / math.sqrt(DIFF_D)).astype(BF16)
    kt = norm_rope(k_raw, kw_ref[...], 1.0)
    for g in range(2 * DIFF_HEADS):
        kg = kt[g * DIFF_D:(g + 1) * DIFF_D, :]
        ks_ref[g // 2, g % 2:g % 2 + 1, :] = jnp.sum(kg * kg, axis=0, keepdims=True)
    kb_ref[...] = kt.T.astype(BF16)

    tm = x.shape[0]
    lane_t = lax.broadcasted_iota(jnp.int32, (blk, LANES), 1)
    for b in range(3):
        rows = slice(b * blk, (b + 1) * blk)
        raw = proj_t(b * blk, blk)
        tail = halo_ref[rows, :]
        halo_ref[rows, :] = raw[:, tm - LANES:]
        y = jnp.concatenate([cw_ref[CONV_K - 1, rows, :]] * (tm // LANES), axis=1) * raw
        for j in range(1, CONV_K):
            rolled = pltpu.roll(raw, j, axis=1)
            first = jnp.where(lane_t >= j, rolled[:, :LANES], pltpu.roll(tail, j, axis=1))
            shifted = jnp.concatenate([first, rolled[:, LANES:]], axis=1)
            y = y + jnp.concatenate([cw_ref[CONV_K - 1 - j, rows, :]] * (tm // LANES), axis=1) * shifted
        gdn_ref[rows, :] = _silu(y).astype(BF16)
    gdn_ref[3 * blk:4 * blk, :] = _silu(proj_t(3 * blk, blk)).astype(BF16)
    zb_ref[...] = _silu(proj_t(7 * blk, blk)).astype(BF16)
    ba_ref[...] = lax.dot_general(wba_ref[...], hn, NT_DIMS, preferred_element_type=F32)
    vb_ref[...] = proj_t(6 * blk, blk).astype(BF16)


def _proj_call(x2, w_norm, wt, wba, cos_t, sin_t, qw, kw, cw):
    s, d = x2.shape
    tm = PROJ_TM
    const = lambda shape: pl.BlockSpec(shape, lambda i: (0,) * len(shape))
    cols = lambda rows: pl.BlockSpec((rows, tm), lambda i: (0, i))
    return pl.pallas_call(
        _proj_kernel,
        grid=(s // tm,),
        in_specs=[
            pl.BlockSpec((tm, d), lambda i: (i, 0)),
            const((1, d)),
            const(wt.shape),
            const(wba.shape),
            cols(DIFF_D // 2),
            cols(DIFF_D // 2),
            const((DIFF_D, tm)),
            const((DIFF_D, tm)),
            const(cw.shape),
        ],
        out_specs=[
            cols(2048),
            cols(8),
            cols(512),
            pl.BlockSpec((tm, 512), lambda i: (i, 0)),
            pl.BlockSpec((DIFF_HEADS, 2, tm), lambda i: (0, 0, i)),
            cols(512),
            cols(512),
        ],
        out_shape=[
            jax.ShapeDtypeStruct((2048, s), BF16),
            jax.ShapeDtypeStruct((8, s), F32),
            jax.ShapeDtypeStruct((512, s), BF16),
            jax.ShapeDtypeStruct((s, 512), BF16),
            jax.ShapeDtypeStruct((DIFF_HEADS, 2, s), F32),
            jax.ShapeDtypeStruct((512, s), BF16),
            jax.ShapeDtypeStruct((512, s), BF16),
        ],
        scratch_shapes=[
            pltpu.VMEM((cw.shape[1], LANES), F32),
        ],
        compiler_params=pltpu.CompilerParams(
            dimension_semantics=("arbitrary",), vmem_limit_bytes=VMEM_LIMIT),
        name="proj",
    )(x2, w_norm, wt, wba, cos_t, sin_t, qw, kw, cw)


def _gdn_kernel(g_ref, ba_ref, alog_ref, dt_ref, gw_ref, x_ref, ob_ref, wo_ref,
                y_ref, st_ref):
    nqkv = 3 * GDN_HEADS * GDN_D
    n_chunks = g_ref.shape[1] // CHUNK
    n_a = GDN_HEADS * GDN_D

    @pl.when(pl.program_id(0) == 0)
    def _():
        st_ref[...] = jnp.zeros_like(st_ref)

    for lo in range(0, y_ref.shape[1], OUT_PIECE):
        _out_b_piece(x_ref, ob_ref, wo_ref, y_ref, n_a, lo)

    lane = lax.broadcasted_iota(jnp.int32, (CHUNK, CHUNK), 1)
    subl = lax.broadcasted_iota(jnp.int32, (CHUNK, CHUNK), 0)
    eye = (lane == subl).astype(F32)
    lane8 = lax.broadcasted_iota(jnp.int32, (8, CHUNK), 1)

    chains = []
    for c in range(n_chunks):
        lanes = slice(c * CHUNK, (c + 1) * CHUNK)
        y = g_ref[0:nqkv, lanes].astype(F32)

        ba = ba_ref[:, lanes]
        beta8 = _sigmoid(ba)
        sp = ba + dt_ref[...]
        softplus = jnp.maximum(sp, 0.0) + jnp.log(1.0 + jnp.exp(-jnp.abs(sp)))
        gc8 = (-LOG2E) * jnp.exp(alog_ref[...]) * softplus
        sh = 1
        while sh < CHUNK:
            gc8 = gc8 + jnp.where(lane8 >= sh, pltpu.roll(gc8, sh, axis=1), 0.0)
            sh *= 2

        for h in range(GDN_HEADS):
            qt = y[h * GDN_D:(h + 1) * GDN_D, :]
            kt = y[(GDN_HEADS + h) * GDN_D:(GDN_HEADS + h + 1) * GDN_D, :]
            vt = y[(2 * GDN_HEADS + h) * GDN_D:(2 * GDN_HEADS + h + 1) * GDN_D, :]
            qt = qt * lax.rsqrt(jnp.sum(qt * qt, axis=0, keepdims=True) + EPS) * (GDN_D ** -0.5)
            kt = kt * lax.rsqrt(jnp.sum(kt * kt, axis=0, keepdims=True) + EPS)
            beta = beta8[h:h + 1, :]
            gc = gc8[GDN_HEADS + h:GDN_HEADS + h + 1, :]
            g_row = jnp.broadcast_to(gc, (CHUNK, CHUNK))
            g_col = g_row.T
            g_last = g_col[CHUNK - 1:CHUNK, :]
            e_gc = jnp.exp2(gc)
            chains.append(dict(
                c=c, h=h, qt=qt, kt=kt, beta=beta, g_last=g_last, g_col=g_col,
                decay_t=jnp.exp2(jnp.where(lane >= subl, g_row - g_col, NEG_BIG)),
                k_nat=kt.T,
                qg=qt * e_gc,
                rhs=jnp.concatenate([vt * beta, kt * (beta * e_gc)], axis=0).astype(BF16)))

    for ch in chains:
        gram = jnp.dot(ch["k_nat"].astype(BF16),
                       jnp.concatenate([ch["kt"], ch["qt"]], axis=1).astype(BF16),
                       preferred_element_type=F32)
        ch["a_t"] = gram[:, CHUNK:] * ch["decay_t"]
        ch["n"] = jnp.where(lane > subl, gram[:, :CHUNK] * ch["decay_t"] * (-ch["beta"]), 0.0)

    for ch in chains:
        nb = ch["n"].astype(BF16)
        ch["q"] = eye + ch["n"]
        ch["n"] = jnp.dot(nb, nb, preferred_element_type=F32)
    m = 2
    while 2 * m < CHUNK:
        for ch in chains:
            nb = ch["n"].astype(BF16)
            both = jnp.dot(jnp.concatenate([ch["q"].astype(BF16), nb], axis=0), nb,
                           preferred_element_type=F32)
            ch["q"] = ch["q"] + both[:CHUNK]
            ch["n"] = both[CHUNK:]
        m *= 2
    for ch in chains:
        ch["q"] = ch["q"] + jnp.dot(ch["q"].astype(BF16), ch["n"].astype(BF16),
                                    preferred_element_type=F32)

    for ch in chains:
        uw = jnp.dot(ch["rhs"], ch["q"].astype(BF16), preferred_element_type=F32)
        ch["u_t"] = uw[:GDN_D, :]
        ch["r1"] = jnp.concatenate([uw[GDN_D:, :], ch["qg"]], axis=1).astype(BF16)
        k_dec = ch["k_nat"] * jnp.exp2(ch["g_last"] - ch["g_col"])
        ch["r2"] = jnp.concatenate([k_dec, ch["a_t"]], axis=1).astype(BF16)

    def out_a_piece(c, oa_chunk):
        rows = slice(c * CHUNK, (c + 1) * CHUNK)
        y_ref[rows, :] += lax.dot_general(oa_chunk, wo_ref[0:n_a, :], TN_DIMS,
                                          preferred_element_type=F32)

    states = [st_ref[h] for h in range(GDN_HEADS)]
    ready = None
    for c in range(n_chunks):
        lanes = slice(c * CHUNK, (c + 1) * CHUNK)
        row = chains[c * GDN_HEADS:(c + 1) * GDN_HEADS]
        x1 = [jnp.dot(states[h].astype(BF16), row[h]["r1"], preferred_element_type=F32)
              for h in range(GDN_HEADS)]
        if ready is not None:
            out_a_piece(*ready)
        x2 = [jnp.dot((row[h]["u_t"] - x1[h][:, :CHUNK]).astype(BF16), row[h]["r2"],
                      preferred_element_type=F32) for h in range(GDN_HEADS)]
        oa = []
        for h in range(GDN_HEADS):
            states[h] = states[h] * jnp.exp2(row[h]["g_last"]) + x2[h][:, :GDN_D]
            o_t = x1[h][:, CHUNK:] + x2[h][:, GDN_D:]
            o_n = o_t * lax.rsqrt(jnp.mean(o_t * o_t, axis=0, keepdims=True) + EPS) * gw_ref[...]
            gate = g_ref[nqkv + h * GDN_D:nqkv + (h + 1) * GDN_D, lanes].astype(F32)
            oa.append((o_n * gate).astype(BF16))
        ready = (c, jnp.concatenate(oa, axis=0))
    out_a_piece(*ready)
    for h in range(GDN_HEADS):
        st_ref[h] = states[h]


def _out_b_piece(x_ref, ob_ref, wo_ref, y_ref, n_a, lo):
    cols = slice(lo, lo + OUT_PIECE)
    y_ref[:, cols] = x_ref[:, cols] + jnp.dot(
        ob_ref[...], wo_ref[n_a:, cols], preferred_element_type=F32)


def _gdn_call(gdn_t, ba_t, alog8, dt8, gw, x2, ob, w_out):
    s, d = x2.shape
    tb = GDN_TB
    const = lambda shape: pl.BlockSpec(shape, lambda i: (0,) * len(shape))
    return pl.pallas_call(
        _gdn_kernel,
        grid=(s // tb,),
        in_specs=[
            pl.BlockSpec((gdn_t.shape[0], tb), lambda i: (0, i)),
            pl.BlockSpec((8, tb), lambda i: (0, i)),
            const(alog8.shape),
            const(dt8.shape),
            const(gw.shape),
            pl.BlockSpec((tb, d), lambda i: (i, 0)),
            pl.BlockSpec((tb, ob.shape[1]), lambda i: (i, 0)),
            const(w_out.shape),
        ],
        out_specs=pl.BlockSpec((tb, d), lambda i: (i, 0)),
        out_shape=jax.ShapeDtypeStruct((s, d), F32),
        scratch_shapes=[
            pltpu.VMEM((GDN_HEADS, GDN_D, GDN_D), F32),
        ],
        compiler_params=pltpu.CompilerParams(
            dimension_semantics=("arbitrary",), vmem_limit_bytes=VMEM_LIMIT),
        name="gdn",
    )(gdn_t, ba_t, alog8, dt8, gw, x2, ob, w_out)


def _attn_kernel(lam_ref, q_ref, k_ref, v_ref, z_ref, ks_ref, sw_ref, o_ref,
                 acc_ref, l_ref, m_ref, p_ref, *, lam_init):
    for sub in range(ATT_SUB):
        _attn_tile(sub, lam_ref, q_ref, k_ref, v_ref, z_ref, ks_ref, sw_ref, o_ref,
                   acc_ref.at[sub], l_ref.at[sub], m_ref, p_ref, lam_init)


def _attn_tile(sub, lam_ref, q_ref, k_ref, v_ref, z_ref, ks_ref, sw_ref, o_ref,
               acc_ref, l_ref, m_ref, p_ref, lam_init):
    tq, tk = ATT_TQ, ATT_TK
    dv = 2 * DIFF_D
    cols = slice(sub * tq, (sub + 1) * tq)
    i = pl.program_id(1) * ATT_SUB + sub
    qt = q_ref[:, cols]
    row = lax.broadcasted_iota(jnp.int32, qt.shape, 0)
    zero = jnp.zeros_like(qt)
    q_cat = jnp.concatenate(
        [jnp.where(row < DIFF_D, qt, zero), jnp.where(row >= DIFF_D, qt, zero)], axis=1)

    def scores(j):
        start = pl.multiple_of(j * tk, tk)
        return jnp.dot(k_ref[pl.ds(start, tk), :], q_cat, preferred_element_type=F32)

    def flush(j_pending, alpha):
        start = pl.multiple_of(j_pending * tk, tk)
        vt = v_ref[:, pl.ds(start, tk)]
        for c in range(2):
            upd = acc_ref[c] + jnp.dot(vt, p_ref[:, c * tq:(c + 1) * tq],
                                       preferred_element_type=F32)
            acc_ref[c] = upd if alpha is None else upd * alpha[:, c * tq:(c + 1) * tq]

    def exp_store(s_j, m):
        p = jnp.exp2(s_j - m)
        p_ref[...] = p.astype(BF16)
        return jnp.sum(p.reshape(tk // 8, 8, 2 * tq), axis=0)

    def pending(j):
        return jnp.where(j == 0, i, j - 1)

    qf = qt.astype(F32)
    qq = qf * qf
    qsq = jnp.concatenate([jnp.sum(qq[:DIFF_D], axis=0, keepdims=True),
                           jnp.sum(qq[DIFF_D:], axis=0, keepdims=True)], axis=1)
    kmax = jnp.max(ks_ref[0], axis=1, keepdims=True)
    kmax = jnp.concatenate([jnp.broadcast_to(kmax[0:1], (1, tq)),
                            jnp.broadcast_to(kmax[1:2], (1, tq))], axis=1)
    frozen_ok = jnp.max(jnp.sqrt(qsq * kmax)) * NORM_SLACK <= FROZEN_MAX_BOUND

    tri = (lax.broadcasted_iota(jnp.int32, (LANES, LANES), 0)
           <= lax.broadcasted_iota(jnp.int32, (LANES, LANES), 1))
    def cat(parts):
        parts = [t for t in parts if t.shape[1] > 0]
        return parts[0] if len(parts) == 1 else jnp.concatenate(parts, axis=1)

    def widen(t, lo, w, fill):
        pad = jnp.full((t.shape[0], lo), fill, t.dtype)
        return cat([pad, t[:, :w], pad, t[:, w:]])

    strips = []
    m0 = jnp.full((1, 2 * tq), NEG_BIG, F32)
    for a in range(tk // LANES):
        lo, w = a * LANES, tq - a * LANES
        start = pl.multiple_of(i * tk + lo, LANES)
        s_a = jnp.dot(k_ref[pl.ds(start, LANES), :], cat([q_cat[:, lo:tq], q_cat[:, tq + lo:]]),
                      preferred_element_type=F32)
        s_a = cat([jnp.where(tri, s_a[:, :LANES], NEG_BIG), s_a[:, LANES:w],
                   jnp.where(tri, s_a[:, w:w + LANES], NEG_BIG), s_a[:, w + LANES:]])
        strips.append(s_a)
        m0 = jnp.maximum(m0, widen(jnp.max(s_a, axis=0, keepdims=True), lo, w, NEG_BIG))
    m_ref[...] = m0
    l0 = jnp.zeros((8, 2 * tq), F32)
    for a, s_a in enumerate(strips):
        lo, w = a * LANES, tq - a * LANES
        p = jnp.exp2(s_a - cat([m0[:, lo:tq], m0[:, tq + lo:]]))
        p_ref[lo:lo + LANES, :] = widen(p.astype(BF16), lo, w, 0.0)
        l0 = l0 + widen(jnp.sum(p.reshape(LANES // 8, 8, 2 * w), axis=0), lo, w, 0.0)
    l_ref[...] = l0
    acc_ref[...] = jnp.zeros_like(acc_ref)

    def frozen_step(j, j_pending):
        s_j = scores(j)
        flush(j_pending, None)
        l_ref[...] += exp_store(s_j, m_ref[...])

    def frozen_steps(j0, n):
        for u in range(n):
            frozen_step(j0 + u, pending(j0) if u == 0 else j0 + u - 1)

    def frozen_group(t, carry):
        frozen_steps(ATT_UNROLL * t, ATT_UNROLL)
        return carry

    def online_body(j, carry):
        s_j = scores(j)
        m_old = m_ref[...]
        m_new = jnp.maximum(m_old, jnp.max(s_j, axis=0, keepdims=True))
        alpha = jnp.exp2(m_old - m_new)
        flush(pending(j), alpha)
        m_ref[...] = m_new
        l_ref[...] = l_ref[...] * alpha + exp_store(s_j, m_new)
        return carry

    @pl.when(frozen_ok)
    def _():
        groups = lax.shift_right_logical(i, ATT_UNROLL.bit_length() - 1)
        lax.fori_loop(0, groups, frozen_group, 0)
        j0 = groups * ATT_UNROLL
        n = ATT_UNROLL // 2
        while n >= 1:
            @pl.when((i & n) != 0)
            def _(j0=j0, n=n):
                frozen_steps(j0, n)
            j0 = j0 + (i & n)
            n //= 2

    @pl.when(jnp.logical_not(frozen_ok))
    def _():
        lax.fori_loop(0, i, online_body, 0)

    flush(pending(i), None)

    lam = lam_ref[0, 0]
    l = jnp.sum(l_ref[...], axis=0, keepdims=True)
    o_t = acc_ref[0] / l[:, :tq] - lam * (acc_ref[1] / l[:, tq:])
    o_n = o_t * lax.rsqrt(jnp.mean(o_t * o_t, axis=0, keepdims=True) + SUBLN_EPS) * sw_ref[...]
    o_n = o_n * (1.0 - lam_init)
    o_ref[cols, :] = (o_n * z_ref[:, cols].astype(F32)).T.astype(o_ref.dtype)


def _attn_call(lam, q_t, k_nat, v_t, z_t, ksq, sw, lam_init):
    dv = 2 * DIFF_D
    s = k_nat.shape[0]
    h = k_nat.shape[1] // dv
    tq = ATT_TQ
    tg = ATT_SUB * tq
    assert ATT_TQ == ATT_TK
    return pl.pallas_call(
        functools.partial(_attn_kernel, lam_init=lam_init),
        grid=(h, s // tg),
        in_specs=[
            pl.BlockSpec(memory_space=pltpu.SMEM),
            pl.BlockSpec((dv, tg), lambda hh, i: (hh, i)),
            pl.BlockSpec((s, dv), lambda hh, i: (0, hh)),
            pl.BlockSpec((dv, s), lambda hh, i: (hh, 0)),
            pl.BlockSpec((dv, tg), lambda hh, i: (hh, i)),
            pl.BlockSpec((1, 2, s), lambda hh, i: (hh, 0, 0)),
            pl.BlockSpec((dv, tq), lambda hh, i: (0, 0)),
        ],
        out_specs=pl.BlockSpec((tg, dv), lambda hh, i: (i, hh)),
        out_shape=jax.ShapeDtypeStruct((s, h * dv), BF16),
        scratch_shapes=[
            pltpu.VMEM((ATT_SUB, 2, dv, tq), F32),
            pltpu.VMEM((ATT_SUB, 8, 2 * tq), F32),
            pltpu.VMEM((1, 2 * tq), F32),
            pltpu.VMEM((ATT_TK, 2 * tq), BF16),
        ],
        compiler_params=pltpu.CompilerParams(
            dimension_semantics=("arbitrary", "arbitrary"), vmem_limit_bytes=VMEM_LIMIT),
        name="attn",
    )(lam, q_t, k_nat, v_t, z_t, ksq, sw)


def _layer(l, x2, cos_t, sin_t, w_norm, w_in, conv_w, a_log, dt_bias, gdn_norm_w, q_norm_w,
           k_norm_w, lambda_q1, lambda_k1, lambda_q2, lambda_k2, subln_w, w_out):
    s, d = x2.shape
    nqk = GDN_HEADS * GDN_D
    ba_lo = 4 * nqk
    ba_hi = ba_lo + 2 * GDN_HEADS
    wt = jnp.concatenate([w_in[:, :ba_lo], w_in[:, ba_hi:]], axis=1).T.astype(BF16)
    wba = w_in[:, ba_lo:ba_hi].T.astype(BF16)
    qw = jnp.broadcast_to(q_norm_w[:, None], (DIFF_D, PROJ_TM)).astype(F32)
    kw = jnp.broadcast_to(k_norm_w[:, None], (DIFF_D, PROJ_TM)).astype(F32)

    cw = jnp.broadcast_to(conv_w.astype(F32)[:, :, None], (CONV_K, 3 * nqk, LANES))
    gdn_t, ba_t, qb_t, kb, ksq, vb_t, zb_t = _proj_call(
        x2, w_norm[None, :].astype(F32), wt, wba, cos_t, sin_t, qw, kw, cw)

    zeros4 = jnp.zeros((GDN_HEADS, LANES), F32)
    alog8 = jnp.concatenate([zeros4, jnp.broadcast_to(a_log.astype(F32)[:, None], (GDN_HEADS, LANES))], 0)
    dt8 = jnp.concatenate([zeros4, jnp.broadcast_to(dt_bias.astype(F32)[:, None], (GDN_HEADS, LANES))], 0)
    gw = jnp.broadcast_to(gdn_norm_w.astype(F32)[:, None], (GDN_D, LANES))

    lam_init = 0.8 - 0.6 * math.exp(-0.3 * l)
    lam = (jnp.exp(jnp.sum(lambda_q1.astype(F32) * lambda_k1.astype(F32)))
           - jnp.exp(jnp.sum(lambda_q2.astype(F32) * lambda_k2.astype(F32))) + lam_init)
    dv = 2 * DIFF_D
    sw = jnp.broadcast_to(subln_w.astype(F32)[:, None], (dv, ATT_TQ))
    ob = _attn_call(lam.reshape(1, 1).astype(F32), qb_t, kb, vb_t, zb_t, ksq, sw, lam_init)

    return _gdn_call(gdn_t, ba_t, alog8, dt8, gw, x2, ob, w_out.astype(BF16))


def kernel(x, w_norm, w_in, conv_w, a_log, dt_bias, gdn_norm_w, q_norm_w, k_norm_w,
           lambda_q1, lambda_k1, lambda_q2, lambda_k2, subln_w, w_out):
    b, s, d = x.shape
    assert b == 1
    inv_freq = ROPE_THETA ** (-jnp.arange(0, DIFF_D, 2, dtype=jnp.float32) / DIFF_D)
    ang = jnp.arange(s, dtype=jnp.float32)[:, None] * inv_freq[None, :]
    cos_t, sin_t = jnp.cos(ang).T, jnp.sin(ang).T
    x2 = x[0]
    for l in range(w_norm.shape[0]):
        x2 = _layer(l, x2, cos_t, sin_t, w_norm[l], w_in[l], conv_w[l], a_log[l], dt_bias[l],
                    gdn_norm_w[l], q_norm_w[l], k_norm_w[l], lambda_q1[l], lambda_k1[l],
                    lambda_q2[l], lambda_k2[l], subln_w[l], w_out[l])
    return x2[None]
```

```python
import functools
import math

import jax
import jax.numpy as jnp
from jax import lax
from jax.experimental import pallas as pl
from jax.experimental.pallas import tpu as pltpu

F32 = jnp.float32
BF16 = jnp.bfloat16

GDN_HEADS = 4
GDN_D = 128
CONV_K = 4
DIFF_HEADS = 4
DIFF_D = 64
ROPE_THETA = 10000.0
EPS = 1e-6
SUBLN_EPS = 1e-5

LANES = 128
CHUNK = LANES
NEG_BIG = -1e30
LOG2E = 1.4426950408889634
FROZEN_MAX_BOUND = 40.0
NORM_SLACK = 1.01

PROJ_TM = 512
GDN_TB = 512
ATT_TQ = 512
ATT_TK = 512
ATT_UNROLL = 8
ATT_SUB = 2
OUT_PIECE = 256
VMEM_LIMIT = 56 * 1024 * 1024

NT_DIMS = (((1,), (1,)), ((), ()))
TN_DIMS = (((0,), (0,)), ((), ()))


def _sigmoid(v):
    return 1.0 / (1.0 + jnp.exp(-v))


def _silu(v):
    return v * _sigmoid(v)


def _proj_kernel(x_ref, wn_ref, wt_ref, wba_ref, ca_ref, sa_ref, cb_ref, sb_ref, qw_ref, kw_ref, cw_ref,
                 gdn_ref, ba_ref, qb_ref, kb_ref, ks_ref, vb_ref, zb_ref, halo_ref):
    @pl.when(pl.program_id(0) == 0)
    def _():
        halo_ref[...] = jnp.zeros_like(halo_ref)

    x = x_ref[...]
    ms = jnp.mean(x * x, axis=-1, keepdims=True)
    hn = (x * lax.rsqrt(ms + EPS) * wn_ref[...]).astype(BF16)

    def proj_t(lo, n):
        return lax.dot_general(wt_ref[lo:lo + n, :], hn, NT_DIMS, preferred_element_type=F32)

    blk = 512
    tm = x.shape[0]
    lane_t = lax.broadcasted_iota(jnp.int32, (blk, LANES), 1)

    def conv_block(b):
        rows = slice(b * blk, (b + 1) * blk)
        raw = proj_t(b * blk, blk)
        tail = halo_ref[rows, :]
        halo_ref[rows, :] = raw[:, tm - LANES:]
        y = jnp.concatenate([cw_ref[CONV_K - 1, rows, :]] * (tm // LANES), axis=1) * raw
        for j in range(1, CONV_K):
            rolled = pltpu.roll(raw, j, axis=1)
            first = jnp.where(lane_t >= j, rolled[:, :LANES], pltpu.roll(tail, j, axis=1))
            shifted = jnp.concatenate([first, rolled[:, LANES:]], axis=1)
            y = y + jnp.concatenate([cw_ref[CONV_K - 1 - j, rows, :]] * (tm // LANES), axis=1) * shifted
        gdn_ref[rows, :] = _silu(y).astype(BF16)

    q_raw = proj_t(4 * blk, blk)
    k_raw = proj_t(5 * blk, blk)

    ca, sa, cb, sb = ca_ref[0], sa_ref[0], cb_ref[...], sb_ref[...]
    cos = jnp.concatenate([ca[:, c:c + 1] * cb - sa[:, c:c + 1] * sb for c in range(tm // LANES)], axis=1)
    sin = jnp.concatenate([sa[:, c:c + 1] * cb + ca[:, c:c + 1] * sb for c in range(tm // LANES)], axis=1)
    half = DIFF_D // 2

    def norm_rope(t, w, scale):
        outs = []
        for g in range(t.shape[0] // DIFF_D):
            tg = t[g * DIFF_D:(g + 1) * DIFF_D, :]
            tg = tg * lax.rsqrt(jnp.mean(tg * tg, axis=0, keepdims=True) + EPS) * w
            t1, t2 = tg[:half, :], tg[half:, :]
            outs.append((t1 * cos - t2 * sin) * scale)
            outs.append((t2 * cos + t1 * sin) * scale)
        return jnp.concatenate(outs, axis=0)

    qb_ref[...] = norm_rope(q_raw, qw_ref[...], LOG2E / math.sqrt(DIFF_D)).astype(BF16)
    kt = norm_rope(k_raw, kw_ref[...], 1.0)
    for g in range(2 * DIFF_HEADS):
        kg = kt[g * DIFF_D:(g + 1) * DIFF_D, :]
        ks_ref[g // 2, g % 2:g % 2 + 1, :] = jnp.sum(kg * kg, axis=0, keepdims=True)
    kb_ref[...] = kt.T.astype(BF16)

    for b in range(3):
        conv_block(b)
    gdn_ref[3 * blk:4 * blk, :] = _silu(proj_t(3 * blk, blk)).astype(BF16)
    zb_ref[...] = _silu(proj_t(7 * blk, blk)).astype(BF16)
    ba_ref[...] = lax.dot_general(wba_ref[...], hn, NT_DIMS, preferred_element_type=F32)
    vb_ref[...] = proj_t(6 * blk, blk).astype(BF16)


def _proj_call(x2, w_norm, wt, wba, rope, qw, kw, cw):
    s, d = x2.shape
    tm = PROJ_TM
    cos_a, sin_a, cos_b, sin_b = rope
    const = lambda shape: pl.BlockSpec(shape, lambda i: (0,) * len(shape))
    cols = lambda rows: pl.BlockSpec((rows, tm), lambda i: (0, i))
    coarse = pl.BlockSpec((1,) + cos_a.shape[1:], lambda i: (i, 0, 0))
    return pl.pallas_call(
        _proj_kernel,
        grid=(s // tm,),
        in_specs=[
            pl.BlockSpec((tm, d), lambda i: (i, 0)),
            const((1, d)),
            const(wt.shape),
            const(wba.shape),
            coarse,
            coarse,
            const(cos_b.shape),
            const(sin_b.shape),
            const((DIFF_D, tm)),
            const((DIFF_D, tm)),
            const(cw.shape),
        ],
        out_specs=[
            cols(2048),
            cols(8),
            cols(512),
            pl.BlockSpec((tm, 512), lambda i: (i, 0)),
            pl.BlockSpec((DIFF_HEADS, 2, tm), lambda i: (0, 0, i)),
            cols(512),
            cols(512),
        ],
        out_shape=[
            jax.ShapeDtypeStruct((2048, s), BF16),
            jax.ShapeDtypeStruct((8, s), F32),
            jax.ShapeDtypeStruct((512, s), BF16),
            jax.ShapeDtypeStruct((s, 512), BF16),
            jax.ShapeDtypeStruct((DIFF_HEADS, 2, s), F32),
            jax.ShapeDtypeStruct((512, s), BF16),
            jax.ShapeDtypeStruct((512, s), BF16),
        ],
        scratch_shapes=[
            pltpu.VMEM((cw.shape[1], LANES), F32),
        ],
        compiler_params=pltpu.CompilerParams(
            dimension_semantics=("arbitrary",), vmem_limit_bytes=VMEM_LIMIT),
        name="proj",
    )(x2, w_norm, wt, wba, cos_a, sin_a, cos_b, sin_b, qw, kw, cw)


def _gdn_kernel(g_ref, ba_ref, alog_ref, dt_ref, gw_ref, x_ref, ob_ref, wo_ref,
                y_ref, st_ref):
    nqkv = 3 * GDN_HEADS * GDN_D
    n_chunks = g_ref.shape[1] // CHUNK
    n_a = GDN_HEADS * GDN_D

    @pl.when(pl.program_id(0) == 0)
    def _():
        st_ref[...] = jnp.zeros_like(st_ref)

    for lo in range(0, y_ref.shape[1], OUT_PIECE):
        _out_b_piece(x_ref, ob_ref, wo_ref, y_ref, n_a, lo)

    lane = lax.broadcasted_iota(jnp.int32, (CHUNK, CHUNK), 1)
    subl = lax.broadcasted_iota(jnp.int32, (CHUNK, CHUNK), 0)
    eye = (lane == subl).astype(F32)
    lane8 = lax.broadcasted_iota(jnp.int32, (8, CHUNK), 1)

    chains = []
    for c in range(n_chunks):
        lanes = slice(c * CHUNK, (c + 1) * CHUNK)
        y = g_ref[0:nqkv, lanes].astype(F32)

        ba = ba_ref[:, lanes]
        beta8 = _sigmoid(ba)
        sp = ba + dt_ref[...]
        softplus = jnp.maximum(sp, 0.0) + jnp.log(1.0 + jnp.exp(-jnp.abs(sp)))
        gc8 = (-LOG2E) * jnp.exp(alog_ref[...]) * softplus
        sh = 1
        while sh < CHUNK:
            gc8 = gc8 + jnp.where(lane8 >= sh, pltpu.roll(gc8, sh, axis=1), 0.0)
            sh *= 2

        for h in range(GDN_HEADS):
            qt = y[h * GDN_D:(h + 1) * GDN_D, :]
            kt = y[(GDN_HEADS + h) * GDN_D:(GDN_HEADS + h + 1) * GDN_D, :]
            vt = y[(2 * GDN_HEADS + h) * GDN_D:(2 * GDN_HEADS + h + 1) * GDN_D, :]
            qt = qt * lax.rsqrt(jnp.sum(qt * qt, axis=0, keepdims=True) + EPS) * (GDN_D ** -0.5)
            kt = kt * lax.rsqrt(jnp.sum(kt * kt, axis=0, keepdims=True) + EPS)
            beta = beta8[h:h + 1, :]
            gc = gc8[GDN_HEADS + h:GDN_HEADS + h + 1, :]
            g_row = jnp.broadcast_to(gc, (CHUNK, CHUNK))
            g_col = g_row.T
            g_last = g_col[CHUNK - 1:CHUNK, :]
            e_gc = jnp.exp2(gc)
            chains.append(dict(
                c=c, h=h, qt=qt, kt=kt, beta=beta, g_last=g_last, g_col=g_col,
                decay_t=jnp.exp2(jnp.where(lane >= subl, g_row - g_col, NEG_BIG)),
                k_nat=kt.T,
                qg=qt * e_gc,
                rhs=jnp.concatenate([vt * beta, kt * (beta * e_gc)], axis=0).astype(BF16)))

    for ch in chains:
        gram = jnp.dot(ch["k_nat"].astype(BF16),
                       jnp.concatenate([ch["kt"], ch["qt"]], axis=1).astype(BF16),
                       preferred_element_type=F32)
        ch["a_t"] = gram[:, CHUNK:] * ch["decay_t"]
        ch["n"] = jnp.where(lane > subl, gram[:, :CHUNK] * ch["decay_t"] * (-ch["beta"]), 0.0)

    for ch in chains:
        nb = ch["n"].astype(BF16)
        ch["q"] = eye + ch["n"]
        ch["n"] = jnp.dot(nb, nb, preferred_element_type=F32)
    m = 2
    while 2 * m < CHUNK:
        for ch in chains:
            nb = ch["n"].astype(BF16)
            both = jnp.dot(jnp.concatenate([ch["q"].astype(BF16), nb], axis=0), nb,
                           preferred_element_type=F32)
            ch["q"] = ch["q"] + both[:CHUNK]
            ch["n"] = both[CHUNK:]
        m *= 2
    for ch in chains:
        ch["q"] = ch["q"] + jnp.dot(ch["q"].astype(BF16), ch["n"].astype(BF16),
                                    preferred_element_type=F32)

    for ch in chains:
        uw = jnp.dot(ch["rhs"], ch["q"].astype(BF16), preferred_element_type=F32)
        ch["u_t"] = uw[:GDN_D, :]
        ch["r1"] = jnp.concatenate([uw[GDN_D:, :], ch["qg"]], axis=1).astype(BF16)
        k_dec = ch["k_nat"] * jnp.exp2(ch["g_last"] - ch["g_col"])
        ch["r2"] = jnp.concatenate([k_dec, ch["a_t"]], axis=1).astype(BF16)

    def out_a_piece(c, oa_chunk):
        rows = slice(c * CHUNK, (c + 1) * CHUNK)
        y_ref[rows, :] += lax.dot_general(oa_chunk, wo_ref[0:n_a, :], TN_DIMS,
                                          preferred_element_type=F32)

    states = [st_ref[h] for h in range(GDN_HEADS)]
    ready = None
    for c in range(n_chunks):
        lanes = slice(c * CHUNK, (c + 1) * CHUNK)
        row = chains[c * GDN_HEADS:(c + 1) * GDN_HEADS]
        x1 = [jnp.dot(states[h].astype(BF16), row[h]["r1"], preferred_element_type=F32)
              for h in range(GDN_HEADS)]
        if ready is not None:
            out_a_piece(*ready)
        x2 = [jnp.dot((row[h]["u_t"] - x1[h][:, :CHUNK]).astype(BF16), row[h]["r2"],
                      preferred_element_type=F32) for h in range(GDN_HEADS)]
        oa = []
        for h in range(GDN_HEADS):
            states[h] = states[h] * jnp.exp2(row[h]["g_last"]) + x2[h][:, :GDN_D]
            o_t = x1[h][:, CHUNK:] + x2[h][:, GDN_D:]
            o_n = o_t * lax.rsqrt(jnp.mean(o_t * o_t, axis=0, keepdims=True) + EPS) * gw_ref[...]
            gate = g_ref[nqkv + h * GDN_D:nqkv + (h + 1) * GDN_D, lanes].astype(F32)
            oa.append((o_n * gate).astype(BF16))
        ready = (c, jnp.concatenate(oa, axis=0))
    out_a_piece(*ready)
    for h in range(GDN_HEADS):
        st_ref[h] = states[h]


def _out_b_piece(x_ref, ob_ref, wo_ref, y_ref, n_a, lo):
    cols = slice(lo, lo + OUT_PIECE)
    y_ref[:, cols] = x_ref[:, cols] + jnp.dot(
        ob_ref[...], wo_ref[n_a:, cols], preferred_element_type=F32)


def _gdn_call(gdn_t, ba_t, alog8, dt8, gw, x2, ob, w_out):
    s, d = x2.shape
    tb = GDN_TB
    const = lambda shape: pl.BlockSpec(shape, lambda i: (0,) * len(shape))
    return pl.pallas_call(
        _gdn_kernel,
        grid=(s // tb,),
        in_specs=[
            pl.BlockSpec((gdn_t.shape[0], tb), lambda i: (0, i)),
            pl.BlockSpec((8, tb), lambda i: (0, i)),
            const(alog8.shape),
            const(dt8.shape),
            const(gw.shape),
            pl.BlockSpec((tb, d), lambda i: (i, 0)),
            pl.BlockSpec((tb, ob.shape[1]), lambda i: (i, 0)),
            const(w_out.shape),
        ],
        out_specs=pl.BlockSpec((tb, d), lambda i: (i, 0)),
        out_shape=jax.ShapeDtypeStruct((s, d), F32),
        scratch_shapes=[
            pltpu.VMEM((GDN_HEADS, GDN_D, GDN_D), F32),
        ],
        compiler_params=pltpu.CompilerParams(
            dimension_semantics=("arbitrary",), vmem_limit_bytes=VMEM_LIMIT),
        name="gdn",
    )(gdn_t, ba_t, alog8, dt8, gw, x2, ob, w_out)


def _attn_kernel(lam_ref, q_ref, k_ref, v_ref, z_ref, ks_ref, sw_ref, o_ref,
                 acc_ref, l_ref, m_ref, p_ref, *, lam_init):
    for sub in range(ATT_SUB):
        _attn_tile(sub, lam_ref, q_ref, k_ref, v_ref, z_ref, ks_ref, sw_ref, o_ref,
                   acc_ref.at[sub], l_ref.at[sub], m_ref, p_ref, lam_init)


def _attn_tile(sub, lam_ref, q_ref, k_ref, v_ref, z_ref, ks_ref, sw_ref, o_ref,
               acc_ref, l_ref, m_ref, p_ref, lam_init):
    tq, tk = ATT_TQ, ATT_TK
    dv = 2 * DIFF_D
    cols = slice(sub * tq, (sub + 1) * tq)
    i = pl.program_id(1) * ATT_SUB + sub
    qt = q_ref[:, cols]
    row = lax.broadcasted_iota(jnp.int32, qt.shape, 0)
    zero = jnp.zeros_like(qt)
    q_cat = jnp.concatenate(
        [jnp.where(row < DIFF_D, qt, zero), jnp.where(row >= DIFF_D, qt, zero)], axis=1)

    def scores(j):
        start = pl.multiple_of(j * tk, tk)
        return jnp.dot(k_ref[pl.ds(start, tk), :], q_cat, preferred_element_type=F32)

    def flush(j_pending, alpha):
        start = pl.multiple_of(j_pending * tk, tk)
        vt = v_ref[:, pl.ds(start, tk)]
        for c in range(2):
            upd = acc_ref[c] + jnp.dot(vt, p_ref[:, c * tq:(c + 1) * tq],
                                       preferred_element_type=F32)
            acc_ref[c] = upd if alpha is None else upd * alpha[:, c * tq:(c + 1) * tq]

    def exp_store(s_j, m):
        p = jnp.exp2(s_j if m is None else s_j - m)
        p_ref[...] = p.astype(BF16)
        return jnp.sum(p.reshape(tk // 8, 8, 2 * tq), axis=0)

    def pending(j):
        return jnp.where(j == 0, i, j - 1)

    qf = qt.astype(F32)
    qq = qf * qf
    qsq = jnp.concatenate([jnp.sum(qq[:DIFF_D], axis=0, keepdims=True),
                           jnp.sum(qq[DIFF_D:], axis=0, keepdims=True)], axis=1)
    kmax = jnp.max(ks_ref[0], axis=1, keepdims=True)
    kmax = jnp.concatenate([jnp.broadcast_to(kmax[0:1], (1, tq)),
                            jnp.broadcast_to(kmax[1:2], (1, tq))], axis=1)
    frozen_ok = jnp.max(jnp.sqrt(qsq * kmax)) * NORM_SLACK <= FROZEN_MAX_BOUND

    tri = (lax.broadcasted_iota(jnp.int32, (LANES, LANES), 0)
           <= lax.broadcasted_iota(jnp.int32, (LANES, LANES), 1))
    def cat(parts):
        parts = [t for t in parts if t.shape[1] > 0]
        return parts[0] if len(parts) == 1 else jnp.concatenate(parts, axis=1)

    def widen(t, lo, w, fill):
        pad = jnp.full((t.shape[0], lo), fill, t.dtype)
        return cat([pad, t[:, :w], pad, t[:, w:]])

    strips = []
    m0 = jnp.full((1, 2 * tq), NEG_BIG, F32)
    for a in range(tk // LANES):
        lo, w = a * LANES, tq - a * LANES
        start = pl.multiple_of(i * tk + lo, LANES)
        s_a = jnp.dot(k_ref[pl.ds(start, LANES), :], cat([q_cat[:, lo:tq], q_cat[:, tq + lo:]]),
                      preferred_element_type=F32)
        s_a = cat([jnp.where(tri, s_a[:, :LANES], NEG_BIG), s_a[:, LANES:w],
                   jnp.where(tri, s_a[:, w:w + LANES], NEG_BIG), s_a[:, w + LANES:]])
        strips.append(s_a)
        m0 = jnp.maximum(m0, widen(jnp.max(s_a, axis=0, keepdims=True), lo, w, NEG_BIG))
    m0 = jnp.where(frozen_ok, 0.0, m0)
    m_ref[...] = m0
    l0 = jnp.zeros((8, 2 * tq), F32)
    for a, s_a in enumerate(strips):
        lo, w = a * LANES, tq - a * LANES
        p = jnp.exp2(s_a - cat([m0[:, lo:tq], m0[:, tq + lo:]]))
        p_ref[lo:lo + LANES, :] = widen(p.astype(BF16), lo, w, 0.0)
        l0 = l0 + widen(jnp.sum(p.reshape(LANES // 8, 8, 2 * w), axis=0), lo, w, 0.0)
    l_ref[...] = l0
    acc_ref[...] = jnp.zeros_like(acc_ref)

    def frozen_step(j, j_pending):
        s_j = scores(j)
        flush(j_pending, None)
        l_ref[...] += exp_store(s_j, None)

    def frozen_steps(j0, n):
        for u in range(n):
            frozen_step(j0 + u, pending(j0) if u == 0 else j0 + u - 1)

    def frozen_group(t, carry):
        frozen_steps(ATT_UNROLL * t, ATT_UNROLL)
        return carry

    def online_body(j, carry):
        s_j = scores(j)
        m_old = m_ref[...]
        m_new = jnp.maximum(m_old, jnp.max(s_j, axis=0, keepdims=True))
        alpha = jnp.exp2(m_old - m_new)
        flush(pending(j), alpha)
        m_ref[...] = m_new
        l_ref[...] = l_ref[...] * alpha + exp_store(s_j, m_new)
        return carry

    @pl.when(frozen_ok)
    def _():
        groups = lax.shift_right_logical(i, ATT_UNROLL.bit_length() - 1)
        lax.fori_loop(0, groups, frozen_group, 0)
        j0 = groups * ATT_UNROLL
        n = ATT_UNROLL // 2
        while n >= 1:
            @pl.when((i & n) != 0)
            def _(j0=j0, n=n):
                frozen_steps(j0, n)
            j0 = j0 + (i & n)
            n //= 2

    @pl.when(jnp.logical_not(frozen_ok))
    def _():
        lax.fori_loop(0, i, online_body, 0)

    flush(pending(i), None)

    lam = lam_ref[0, 0]
    l = jnp.sum(l_ref[...], axis=0, keepdims=True)
    o_t = acc_ref[0] / l[:, :tq] - lam * (acc_ref[1] / l[:, tq:])
    o_n = o_t * lax.rsqrt(jnp.mean(o_t * o_t, axis=0, keepdims=True) + SUBLN_EPS) * sw_ref[...]
    o_n = o_n * (1.0 - lam_init)
    o_ref[cols, :] = (o_n * z_ref[:, cols].astype(F32)).T.astype(o_ref.dtype)


def _attn_call(lam, q_t, k_nat, v_t, z_t, ksq, sw, lam_init):
    dv = 2 * DIFF_D
    s = k_nat.shape[0]
    h = k_nat.shape[1] // dv
    tq = ATT_TQ
    tg = ATT_SUB * tq
    assert ATT_TQ == ATT_TK
    return pl.pallas_call(
        functools.partial(_attn_kernel, lam_init=lam_init),
        grid=(h, s // tg),
        in_specs=[
            pl.BlockSpec(memory_space=pltpu.SMEM),
            pl.BlockSpec((dv, tg), lambda hh, i: (hh, i)),
            pl.BlockSpec((s, dv), lambda hh, i: (0, hh)),
            pl.BlockSpec((dv, s), lambda hh, i: (hh, 0)),
            pl.BlockSpec((dv, tg), lambda hh, i: (hh, i)),
            pl.BlockSpec((1, 2, s), lambda hh, i: (hh, 0, 0)),
            pl.BlockSpec((dv, tq), lambda hh, i: (0, 0)),
        ],
        out_specs=pl.BlockSpec((tg, dv), lambda hh, i: (i, hh)),
        out_shape=jax.ShapeDtypeStruct((s, h * dv), BF16),
        scratch_shapes=[
            pltpu.VMEM((ATT_SUB, 2, dv, tq), F32),
            pltpu.VMEM((ATT_SUB, 8, 2 * tq), F32),
            pltpu.VMEM((1, 2 * tq), F32),
            pltpu.VMEM((ATT_TK, 2 * tq), BF16),
        ],
        compiler_params=pltpu.CompilerParams(
            dimension_semantics=("arbitrary", "arbitrary"), vmem_limit_bytes=VMEM_LIMIT),
        name="attn",
    )(lam, q_t, k_nat, v_t, z_t, ksq, sw)


def _layer(l, x2, rope, w_norm, w_in, conv_w, a_log, dt_bias, gdn_norm_w, q_norm_w,
           k_norm_w, lambda_q1, lambda_k1, lambda_q2, lambda_k2, subln_w, w_out):
    s, d = x2.shape
    nqk = GDN_HEADS * GDN_D
    ba_lo = 4 * nqk
    ba_hi = ba_lo + 2 * GDN_HEADS
    wt = jnp.concatenate([w_in[:, :ba_lo], w_in[:, ba_hi:]], axis=1).T.astype(BF16)
    wba = w_in[:, ba_lo:ba_hi].T.astype(BF16)
    qw = jnp.broadcast_to(q_norm_w[:, None], (DIFF_D, PROJ_TM)).astype(F32)
    kw = jnp.broadcast_to(k_norm_w[:, None], (DIFF_D, PROJ_TM)).astype(F32)

    cw = jnp.broadcast_to(conv_w.astype(F32)[:, :, None], (CONV_K, 3 * nqk, LANES))
    gdn_t, ba_t, qb_t, kb, ksq, vb_t, zb_t = _proj_call(
        x2, w_norm[None, :].astype(F32), wt, wba, rope, qw, kw, cw)

    zeros4 = jnp.zeros((GDN_HEADS, LANES), F32)
    alog8 = jnp.concatenate([zeros4, jnp.broadcast_to(a_log.astype(F32)[:, None], (GDN_HEADS, LANES))], 0)
    dt8 = jnp.concatenate([zeros4, jnp.broadcast_to(dt_bias.astype(F32)[:, None], (GDN_HEADS, LANES))], 0)
    gw = jnp.broadcast_to(gdn_norm_w.astype(F32)[:, None], (GDN_D, LANES))

    lam_init = 0.8 - 0.6 * math.exp(-0.3 * l)
    lam = (jnp.exp(jnp.sum(lambda_q1.astype(F32) * lambda_k1.astype(F32)))
           - jnp.exp(jnp.sum(lambda_q2.astype(F32) * lambda_k2.astype(F32))) + lam_init)
    dv = 2 * DIFF_D
    sw = jnp.broadcast_to(subln_w.astype(F32)[:, None], (dv, ATT_TQ))
    ob = _attn_call(lam.reshape(1, 1).astype(F32), qb_t, kb, vb_t, zb_t, ksq, sw, lam_init)

    return _gdn_call(gdn_t, ba_t, alog8, dt8, gw, x2, ob, w_out.astype(BF16))


def kernel(x, w_norm, w_in, conv_w, a_log, dt_bias, gdn_norm_w, q_norm_w, k_norm_w,
           lambda_q1, lambda_k1, lambda_q2, lambda_k2, subln_w, w_out):
    b, s, d = x.shape
    assert b == 1
    inv_freq = ROPE_THETA ** (-jnp.arange(0, DIFF_D, 2, dtype=jnp.float32) / DIFF_D)
    per_tile = PROJ_TM // LANES
    coarse = jnp.arange(0, s, LANES, dtype=jnp.float32)[:, None] * inv_freq[None, :]
    coarse = coarse.reshape(s // PROJ_TM, per_tile, DIFF_D // 2).transpose(0, 2, 1)
    fine = inv_freq[:, None] * jnp.arange(LANES, dtype=jnp.float32)[None, :]
    rope = (jnp.cos(coarse), jnp.sin(coarse), jnp.cos(fine), jnp.sin(fine))
    x2 = x[0]
    for l in range(w_norm.shape[0]):
        x2 = _layer(l, x2, rope, w_norm[l], w_in[l], conv_w[l], a_log[l], dt_bias[l],
                    gdn_norm_w[l], q_norm_w[l], k_norm_w[l], lambda_q1[l], lambda_k1[l],
                    lambda_q2[l], lambda_k2[l], subln_w[l], w_out[l])
    return x2[None]
```

```python
import functools
import math

import jax
import jax.numpy as jnp
from jax import lax
from jax.experimental import pallas as pl
from jax.experimental.pallas import tpu as pltpu

F32 = jnp.float32
BF16 = jnp.bfloat16

GDN_HEADS = 4
GDN_D = 128
CONV_K = 4
DIFF_HEADS = 4
DIFF_D = 64
ROPE_THETA = 10000.0
EPS = 1e-6
SUBLN_EPS = 1e-5

HEAD_GROUP_ROWS = GDN_HEADS * GDN_D
assert HEAD_GROUP_ROWS == DIFF_HEADS * 2 * DIFF_D

LANES = 128
CHUNK = LANES
NEG_BIG = -1e30
LOG2E = 1.4426950408889634
FROZEN_MAX_BOUND = 40.0
NORM_SLACK = 1.01

PROJ_TM = 1024
GDN_TB = 512
ATT_TQ = 512
ATT_TK = 512
ATT_UNROLL = 8
ATT_SUB = 2
OUT_PIECE = 256
VMEM_LIMIT = 56 * 1024 * 1024

NT_DIMS = (((1,), (1,)), ((), ()))
TN_DIMS = (((0,), (0,)), ((), ()))


def _sigmoid(v):
    return 1.0 / (1.0 + jnp.exp(-v))


def _silu(v):
    return v * _sigmoid(v)


def _proj_kernel(x_ref, wn_ref, wa_ref, wb_ref, wba_ref, ca_ref, sa_ref, cb_ref, sb_ref,
                 qw_ref, kw_ref, cw_ref,
                 gdn_ref, ba_ref, qb_ref, kb_ref, ks_ref, vb_ref, zb_ref, halo_ref):
    @pl.when(pl.program_id(0) == 0)
    def _():
        halo_ref[...] = jnp.zeros_like(halo_ref)

    x = x_ref[...]
    ms = jnp.mean(x * x, axis=-1, keepdims=True)
    hn = (x * lax.rsqrt(ms + EPS) * wn_ref[...]).astype(BF16)

    blk = HEAD_GROUP_ROWS
    n_group_a = wa_ref.shape[0]

    def proj_t(lo, n):
        w = wa_ref[lo:lo + n, :] if lo < n_group_a else wb_ref[lo - n_group_a:lo - n_group_a + n, :]
        return lax.dot_general(w, hn, NT_DIMS, preferred_element_type=F32)

    tm = x.shape[0]
    lane_t = lax.broadcasted_iota(jnp.int32, (blk, LANES), 1)

    def conv_block(b):
        rows = slice(b * blk, (b + 1) * blk)
        raw = proj_t(b * blk, blk)
        tail = halo_ref[rows, :]
        halo_ref[rows, :] = raw[:, tm - LANES:]
        y = jnp.concatenate([cw_ref[CONV_K - 1, rows, :]] * (tm // LANES), axis=1) * raw
        for j in range(1, CONV_K):
            rolled = pltpu.roll(raw, j, axis=1)
            first = jnp.where(lane_t >= j, rolled[:, :LANES], pltpu.roll(tail, j, axis=1))
            shifted = jnp.concatenate([first, rolled[:, LANES:]], axis=1)
            y = y + jnp.concatenate([cw_ref[CONV_K - 1 - j, rows, :]] * (tm // LANES), axis=1) * shifted
        gdn_ref[rows, :] = _silu(y).astype(BF16)

    q_raw = proj_t(4 * blk, blk)
    k_raw = proj_t(5 * blk, blk)

    ca, sa, cb, sb = ca_ref[0], sa_ref[0], cb_ref[...], sb_ref[...]
    cos = jnp.concatenate([ca[:, c:c + 1] * cb - sa[:, c:c + 1] * sb for c in range(tm // LANES)], axis=1)
    sin = jnp.concatenate([sa[:, c:c + 1] * cb + ca[:, c:c + 1] * sb for c in range(tm // LANES)], axis=1)
    half = DIFF_D // 2

    def norm_rope(t, w, scale):
        outs = []
        for g in range(t.shape[0] // DIFF_D):
            tg = t[g * DIFF_D:(g + 1) * DIFF_D, :]
            tg = tg * lax.rsqrt(jnp.mean(tg * tg, axis=0, keepdims=True) + EPS) * w
            t1, t2 = tg[:half, :], tg[half:, :]
            outs.append((t1 * cos - t2 * sin) * scale)
            outs.append((t2 * cos + t1 * sin) * scale)
        return jnp.concatenate(outs, axis=0)

    qb_ref[...] = norm_rope(q_raw, qw_ref[...], LOG2E / math.sqrt(DIFF_D)).astype(BF16)
    kt = norm_rope(k_raw, kw_ref[...], 1.0)
    for g in range(2 * DIFF_HEADS):
        kg = kt[g * DIFF_D:(g + 1) * DIFF_D, :]
        ks_ref[g // 2, g % 2:g % 2 + 1, :] = jnp.sum(kg * kg, axis=0, keepdims=True)
    kb_ref[...] = kt.T.astype(BF16)

    for b in range(3):
        conv_block(b)
    gdn_ref[3 * blk:4 * blk, :] = _silu(proj_t(3 * blk, blk)).astype(BF16)
    zb_ref[...] = _silu(proj_t(7 * blk, blk)).astype(BF16)
    ba_ref[...] = lax.dot_general(wba_ref[...], hn, NT_DIMS, preferred_element_type=F32)
    vb_ref[...] = proj_t(6 * blk, blk).astype(BF16)


def _proj_call(x2, w_norm, wt_a, wt_b, wba, rope, qw, kw, cw):
    s, d = x2.shape
    tm = PROJ_TM
    blk = HEAD_GROUP_ROWS
    n_gates = wba.shape[0]
    cos_a, sin_a, cos_b, sin_b = rope
    const = lambda shape: pl.BlockSpec(shape, lambda i: (0,) * len(shape))
    cols = lambda rows: pl.BlockSpec((rows, tm), lambda i: (0, i))
    coarse = pl.BlockSpec((1,) + cos_a.shape[1:], lambda i: (i, 0, 0))
    return pl.pallas_call(
        _proj_kernel,
        grid=(s // tm,),
        in_specs=[
            pl.BlockSpec((tm, d), lambda i: (i, 0)),
            const((1, d)),
            const(wt_a.shape),
            const(wt_b.shape),
            const(wba.shape),
            coarse,
            coarse,
            const(cos_b.shape),
            const(sin_b.shape),
            const((DIFF_D, tm)),
            const((DIFF_D, tm)),
            const(cw.shape),
        ],
        out_specs=[
            cols(4 * blk),
            cols(n_gates),
            cols(blk),
            pl.BlockSpec((tm, blk), lambda i: (i, 0)),
            pl.BlockSpec((DIFF_HEADS, 2, tm), lambda i: (0, 0, i)),
            cols(blk),
            cols(blk),
        ],
        out_shape=[
            jax.ShapeDtypeStruct((4 * blk, s), BF16),
            jax.ShapeDtypeStruct((n_gates, s), F32),
            jax.ShapeDtypeStruct((blk, s), BF16),
            jax.ShapeDtypeStruct((s, blk), BF16),
            jax.ShapeDtypeStruct((DIFF_HEADS, 2, s), F32),
            jax.ShapeDtypeStruct((blk, s), BF16),
            jax.ShapeDtypeStruct((blk, s), BF16),
        ],
        scratch_shapes=[
            pltpu.VMEM((cw.shape[1], LANES), F32),
        ],
        compiler_params=pltpu.CompilerParams(
            dimension_semantics=("arbitrary",), vmem_limit_bytes=VMEM_LIMIT),
        name="proj",
    )(x2, w_norm, wt_a, wt_b, wba, cos_a, sin_a, cos_b, sin_b, qw, kw, cw)


def _gdn_kernel(g_ref, ba_ref, alog_ref, dt_ref, gw_ref, x_ref, ob_ref, wo_ref,
                y_ref, st_ref):
    nqkv = 3 * GDN_HEADS * GDN_D
    n_chunks = g_ref.shape[1] // CHUNK
    n_a = GDN_HEADS * GDN_D

    @pl.when(pl.program_id(0) == 0)
    def _():
        st_ref[...] = jnp.zeros_like(st_ref)

    for lo in range(0, y_ref.shape[1], OUT_PIECE):
        _out_b_piece(x_ref, ob_ref, wo_ref, y_ref, n_a, lo)

    lane = lax.broadcasted_iota(jnp.int32, (CHUNK, CHUNK), 1)
    subl = lax.broadcasted_iota(jnp.int32, (CHUNK, CHUNK), 0)
    eye = (lane == subl).astype(F32)
    lane8 = lax.broadcasted_iota(jnp.int32, (8, CHUNK), 1)

    chains = []
    for c in range(n_chunks):
        lanes = slice(c * CHUNK, (c + 1) * CHUNK)
        y = g_ref[0:nqkv, lanes].astype(F32)

        ba = ba_ref[:, lanes]
        beta8 = _sigmoid(ba)
        sp = ba + dt_ref[...]
        softplus = jnp.maximum(sp, 0.0) + jnp.log(1.0 + jnp.exp(-jnp.abs(sp)))
        gc8 = (-LOG2E) * jnp.exp(alog_ref[...]) * softplus
        sh = 1
        while sh < CHUNK:
            gc8 = gc8 + jnp.where(lane8 >= sh, pltpu.roll(gc8, sh, axis=1), 0.0)
            sh *= 2

        for h in range(GDN_HEADS):
            qt = y[h * GDN_D:(h + 1) * GDN_D, :]
            kt = y[(GDN_HEADS + h) * GDN_D:(GDN_HEADS + h + 1) * GDN_D, :]
            vt = y[(2 * GDN_HEADS + h) * GDN_D:(2 * GDN_HEADS + h + 1) * GDN_D, :]
            qt = qt * lax.rsqrt(jnp.sum(qt * qt, axis=0, keepdims=True) + EPS) * (GDN_D ** -0.5)
            kt = kt * lax.rsqrt(jnp.sum(kt * kt, axis=0, keepdims=True) + EPS)
            beta = beta8[h:h + 1, :]
            gc = gc8[GDN_HEADS + h:GDN_HEADS + h + 1, :]
            g_row = jnp.broadcast_to(gc, (CHUNK, CHUNK))
            g_col = g_row.T
            g_last = g_col[CHUNK - 1:CHUNK, :]
            e_gc = jnp.exp2(gc)
            chains.append(dict(
                c=c, h=h, qt=qt, kt=kt, beta=beta, g_last=g_last, g_col=g_col,
                decay_t=jnp.exp2(jnp.where(lane >= subl, g_row - g_col, NEG_BIG)),
                k_nat=kt.T,
                qg=qt * e_gc,
                rhs=jnp.concatenate([vt * beta, kt * (beta * e_gc)], axis=0).astype(BF16)))

    for ch in chains:
        gram = jnp.dot(ch["k_nat"].astype(BF16),
                       jnp.concatenate([ch["kt"], ch["qt"]], axis=1).astype(BF16),
                       preferred_element_type=F32)
        ch["a_t"] = gram[:, CHUNK:] * ch["decay_t"]
        ch["n"] = jnp.where(lane > subl, gram[:, :CHUNK] * ch["decay_t"] * (-ch["beta"]), 0.0)

    for ch in chains:
        nb = ch["n"].astype(BF16)
        ch["q"] = eye + ch["n"]
        ch["n"] = jnp.dot(nb, nb, preferred_element_type=F32)
    m = 2
    while 2 * m < CHUNK:
        for ch in chains:
            nb = ch["n"].astype(BF16)
            both = jnp.dot(jnp.concatenate([ch["q"].astype(BF16), nb], axis=0), nb,
                           preferred_element_type=F32)
            ch["q"] = ch["q"] + both[:CHUNK]
            ch["n"] = both[CHUNK:]
        m *= 2
    for ch in chains:
        ch["q"] = ch["q"] + jnp.dot(ch["q"].astype(BF16), ch["n"].astype(BF16),
                                    preferred_element_type=F32)

    for ch in chains:
        uw = jnp.dot(ch["rhs"], ch["q"].astype(BF16), preferred_element_type=F32)
        ch["u_t"] = uw[:GDN_D, :]
        ch["r1"] = jnp.concatenate([uw[GDN_D:, :], ch["qg"]], axis=1).astype(BF16)
        k_dec = ch["k_nat"] * jnp.exp2(ch["g_last"] - ch["g_col"])
        ch["r2"] = jnp.concatenate([k_dec, ch["a_t"]], axis=1).astype(BF16)

    def out_a_piece(c, oa_chunk):
        rows = slice(c * CHUNK, (c + 1) * CHUNK)
        y_ref[rows, :] += lax.dot_general(oa_chunk, wo_ref[0:n_a, :], TN_DIMS,
                                          preferred_element_type=F32)

    states = [st_ref[h] for h in range(GDN_HEADS)]
    ready = None
    for c in range(n_chunks):
        lanes = slice(c * CHUNK, (c + 1) * CHUNK)
        row = chains[c * GDN_HEADS:(c + 1) * GDN_HEADS]
        x1 = [jnp.dot(states[h].astype(BF16), row[h]["r1"], preferred_element_type=F32)
              for h in range(GDN_HEADS)]
        if ready is not None:
            out_a_piece(*ready)
        x2 = [jnp.dot((row[h]["u_t"] - x1[h][:, :CHUNK]).astype(BF16), row[h]["r2"],
                      preferred_element_type=F32) for h in range(GDN_HEADS)]
        oa = []
        for h in range(GDN_HEADS):
            states[h] = states[h] * jnp.exp2(row[h]["g_last"]) + x2[h][:, :GDN_D]
            o_t = x1[h][:, CHUNK:] + x2[h][:, GDN_D:]
            o_n = o_t * lax.rsqrt(jnp.mean(o_t * o_t, axis=0, keepdims=True) + EPS) * gw_ref[...]
            gate = g_ref[nqkv + h * GDN_D:nqkv + (h + 1) * GDN_D, lanes].astype(F32)
            oa.append((o_n * gate).astype(BF16))
        ready = (c, jnp.concatenate(oa, axis=0))
    out_a_piece(*ready)
    for h in range(GDN_HEADS):
        st_ref[h] = states[h]


def _out_b_piece(x_ref, ob_ref, wo_ref, y_ref, n_a, lo):
    cols = slice(lo, lo + OUT_PIECE)
    y_ref[:, cols] = x_ref[:, cols] + jnp.dot(
        ob_ref[...], wo_ref[n_a:, cols], preferred_element_type=F32)


def _gdn_call(gdn_t, ba_t, alog8, dt8, gw, x2, ob, w_out):
    s, d = x2.shape
    tb = GDN_TB
    const = lambda shape: pl.BlockSpec(shape, lambda i: (0,) * len(shape))
    return pl.pallas_call(
        _gdn_kernel,
        grid=(s // tb,),
        in_specs=[
            pl.BlockSpec((gdn_t.shape[0], tb), lambda i: (0, i)),
            pl.BlockSpec((8, tb), lambda i: (0, i)),
            const(alog8.shape),
            const(dt8.shape),
            const(gw.shape),
            pl.BlockSpec((tb, d), lambda i: (i, 0)),
            pl.BlockSpec((tb, ob.shape[1]), lambda i: (i, 0)),
            const(w_out.shape),
        ],
        out_specs=pl.BlockSpec((tb, d), lambda i: (i, 0)),
        out_shape=jax.ShapeDtypeStruct((s, d), F32),
        scratch_shapes=[
            pltpu.VMEM((GDN_HEADS, GDN_D, GDN_D), F32),
        ],
        compiler_params=pltpu.CompilerParams(
            dimension_semantics=("arbitrary",), vmem_limit_bytes=VMEM_LIMIT),
        name="gdn",
    )(gdn_t, ba_t, alog8, dt8, gw, x2, ob, w_out)


def _attn_kernel(lam_ref, q_ref, k_ref, v_ref, z_ref, ks_ref, sw_ref, o_ref,
                 acc_ref, l_ref, m_ref, p_ref, *, lam_init):
    for sub in range(ATT_SUB):
        _attn_tile(sub, lam_ref, q_ref, k_ref, v_ref, z_ref, ks_ref, sw_ref, o_ref,
                   acc_ref.at[sub], l_ref.at[sub], m_ref, p_ref, lam_init)


def _attn_tile(sub, lam_ref, q_ref, k_ref, v_ref, z_ref, ks_ref, sw_ref, o_ref,
               acc_ref, l_ref, m_ref, p_ref, lam_init):
    tq, tk = ATT_TQ, ATT_TK
    dv = 2 * DIFF_D
    cols = slice(sub * tq, (sub + 1) * tq)
    i = pl.program_id(1) * ATT_SUB + sub
    qt = q_ref[:, cols]
    row = lax.broadcasted_iota(jnp.int32, qt.shape, 0)
    zero = jnp.zeros_like(qt)
    q_cat = jnp.concatenate(
        [jnp.where(row < DIFF_D, qt, zero), jnp.where(row >= DIFF_D, qt, zero)], axis=1)

    def scores(j):
        start = pl.multiple_of(j * tk, tk)
        return jnp.dot(k_ref[pl.ds(start, tk), :], q_cat, preferred_element_type=F32)

    def flush(j_pending, alpha):
        start = pl.multiple_of(j_pending * tk, tk)
        vt = v_ref[:, pl.ds(start, tk)]
        for c in range(2):
            upd = acc_ref[c] + jnp.dot(vt, p_ref[:, c * tq:(c + 1) * tq],
                                       preferred_element_type=F32)
            acc_ref[c] = upd if alpha is None else upd * alpha[:, c * tq:(c + 1) * tq]

    def exp_store(s_j, m):
        p = jnp.exp2(s_j if m is None else s_j - m)
        p_ref[...] = p.astype(BF16)
        return jnp.sum(p.reshape(tk // 8, 8, 2 * tq), axis=0)

    def pending(j):
        return jnp.where(j == 0, i, j - 1)

    qf = qt.astype(F32)
    qq = qf * qf
    qsq = jnp.concatenate([jnp.sum(qq[:DIFF_D], axis=0, keepdims=True),
                           jnp.sum(qq[DIFF_D:], axis=0, keepdims=True)], axis=1)
    kmax = jnp.max(ks_ref[0], axis=1, keepdims=True)
    kmax = jnp.concatenate([jnp.broadcast_to(kmax[0:1], (1, tq)),
                            jnp.broadcast_to(kmax[1:2], (1, tq))], axis=1)
    frozen_ok = jnp.max(jnp.sqrt(qsq * kmax)) * NORM_SLACK <= FROZEN_MAX_BOUND

    tri = (lax.broadcasted_iota(jnp.int32, (LANES, LANES), 0)
           <= lax.broadcasted_iota(jnp.int32, (LANES, LANES), 1))
    def cat(parts):
        parts = [t for t in parts if t.shape[1] > 0]
        return parts[0] if len(parts) == 1 else jnp.concatenate(parts, axis=1)

    def widen(t, lo, w, fill):
        pad = jnp.full((t.shape[0], lo), fill, t.dtype)
        return cat([pad, t[:, :w], pad, t[:, w:]])

    strips = []
    m0 = jnp.full((1, 2 * tq), NEG_BIG, F32)
    for a in range(tk // LANES):
        lo, w = a * LANES, tq - a * LANES
        start = pl.multiple_of(i * tk + lo, LANES)
        s_a = jnp.dot(k_ref[pl.ds(start, LANES), :], cat([q_cat[:, lo:tq], q_cat[:, tq + lo:]]),
                      preferred_element_type=F32)
        s_a = cat([jnp.where(tri, s_a[:, :LANES], NEG_BIG), s_a[:, LANES:w],
                   jnp.where(tri, s_a[:, w:w + LANES], NEG_BIG), s_a[:, w + LANES:]])
        strips.append(s_a)
        m0 = jnp.maximum(m0, widen(jnp.max(s_a, axis=0, keepdims=True), lo, w, NEG_BIG))
    m0 = jnp.where(frozen_ok, 0.0, m0)
    m_ref[...] = m0
    l0 = jnp.zeros((8, 2 * tq), F32)
    for a, s_a in enumerate(strips):
        lo, w = a * LANES, tq - a * LANES
        p = jnp.exp2(s_a - cat([m0[:, lo:tq], m0[:, tq + lo:]]))
        p_ref[lo:lo + LANES, :] = widen(p.astype(BF16), lo, w, 0.0)
        l0 = l0 + widen(jnp.sum(p.reshape(LANES // 8, 8, 2 * w), axis=0), lo, w, 0.0)
    l_ref[...] = l0
    acc_ref[...] = jnp.zeros_like(acc_ref)

    def frozen_step(j, j_pending):
        s_j = scores(j)
        flush(j_pending, None)
        l_ref[...] += exp_store(s_j, None)

    def frozen_steps(j0, n):
        for u in range(n):
            frozen_step(j0 + u, pending(j0) if u == 0 else j0 + u - 1)

    def frozen_group(t, carry):
        frozen_steps(ATT_UNROLL * t, ATT_UNROLL)
        return carry

    def online_body(j, carry):
        s_j = scores(j)
        m_old = m_ref[...]
        m_new = jnp.maximum(m_old, jnp.max(s_j, axis=0, keepdims=True))
        alpha = jnp.exp2(m_old - m_new)
        flush(pending(j), alpha)
        m_ref[...] = m_new
        l_ref[...] = l_ref[...] * alpha + exp_store(s_j, m_new)
        return carry

    @pl.when(frozen_ok)
    def _():
        groups = lax.shift_right_logical(i, ATT_UNROLL.bit_length() - 1)
        lax.fori_loop(0, groups, frozen_group, 0)
        j0 = groups * ATT_UNROLL
        n = ATT_UNROLL // 2
        while n >= 1:
            @pl.when((i & n) != 0)
            def _(j0=j0, n=n):
                frozen_steps(j0, n)
            j0 = j0 + (i & n)
            n //= 2

    @pl.when(jnp.logical_not(frozen_ok))
    def _():
        lax.fori_loop(0, i, online_body, 0)

    flush(pending(i), None)

    lam = lam_ref[0, 0]
    l = jnp.sum(l_ref[...], axis=0, keepdims=True)
    o_t = acc_ref[0] / l[:, :tq] - lam * (acc_ref[1] / l[:, tq:])
    o_n = o_t * lax.rsqrt(jnp.mean(o_t * o_t, axis=0, keepdims=True) + SUBLN_EPS) * sw_ref[...]
    o_n = o_n * (1.0 - lam_init)
    o_ref[cols, :] = (o_n * z_ref[:, cols].astype(F32)).T.astype(o_ref.dtype)


def _attn_call(lam, q_t, k_nat, v_t, z_t, ksq, sw, lam_init):
    dv = 2 * DIFF_D
    s = k_nat.shape[0]
    h = k_nat.shape[1] // dv
    tq = ATT_TQ
    tg = ATT_SUB * tq
    assert ATT_TQ == ATT_TK
    return pl.pallas_call(
        functools.partial(_attn_kernel, lam_init=lam_init),
        grid=(h, s // tg),
        in_specs=[
            pl.BlockSpec(memory_space=pltpu.SMEM),
            pl.BlockSpec((dv, tg), lambda hh, i: (hh, i)),
            pl.BlockSpec((s, dv), lambda hh, i: (0, hh)),
            pl.BlockSpec((dv, s), lambda hh, i: (hh, 0)),
            pl.BlockSpec((dv, tg), lambda hh, i: (hh, i)),
            pl.BlockSpec((1, 2, s), lambda hh, i: (hh, 0, 0)),
            pl.BlockSpec((dv, tq), lambda hh, i: (0, 0)),
        ],
        out_specs=pl.BlockSpec((tg, dv), lambda hh, i: (i, hh)),
        out_shape=jax.ShapeDtypeStruct((s, h * dv), BF16),
        scratch_shapes=[
            pltpu.VMEM((ATT_SUB, 2, dv, tq), F32),
            pltpu.VMEM((ATT_SUB, 8, 2 * tq), F32),
            pltpu.VMEM((1, 2 * tq), F32),
            pltpu.VMEM((ATT_TK, 2 * tq), BF16),
        ],
        compiler_params=pltpu.CompilerParams(
            dimension_semantics=("arbitrary", "arbitrary"), vmem_limit_bytes=VMEM_LIMIT),
        name="attn",
    )(lam, q_t, k_nat, v_t, z_t, ksq, sw)


def _layer(l, x2, rope, w_norm, w_in, conv_w, a_log, dt_bias, gdn_norm_w, q_norm_w,
           k_norm_w, lambda_q1, lambda_k1, lambda_q2, lambda_k2, subln_w, w_out):
    s, d = x2.shape
    nqk = GDN_HEADS * GDN_D
    ba_lo = 4 * nqk
    ba_hi = ba_lo + 2 * GDN_HEADS
    wt_a = w_in[:, :ba_lo].T.astype(BF16)
    wt_b = w_in[:, ba_hi:].T.astype(BF16)
    wba = w_in[:, ba_lo:ba_hi].T.astype(BF16)
    qw = jnp.broadcast_to(q_norm_w[:, None], (DIFF_D, PROJ_TM)).astype(F32)
    kw = jnp.broadcast_to(k_norm_w[:, None], (DIFF_D, PROJ_TM)).astype(F32)

    cw = jnp.broadcast_to(conv_w.astype(F32)[:, :, None], (CONV_K, 3 * nqk, LANES))
    gdn_t, ba_t, qb_t, kb, ksq, vb_t, zb_t = _proj_call(
        x2, w_norm[None, :].astype(F32), wt_a, wt_b, wba, rope, qw, kw, cw)

    zeros4 = jnp.zeros((GDN_HEADS, LANES), F32)
    alog8 = jnp.concatenate([zeros4, jnp.broadcast_to(a_log.astype(F32)[:, None], (GDN_HEADS, LANES))], 0)
    dt8 = jnp.concatenate([zeros4, jnp.broadcast_to(dt_bias.astype(F32)[:, None], (GDN_HEADS, LANES))], 0)
    gw = jnp.broadcast_to(gdn_norm_w.astype(F32)[:, None], (GDN_D, LANES))

    lam_init = 0.8 - 0.6 * math.exp(-0.3 * l)
    lam = (jnp.exp(jnp.sum(lambda_q1.astype(F32) * lambda_k1.astype(F32)))
           - jnp.exp(jnp.sum(lambda_q2.astype(F32) * lambda_k2.astype(F32))) + lam_init)
    dv = 2 * DIFF_D
    sw = jnp.broadcast_to(subln_w.astype(F32)[:, None], (dv, ATT_TQ))
    ob = _attn_call(lam.reshape(1, 1).astype(F32), qb_t, kb, vb_t, zb_t, ksq, sw, lam_init)

    return _gdn_call(gdn_t, ba_t, alog8, dt8, gw, x2, ob, w_out.astype(BF16))


def kernel(x, w_norm, w_in, conv_w, a_log, dt_bias, gdn_norm_w, q_norm_w, k_norm_w,
           lambda_q1, lambda_k1, lambda_q2, lambda_k2, subln_w, w_out):
    b, s, d = x.shape
    assert b == 1
    inv_freq = ROPE_THETA ** (-jnp.arange(0, DIFF_D, 2, dtype=jnp.float32) / DIFF_D)
    per_tile = PROJ_TM // LANES
    coarse = jnp.arange(0, s, LANES, dtype=jnp.float32)[:, None] * inv_freq[None, :]
    coarse = coarse.reshape(s // PROJ_TM, per_tile, DIFF_D // 2).transpose(0, 2, 1)
    fine = inv_freq[:, None] * jnp.arange(LANES, dtype=jnp.float32)[None, :]
    rope = (jnp.cos(coarse), jnp.sin(coarse), jnp.cos(fine), jnp.sin(fine))
    x2 = x[0]
    for l in range(w_norm.shape[0]):
        x2 = _layer(l, x2, rope, w_norm[l], w_in[l], conv_w[l], a_log[l], dt_bias[l],
                    gdn_norm_w[l], q_norm_w[l], k_norm_w[l], lambda_q1[l], lambda_k1[l],
                    lambda_q2[l], lambda_k2[l], subln_w[l], w_out[l])
    return x2[None]
```

```python
import functools
import math

import jax
import jax.numpy as jnp
from jax import lax
from jax.experimental import pallas as pl
from jax.experimental.pallas import tpu as pltpu

F32 = jnp.float32
BF16 = jnp.bfloat16

GDN_HEADS = 4
GDN_D = 128
CONV_K = 4
DIFF_HEADS = 4
DIFF_D = 64
ROPE_THETA = 10000.0
EPS = 1e-6
SUBLN_EPS = 1e-5

HEAD_GROUP_ROWS = GDN_HEADS * GDN_D
assert HEAD_GROUP_ROWS == DIFF_HEADS * 2 * DIFF_D

LANES = 128
CHUNK = LANES
NEG_BIG = -1e30
LOG2E = 1.4426950408889634
FROZEN_MAX_BOUND = 40.0
NORM_SLACK = 1.01

PROJ_TM = 1024
GDN_TB = 1024
ATT_TQ = 512
ATT_TK = 512
ATT_UNROLL = 8
ATT_SUB = 2
OUT_PIECE = 256
VMEM_LIMIT = 56 * 1024 * 1024

NT_DIMS = (((1,), (1,)), ((), ()))
TN_DIMS = (((0,), (0,)), ((), ()))


def _sigmoid(v):
    return 1.0 / (1.0 + jnp.exp(-v))


def _silu(v):
    return v * _sigmoid(v)


def _proj_kernel(x_ref, wn_ref, wa_ref, wb_ref, wba_ref, ca_ref, sa_ref, cb_ref, sb_ref,
                 qw_ref, kw_ref, cw_ref,
                 gdn_ref, ba_ref, qb_ref, kb_ref, ks_ref, vb_ref, zb_ref, halo_ref):
    @pl.when(pl.program_id(0) == 0)
    def _():
        halo_ref[...] = jnp.zeros_like(halo_ref)

    x = x_ref[...]
    ms = jnp.mean(x * x, axis=-1, keepdims=True)
    hn = (x * lax.rsqrt(ms + EPS) * wn_ref[...]).astype(BF16)

    blk = HEAD_GROUP_ROWS
    n_group_a = wa_ref.shape[0]

    def proj_t(lo, n):
        w = wa_ref[lo:lo + n, :] if lo < n_group_a else wb_ref[lo - n_group_a:lo - n_group_a + n, :]
        return lax.dot_general(w, hn, NT_DIMS, preferred_element_type=F32)

    tm = x.shape[0]
    lane_t = lax.broadcasted_iota(jnp.int32, (blk, LANES), 1)

    def conv_block(b):
        rows = slice(b * blk, (b + 1) * blk)
        raw = proj_t(b * blk, blk)
        tail = halo_ref[rows, :]
        halo_ref[rows, :] = raw[:, tm - LANES:]
        y = jnp.concatenate([cw_ref[CONV_K - 1, rows, :]] * (tm // LANES), axis=1) * raw
        for j in range(1, CONV_K):
            rolled = pltpu.roll(raw, j, axis=1)
            first = jnp.where(lane_t >= j, rolled[:, :LANES], pltpu.roll(tail, j, axis=1))
            shifted = jnp.concatenate([first, rolled[:, LANES:]], axis=1)
            y = y + jnp.concatenate([cw_ref[CONV_K - 1 - j, rows, :]] * (tm // LANES), axis=1) * shifted
        gdn_ref[rows, :] = _silu(y).astype(BF16)

    q_raw = proj_t(4 * blk, blk)
    k_raw = proj_t(5 * blk, blk)

    ca, sa, cb, sb = ca_ref[0], sa_ref[0], cb_ref[...], sb_ref[...]
    cos = jnp.concatenate([ca[:, c:c + 1] * cb - sa[:, c:c + 1] * sb for c in range(tm // LANES)], axis=1)
    sin = jnp.concatenate([sa[:, c:c + 1] * cb + ca[:, c:c + 1] * sb for c in range(tm // LANES)], axis=1)
    half = DIFF_D // 2

    def norm_rope(t, w, scale):
        outs = []
        for g in range(t.shape[0] // DIFF_D):
            tg = t[g * DIFF_D:(g + 1) * DIFF_D, :]
            tg = tg * lax.rsqrt(jnp.mean(tg * tg, axis=0, keepdims=True) + EPS) * w
            t1, t2 = tg[:half, :], tg[half:, :]
            outs.append((t1 * cos - t2 * sin) * scale)
            outs.append((t2 * cos + t1 * sin) * scale)
        return jnp.concatenate(outs, axis=0)

    qb_ref[...] = norm_rope(q_raw, qw_ref[...], LOG2E / math.sqrt(DIFF_D)).astype(BF16)
    kt = norm_rope(k_raw, kw_ref[...], 1.0)
    for g in range(2 * DIFF_HEADS):
        kg = kt[g * DIFF_D:(g + 1) * DIFF_D, :]
        ks_ref[g // 2, g % 2:g % 2 + 1, :] = jnp.sum(kg * kg, axis=0, keepdims=True)
    kb_ref[...] = kt.T.astype(BF16)

    for b in range(3):
        conv_block(b)
    gdn_ref[3 * blk:4 * blk, :] = _silu(proj_t(3 * blk, blk)).astype(BF16)
    zb_ref[...] = _silu(proj_t(7 * blk, blk)).astype(BF16)
    ba_ref[...] = lax.dot_general(wba_ref[...], hn, NT_DIMS, preferred_element_type=F32)
    vb_ref[...] = proj_t(6 * blk, blk).astype(BF16)


def _proj_call(x2, w_norm, wt_a, wt_b, wba, rope, qw, kw, cw):
    s, d = x2.shape
    tm = PROJ_TM
    blk = HEAD_GROUP_ROWS
    n_gates = wba.shape[0]
    cos_a, sin_a, cos_b, sin_b = rope
    const = lambda shape: pl.BlockSpec(shape, lambda i: (0,) * len(shape))
    cols = lambda rows: pl.BlockSpec((rows, tm), lambda i: (0, i))
    coarse = pl.BlockSpec((1,) + cos_a.shape[1:], lambda i: (i, 0, 0))
    return pl.pallas_call(
        _proj_kernel,
        grid=(s // tm,),
        in_specs=[
            pl.BlockSpec((tm, d), lambda i: (i, 0)),
            const((1, d)),
            const(wt_a.shape),
            const(wt_b.shape),
            const(wba.shape),
            coarse,
            coarse,
            const(cos_b.shape),
            const(sin_b.shape),
            const((DIFF_D, tm)),
            const((DIFF_D, tm)),
            const(cw.shape),
        ],
        out_specs=[
            cols(4 * blk),
            cols(n_gates),
            cols(blk),
            pl.BlockSpec((tm, blk), lambda i: (i, 0)),
            pl.BlockSpec((DIFF_HEADS, 2, tm), lambda i: (0, 0, i)),
            cols(blk),
            cols(blk),
        ],
        out_shape=[
            jax.ShapeDtypeStruct((4 * blk, s), BF16),
            jax.ShapeDtypeStruct((n_gates, s), F32),
            jax.ShapeDtypeStruct((blk, s), BF16),
            jax.ShapeDtypeStruct((s, blk), BF16),
            jax.ShapeDtypeStruct((DIFF_HEADS, 2, s), F32),
            jax.ShapeDtypeStruct((blk, s), BF16),
            jax.ShapeDtypeStruct((blk, s), BF16),
        ],
        scratch_shapes=[
            pltpu.VMEM((cw.shape[1], LANES), F32),
        ],
        compiler_params=pltpu.CompilerParams(
            dimension_semantics=("arbitrary",), vmem_limit_bytes=VMEM_LIMIT),
        name="proj",
    )(x2, w_norm, wt_a, wt_b, wba, cos_a, sin_a, cos_b, sin_b, qw, kw, cw)


def _gdn_kernel(g_ref, ba_ref, alog_ref, dt_ref, gw_ref, x_ref, ob_ref, wo_ref,
                y_ref, st_ref):
    nqkv = 3 * GDN_HEADS * GDN_D
    n_chunks = g_ref.shape[1] // CHUNK
    n_a = GDN_HEADS * GDN_D

    @pl.when(pl.program_id(0) == 0)
    def _():
        st_ref[...] = jnp.zeros_like(st_ref)

    for lo in range(0, y_ref.shape[1], OUT_PIECE):
        _out_b_piece(x_ref, ob_ref, wo_ref, y_ref, n_a, lo)

    lane = lax.broadcasted_iota(jnp.int32, (CHUNK, CHUNK), 1)
    subl = lax.broadcasted_iota(jnp.int32, (CHUNK, CHUNK), 0)
    eye = (lane == subl).astype(F32)
    lane8 = lax.broadcasted_iota(jnp.int32, (8, CHUNK), 1)

    chains = []
    for c in range(n_chunks):
        lanes = slice(c * CHUNK, (c + 1) * CHUNK)
        y = g_ref[0:nqkv, lanes].astype(F32)

        ba = ba_ref[:, lanes]
        beta8 = _sigmoid(ba)
        sp = ba + dt_ref[...]
        softplus = jnp.maximum(sp, 0.0) + jnp.log(1.0 + jnp.exp(-jnp.abs(sp)))
        gc8 = (-LOG2E) * jnp.exp(alog_ref[...]) * softplus
        sh = 1
        while sh < CHUNK:
            gc8 = gc8 + jnp.where(lane8 >= sh, pltpu.roll(gc8, sh, axis=1), 0.0)
            sh *= 2

        for h in range(GDN_HEADS):
            qt = y[h * GDN_D:(h + 1) * GDN_D, :]
            kt = y[(GDN_HEADS + h) * GDN_D:(GDN_HEADS + h + 1) * GDN_D, :]
            vt = y[(2 * GDN_HEADS + h) * GDN_D:(2 * GDN_HEADS + h + 1) * GDN_D, :]
            qt = qt * lax.rsqrt(jnp.sum(qt * qt, axis=0, keepdims=True) + EPS) * (GDN_D ** -0.5)
            kt = kt * lax.rsqrt(jnp.sum(kt * kt, axis=0, keepdims=True) + EPS)
            beta = beta8[h:h + 1, :]
            gc = gc8[GDN_HEADS + h:GDN_HEADS + h + 1, :]
            g_row = jnp.broadcast_to(gc, (CHUNK, CHUNK))
            g_col = g_row.T
            g_last = g_col[CHUNK - 1:CHUNK, :]
            e_gc = jnp.exp2(gc)
            chains.append(dict(
                c=c, h=h, qt=qt, kt=kt, beta=beta, g_last=g_last, g_col=g_col,
                decay_t=jnp.exp2(jnp.where(lane >= subl, g_row - g_col, NEG_BIG)),
                k_nat=kt.T,
                qg=qt * e_gc,
                rhs=jnp.concatenate([vt * beta, kt * (beta * e_gc)], axis=0).astype(BF16)))

    for ch in chains:
        gram = jnp.dot(ch["k_nat"].astype(BF16),
                       jnp.concatenate([ch["kt"], ch["qt"]], axis=1).astype(BF16),
                       preferred_element_type=F32)
        ch["a_t"] = gram[:, CHUNK:] * ch["decay_t"]
        ch["n"] = jnp.where(lane > subl, gram[:, :CHUNK] * ch["decay_t"] * (-ch["beta"]), 0.0)

    for ch in chains:
        nb = ch["n"].astype(BF16)
        ch["q"] = eye + ch["n"]
        ch["n"] = jnp.dot(nb, nb, preferred_element_type=F32)
    m = 2
    while 2 * m < CHUNK:
        for ch in chains:
            nb = ch["n"].astype(BF16)
            both = jnp.dot(jnp.concatenate([ch["q"].astype(BF16), nb], axis=0), nb,
                           preferred_element_type=F32)
            ch["q"] = ch["q"] + both[:CHUNK]
            ch["n"] = both[CHUNK:]
        m *= 2
    for ch in chains:
        ch["q"] = ch["q"] + jnp.dot(ch["q"].astype(BF16), ch["n"].astype(BF16),
                                    preferred_element_type=F32)

    for ch in chains:
        uw = jnp.dot(ch["rhs"], ch["q"].astype(BF16), preferred_element_type=F32)
        ch["u_t"] = uw[:GDN_D, :]
        ch["r1"] = jnp.concatenate([uw[GDN_D:, :], ch["qg"]], axis=1).astype(BF16)
        k_dec = ch["k_nat"] * jnp.exp2(ch["g_last"] - ch["g_col"])
        ch["r2"] = jnp.concatenate([k_dec, ch["a_t"]], axis=1).astype(BF16)

    def out_a_piece(c, oa_chunk):
        rows = slice(c * CHUNK, (c + 1) * CHUNK)
        y_ref[rows, :] += lax.dot_general(oa_chunk, wo_ref[0:n_a, :], TN_DIMS,
                                          preferred_element_type=F32)

    states = [st_ref[h] for h in range(GDN_HEADS)]
    ready = None
    for c in range(n_chunks):
        lanes = slice(c * CHUNK, (c + 1) * CHUNK)
        row = chains[c * GDN_HEADS:(c + 1) * GDN_HEADS]
        x1 = [jnp.dot(states[h].astype(BF16), row[h]["r1"], preferred_element_type=F32)
              for h in range(GDN_HEADS)]
        if ready is not None:
            out_a_piece(*ready)
        x2 = [jnp.dot((row[h]["u_t"] - x1[h][:, :CHUNK]).astype(BF16), row[h]["r2"],
                      preferred_element_type=F32) for h in range(GDN_HEADS)]
        oa = []
        for h in range(GDN_HEADS):
            states[h] = states[h] * jnp.exp2(row[h]["g_last"]) + x2[h][:, :GDN_D]
            o_t = x1[h][:, CHUNK:] + x2[h][:, GDN_D:]
            o_n = o_t * lax.rsqrt(jnp.mean(o_t * o_t, axis=0, keepdims=True) + EPS) * gw_ref[...]
            gate = g_ref[nqkv + h * GDN_D:nqkv + (h + 1) * GDN_D, lanes].astype(F32)
            oa.append((o_n * gate).astype(BF16))
        ready = (c, jnp.concatenate(oa, axis=0))
    out_a_piece(*ready)
    for h in range(GDN_HEADS):
        st_ref[h] = states[h]


def _out_b_piece(x_ref, ob_ref, wo_ref, y_ref, n_a, lo):
    cols = slice(lo, lo + OUT_PIECE)
    y_ref[:, cols] = x_ref[:, cols] + jnp.dot(
        ob_ref[...], wo_ref[n_a:, cols], preferred_element_type=F32)


def _gdn_call(gdn_t, ba_t, alog8, dt8, gw, x2, ob, w_out):
    s, d = x2.shape
    tb = GDN_TB
    const = lambda shape: pl.BlockSpec(shape, lambda i: (0,) * len(shape))
    return pl.pallas_call(
        _gdn_kernel,
        grid=(s // tb,),
        in_specs=[
            pl.BlockSpec((gdn_t.shape[0], tb), lambda i: (0, i)),
            pl.BlockSpec((8, tb), lambda i: (0, i)),
            const(alog8.shape),
            const(dt8.shape),
            const(gw.shape),
            pl.BlockSpec((tb, d), lambda i: (i, 0)),
            pl.BlockSpec((tb, ob.shape[1]), lambda i: (i, 0)),
            const(w_out.shape),
        ],
        out_specs=pl.BlockSpec((tb, d), lambda i: (i, 0)),
        out_shape=jax.ShapeDtypeStruct((s, d), F32),
        scratch_shapes=[
            pltpu.VMEM((GDN_HEADS, GDN_D, GDN_D), F32),
        ],
        compiler_params=pltpu.CompilerParams(
            dimension_semantics=("arbitrary",), vmem_limit_bytes=VMEM_LIMIT),
        name="gdn",
    )(gdn_t, ba_t, alog8, dt8, gw, x2, ob, w_out)


def _attn_kernel(lam_ref, q_ref, k_ref, v_ref, z_ref, ks_ref, sw_ref, o_ref,
                 acc_ref, l_ref, m_ref, p_ref, *, lam_init):
    for sub in range(ATT_SUB):
        _attn_tile(sub, lam_ref, q_ref, k_ref, v_ref, z_ref, ks_ref, sw_ref, o_ref,
                   acc_ref.at[sub], l_ref.at[sub], m_ref, p_ref, lam_init)


def _attn_tile(sub, lam_ref, q_ref, k_ref, v_ref, z_ref, ks_ref, sw_ref, o_ref,
               acc_ref, l_ref, m_ref, p_ref, lam_init):
    tq, tk = ATT_TQ, ATT_TK
    dv = 2 * DIFF_D
    cols = slice(sub * tq, (sub + 1) * tq)
    i = pl.program_id(1) * ATT_SUB + sub
    qt = q_ref[:, cols]
    row = lax.broadcasted_iota(jnp.int32, qt.shape, 0)
    zero = jnp.zeros_like(qt)
    q_cat = jnp.concatenate(
        [jnp.where(row < DIFF_D, qt, zero), jnp.where(row >= DIFF_D, qt, zero)], axis=1)

    def scores(j):
        start = pl.multiple_of(j * tk, tk)
        return jnp.dot(k_ref[pl.ds(start, tk), :], q_cat, preferred_element_type=F32)

    def flush(j_pending, alpha):
        start = pl.multiple_of(j_pending * tk, tk)
        vt = v_ref[:, pl.ds(start, tk)]
        for c in range(2):
            upd = acc_ref[c] + jnp.dot(vt, p_ref[:, c * tq:(c + 1) * tq],
                                       preferred_element_type=F32)
            acc_ref[c] = upd if alpha is None else upd * alpha[:, c * tq:(c + 1) * tq]

    def exp_store(s_j, m):
        p = jnp.exp2(s_j if m is None else s_j - m)
        p_ref[...] = p.astype(BF16)
        return jnp.sum(p.reshape(tk // 8, 8, 2 * tq), axis=0)

    def pending(j):
        return jnp.where(j == 0, i, j - 1)

    qf = qt.astype(F32)
    qq = qf * qf
    qsq = jnp.concatenate([jnp.sum(qq[:DIFF_D], axis=0, keepdims=True),
                           jnp.sum(qq[DIFF_D:], axis=0, keepdims=True)], axis=1)
    kmax = jnp.max(ks_ref[0], axis=1, keepdims=True)
    kmax = jnp.concatenate([jnp.broadcast_to(kmax[0:1], (1, tq)),
                            jnp.broadcast_to(kmax[1:2], (1, tq))], axis=1)
    frozen_ok = jnp.max(jnp.sqrt(qsq * kmax)) * NORM_SLACK <= FROZEN_MAX_BOUND

    tri = (lax.broadcasted_iota(jnp.int32, (LANES, LANES), 0)
           <= lax.broadcasted_iota(jnp.int32, (LANES, LANES), 1))
    def cat(parts):
        parts = [t for t in parts if t.shape[1] > 0]
        return parts[0] if len(parts) == 1 else jnp.concatenate(parts, axis=1)

    def widen(t, lo, w, fill):
        pad = jnp.full((t.shape[0], lo), fill, t.dtype)
        return cat([pad, t[:, :w], pad, t[:, w:]])

    strips = []
    m0 = jnp.full((1, 2 * tq), NEG_BIG, F32)
    for a in range(tk // LANES):
        lo, w = a * LANES, tq - a * LANES
        start = pl.multiple_of(i * tk + lo, LANES)
        s_a = jnp.dot(k_ref[pl.ds(start, LANES), :], cat([q_cat[:, lo:tq], q_cat[:, tq + lo:]]),
                      preferred_element_type=F32)
        s_a = cat([jnp.where(tri, s_a[:, :LANES], NEG_BIG), s_a[:, LANES:w],
                   jnp.where(tri, s_a[:, w:w + LANES], NEG_BIG), s_a[:, w + LANES:]])
        strips.append(s_a)
        m0 = jnp.maximum(m0, widen(jnp.max(s_a, axis=0, keepdims=True), lo, w, NEG_BIG))
    m0 = jnp.where(frozen_ok, 0.0, m0)
    m_ref[...] = m0
    l0 = jnp.zeros((8, 2 * tq), F32)
    for a, s_a in enumerate(strips):
        lo, w = a * LANES, tq - a * LANES
        p = jnp.exp2(s_a - cat([m0[:, lo:tq], m0[:, tq + lo:]]))
        p_ref[lo:lo + LANES, :] = widen(p.astype(BF16), lo, w, 0.0)
        l0 = l0 + widen(jnp.sum(p.reshape(LANES // 8, 8, 2 * w), axis=0), lo, w, 0.0)
    l_ref[...] = l0
    acc_ref[...] = jnp.zeros_like(acc_ref)

    def frozen_step(j, j_pending):
        s_j = scores(j)
        flush(j_pending, None)
        l_ref[...] += exp_store(s_j, None)

    def frozen_steps(j0, n):
        for u in range(n):
            frozen_step(j0 + u, pending(j0) if u == 0 else j0 + u - 1)

    def frozen_group(t, carry):
        frozen_steps(ATT_UNROLL * t, ATT_UNROLL)
        return carry

    def online_body(j, carry):
        s_j = scores(j)
        m_old = m_ref[...]
        m_new = jnp.maximum(m_old, jnp.max(s_j, axis=0, keepdims=True))
        alpha = jnp.exp2(m_old - m_new)
        flush(pending(j), alpha)
        m_ref[...] = m_new
        l_ref[...] = l_ref[...] * alpha + exp_store(s_j, m_new)
        return carry

    @pl.when(frozen_ok)
    def _():
        groups = lax.shift_right_logical(i, ATT_UNROLL.bit_length() - 1)
        lax.fori_loop(0, groups, frozen_group, 0)
        j0 = groups * ATT_UNROLL
        n = ATT_UNROLL // 2
        while n >= 1:
            @pl.when((i & n) != 0)
            def _(j0=j0, n=n):
                frozen_steps(j0, n)
            j0 = j0 + (i & n)
            n //= 2

    @pl.when(jnp.logical_not(frozen_ok))
    def _():
        lax.fori_loop(0, i, online_body, 0)

    flush(pending(i), None)

    lam = lam_ref[0, 0]
    l = jnp.sum(l_ref[...], axis=0, keepdims=True)
    o_t = acc_ref[0] / l[:, :tq] - lam * (acc_ref[1] / l[:, tq:])
    o_n = o_t * lax.rsqrt(jnp.mean(o_t * o_t, axis=0, keepdims=True) + SUBLN_EPS) * sw_ref[...]
    o_n = o_n * (1.0 - lam_init)
    o_ref[cols, :] = (o_n * z_ref[:, cols].astype(F32)).T.astype(o_ref.dtype)


def _attn_call(lam, q_t, k_nat, v_t, z_t, ksq, sw, lam_init):
    dv = 2 * DIFF_D
    s = k_nat.shape[0]
    h = k_nat.shape[1] // dv
    tq = ATT_TQ
    tg = ATT_SUB * tq
    assert ATT_TQ == ATT_TK
    return pl.pallas_call(
        functools.partial(_attn_kernel, lam_init=lam_init),
        grid=(h, s // tg),
        in_specs=[
            pl.BlockSpec(memory_space=pltpu.SMEM),
            pl.BlockSpec((dv, tg), lambda hh, i: (hh, i)),
            pl.BlockSpec((s, dv), lambda hh, i: (0, hh)),
            pl.BlockSpec((dv, s), lambda hh, i: (hh, 0)),
            pl.BlockSpec((dv, tg), lambda hh, i: (hh, i)),
            pl.BlockSpec((1, 2, s), lambda hh, i: (hh, 0, 0)),
            pl.BlockSpec((dv, tq), lambda hh, i: (0, 0)),
        ],
        out_specs=pl.BlockSpec((tg, dv), lambda hh, i: (i, hh)),
        out_shape=jax.ShapeDtypeStruct((s, h * dv), BF16),
        scratch_shapes=[
            pltpu.VMEM((ATT_SUB, 2, dv, tq), F32),
            pltpu.VMEM((ATT_SUB, 8, 2 * tq), F32),
            pltpu.VMEM((1, 2 * tq), F32),
            pltpu.VMEM((ATT_TK, 2 * tq), BF16),
        ],
        compiler_params=pltpu.CompilerParams(
            dimension_semantics=("arbitrary", "arbitrary"), vmem_limit_bytes=VMEM_LIMIT),
        name="attn",
    )(lam, q_t, k_nat, v_t, z_t, ksq, sw)


def _layer(l, x2, rope, w_norm, w_in, conv_w, a_log, dt_bias, gdn_norm_w, q_norm_w,
           k_norm_w, lambda_q1, lambda_k1, lambda_q2, lambda_k2, subln_w, w_out):
    s, d = x2.shape
    nqk = GDN_HEADS * GDN_D
    ba_lo = 4 * nqk
    ba_hi = ba_lo + 2 * GDN_HEADS
    wt_a = w_in[:, :ba_lo].T.astype(BF16)
    wt_b = w_in[:, ba_hi:].T.astype(BF16)
    wba = w_in[:, ba_lo:ba_hi].T.astype(BF16)
    qw = jnp.broadcast_to(q_norm_w[:, None], (DIFF_D, PROJ_TM)).astype(F32)
    kw = jnp.broadcast_to(k_norm_w[:, None], (DIFF_D, PROJ_TM)).astype(F32)

    cw = jnp.broadcast_to(conv_w.astype(F32)[:, :, None], (CONV_K, 3 * nqk, LANES))
    gdn_t, ba_t, qb_t, kb, ksq, vb_t, zb_t = _proj_call(
        x2, w_norm[None, :].astype(F32), wt_a, wt_b, wba, rope, qw, kw, cw)

    zeros4 = jnp.zeros((GDN_HEADS, LANES), F32)
    alog8 = jnp.concatenate([zeros4, jnp.broadcast_to(a_log.astype(F32)[:, None], (GDN_HEADS, LANES))], 0)
    dt8 = jnp.concatenate([zeros4, jnp.broadcast_to(dt_bias.astype(F32)[:, None], (GDN_HEADS, LANES))], 0)
    gw = jnp.broadcast_to(gdn_norm_w.astype(F32)[:, None], (GDN_D, LANES))

    lam_init = 0.8 - 0.6 * math.exp(-0.3 * l)
    lam = (jnp.exp(jnp.sum(lambda_q1.astype(F32) * lambda_k1.astype(F32)))
           - jnp.exp(jnp.sum(lambda_q2.astype(F32) * lambda_k2.astype(F32))) + lam_init)
    dv = 2 * DIFF_D
    sw = jnp.broadcast_to(subln_w.astype(F32)[:, None], (dv, ATT_TQ))
    ob = _attn_call(lam.reshape(1, 1).astype(F32), qb_t, kb, vb_t, zb_t, ksq, sw, lam_init)

    return _gdn_call(gdn_t, ba_t, alog8, dt8, gw, x2, ob, w_out.astype(BF16))


def kernel(x, w_norm, w_in, conv_w, a_log, dt_bias, gdn_norm_w, q_norm_w, k_norm_w,
           lambda_q1, lambda_k1, lambda_q2, lambda_k2, subln_w, w_out):
    b, s, d = x.shape
    assert b == 1
    inv_freq = ROPE_THETA ** (-jnp.arange(0, DIFF_D, 2, dtype=jnp.float32) / DIFF_D)
    per_tile = PROJ_TM // LANES
    coarse = jnp.arange(0, s, LANES, dtype=jnp.float32)[:, None] * inv_freq[None, :]
    coarse = coarse.reshape(s // PROJ_TM, per_tile, DIFF_D // 2).transpose(0, 2, 1)
    fine = inv_freq[:, None] * jnp.arange(LANES, dtype=jnp.float32)[None, :]
    rope = (jnp.cos(coarse), jnp.sin(coarse), jnp.cos(fine), jnp.sin(fine))
    x2 = x[0]
    for l in range(w_norm.shape[0]):
        x2 = _layer(l, x2, rope, w_norm[l], w_in[l], conv_w[l], a_log[l], dt_bias[l],
                    gdn_norm_w[l], q_norm_w[l], k_norm_w[l], lambda_q1[l], lambda_k1[l],
                    lambda_q2[l], lambda_k2[l], subln_w[l], w_out[l])
    return x2[None]
```

```python
import functools
import math

import jax
import jax.numpy as jnp
from jax import lax
from jax.experimental import pallas as pl
from jax.experimental.pallas import tpu as pltpu

F32 = jnp.float32
BF16 = jnp.bfloat16

GDN_HEADS = 4
GDN_D = 128
CONV_K = 4
DIFF_HEADS = 4
DIFF_D = 64
ROPE_THETA = 10000.0
EPS = 1e-6
SUBLN_EPS = 1e-5

HEAD_GROUP_ROWS = GDN_HEADS * GDN_D
assert HEAD_GROUP_ROWS == DIFF_HEADS * 2 * DIFF_D

LANES = 128
CHUNK = LANES
NEG_BIG = -1e30
LOG2E = 1.4426950408889634
FROZEN_MAX_BOUND = 40.0
NORM_SLACK = 1.01

PROJ_TM = 1024
GDN_TB = 512
ATT_TQ = 512
ATT_TK = 512
ATT_UNROLL = 8
ATT_STEPS_PER_HEAD = 2
OUT_PIECE = 256
VMEM_LIMIT = 56 * 1024 * 1024

NT_DIMS = (((1,), (1,)), ((), ()))
TN_DIMS = (((0,), (0,)), ((), ()))


def _sigmoid(v):
    return 1.0 / (1.0 + jnp.exp(-v))


def _silu(v):
    return v * _sigmoid(v)


def _proj_kernel(x_ref, wn_ref, wa_ref, wb_ref, wba_ref, ca_ref, sa_ref, cb_ref, sb_ref,
                 qw_ref, kw_ref, cw_ref,
                 gdn_ref, ba_ref, qb_ref, kb_ref, ks_ref, vb_ref, zb_ref, halo_ref):
    @pl.when(pl.program_id(0) == 0)
    def _():
        halo_ref[...] = jnp.zeros_like(halo_ref)

    x = x_ref[...]
    ms = jnp.mean(x * x, axis=-1, keepdims=True)
    hn = (x * lax.rsqrt(ms + EPS) * wn_ref[...]).astype(BF16)

    blk = HEAD_GROUP_ROWS
    n_group_a = wa_ref.shape[0]

    def proj_t(lo, n):
        w = wa_ref[lo:lo + n, :] if lo < n_group_a else wb_ref[lo - n_group_a:lo - n_group_a + n, :]
        return lax.dot_general(w, hn, NT_DIMS, preferred_element_type=F32)

    tm = x.shape[0]
    lane_t = lax.broadcasted_iota(jnp.int32, (blk, LANES), 1)

    def conv_block(b):
        rows = slice(b * blk, (b + 1) * blk)
        raw = proj_t(b * blk, blk)
        tail = halo_ref[rows, :]
        halo_ref[rows, :] = raw[:, tm - LANES:]
        y = jnp.concatenate([cw_ref[CONV_K - 1, rows, :]] * (tm // LANES), axis=1) * raw
        for j in range(1, CONV_K):
            rolled = pltpu.roll(raw, j, axis=1)
            first = jnp.where(lane_t >= j, rolled[:, :LANES], pltpu.roll(tail, j, axis=1))
            shifted = jnp.concatenate([first, rolled[:, LANES:]], axis=1)
            y = y + jnp.concatenate([cw_ref[CONV_K - 1 - j, rows, :]] * (tm // LANES), axis=1) * shifted
        gdn_ref[rows, :] = _silu(y).astype(BF16)

    q_raw = proj_t(4 * blk, blk)
    k_raw = proj_t(5 * blk, blk)

    ca, sa, cb, sb = ca_ref[0], sa_ref[0], cb_ref[...], sb_ref[...]
    cos = jnp.concatenate([ca[:, c:c + 1] * cb - sa[:, c:c + 1] * sb for c in range(tm // LANES)], axis=1)
    sin = jnp.concatenate([sa[:, c:c + 1] * cb + ca[:, c:c + 1] * sb for c in range(tm // LANES)], axis=1)
    half = DIFF_D // 2

    def norm_rope(t, w, scale):
        outs = []
        for g in range(t.shape[0] // DIFF_D):
            tg = t[g * DIFF_D:(g + 1) * DIFF_D, :]
            tg = tg * lax.rsqrt(jnp.mean(tg * tg, axis=0, keepdims=True) + EPS) * w
            t1, t2 = tg[:half, :], tg[half:, :]
            outs.append((t1 * cos - t2 * sin) * scale)
            outs.append((t2 * cos + t1 * sin) * scale)
        return jnp.concatenate(outs, axis=0)

    qb_ref[...] = norm_rope(q_raw, qw_ref[...], LOG2E / math.sqrt(DIFF_D)).astype(BF16)
    kt = norm_rope(k_raw, kw_ref[...], 1.0)
    for g in range(2 * DIFF_HEADS):
        kg = kt[g * DIFF_D:(g + 1) * DIFF_D, :]
        ks_ref[g // 2, g % 2:g % 2 + 1, :] = jnp.sum(kg * kg, axis=0, keepdims=True)
    kb_ref[...] = kt.T.astype(BF16)

    for b in range(3):
        conv_block(b)
    gdn_ref[3 * blk:4 * blk, :] = _silu(proj_t(3 * blk, blk)).astype(BF16)
    zb_ref[...] = _silu(proj_t(7 * blk, blk)).astype(BF16)
    ba_ref[...] = lax.dot_general(wba_ref[...], hn, NT_DIMS, preferred_element_type=F32)
    vb_ref[...] = proj_t(6 * blk, blk).astype(BF16)


def _proj_call(x2, w_norm, wt_a, wt_b, wba, rope, qw, kw, cw):
    s, d = x2.shape
    tm = PROJ_TM
    blk = HEAD_GROUP_ROWS
    n_gates = wba.shape[0]
    cos_a, sin_a, cos_b, sin_b = rope
    const = lambda shape: pl.BlockSpec(shape, lambda i: (0,) * len(shape))
    cols = lambda rows: pl.BlockSpec((rows, tm), lambda i: (0, i))
    coarse = pl.BlockSpec((1,) + cos_a.shape[1:], lambda i: (i, 0, 0))
    return pl.pallas_call(
        _proj_kernel,
        grid=(s // tm,),
        in_specs=[
            pl.BlockSpec((tm, d), lambda i: (i, 0)),
            const((1, d)),
            const(wt_a.shape),
            const(wt_b.shape),
            const(wba.shape),
            coarse,
            coarse,
            const(cos_b.shape),
            const(sin_b.shape),
            const((DIFF_D, tm)),
            const((DIFF_D, tm)),
            const(cw.shape),
        ],
        out_specs=[
            cols(4 * blk),
            cols(n_gates),
            cols(blk),
            pl.BlockSpec((tm, blk), lambda i: (i, 0)),
            pl.BlockSpec((DIFF_HEADS, 2, tm), lambda i: (0, 0, i)),
            cols(blk),
            cols(blk),
        ],
        out_shape=[
            jax.ShapeDtypeStruct((4 * blk, s), BF16),
            jax.ShapeDtypeStruct((n_gates, s), F32),
            jax.ShapeDtypeStruct((blk, s), BF16),
            jax.ShapeDtypeStruct((s, blk), BF16),
            jax.ShapeDtypeStruct((DIFF_HEADS, 2, s), F32),
            jax.ShapeDtypeStruct((blk, s), BF16),
            jax.ShapeDtypeStruct((blk, s), BF16),
        ],
        scratch_shapes=[
            pltpu.VMEM((cw.shape[1], LANES), F32),
        ],
        compiler_params=pltpu.CompilerParams(
            dimension_semantics=("arbitrary",), vmem_limit_bytes=VMEM_LIMIT),
        name="proj",
    )(x2, w_norm, wt_a, wt_b, wba, cos_a, sin_a, cos_b, sin_b, qw, kw, cw)


def _gdn_kernel(g_ref, ba_ref, alog_ref, dt_ref, gw_ref, x_ref, ob_ref, wo_ref,
                y_ref, st_ref):
    nqkv = 3 * GDN_HEADS * GDN_D
    n_chunks = g_ref.shape[1] // CHUNK
    n_a = GDN_HEADS * GDN_D

    @pl.when(pl.program_id(0) == 0)
    def _():
        st_ref[...] = jnp.zeros_like(st_ref)

    for lo in range(0, y_ref.shape[1], OUT_PIECE):
        _out_b_piece(x_ref, ob_ref, wo_ref, y_ref, n_a, lo)

    lane = lax.broadcasted_iota(jnp.int32, (CHUNK, CHUNK), 1)
    subl = lax.broadcasted_iota(jnp.int32, (CHUNK, CHUNK), 0)
    eye = (lane == subl).astype(F32)
    lane8 = lax.broadcasted_iota(jnp.int32, (8, CHUNK), 1)

    chains = []
    for c in range(n_chunks):
        lanes = slice(c * CHUNK, (c + 1) * CHUNK)
        y = g_ref[0:nqkv, lanes].astype(F32)

        ba = ba_ref[:, lanes]
        beta8 = _sigmoid(ba)
        sp = ba + dt_ref[...]
        softplus = jnp.maximum(sp, 0.0) + jnp.log(1.0 + jnp.exp(-jnp.abs(sp)))
        gc8 = (-LOG2E) * jnp.exp(alog_ref[...]) * softplus
        sh = 1
        while sh < CHUNK:
            gc8 = gc8 + jnp.where(lane8 >= sh, pltpu.roll(gc8, sh, axis=1), 0.0)
            sh *= 2

        for h in range(GDN_HEADS):
            qt = y[h * GDN_D:(h + 1) * GDN_D, :]
            kt = y[(GDN_HEADS + h) * GDN_D:(GDN_HEADS + h + 1) * GDN_D, :]
            vt = y[(2 * GDN_HEADS + h) * GDN_D:(2 * GDN_HEADS + h + 1) * GDN_D, :]
            qt = qt * lax.rsqrt(jnp.sum(qt * qt, axis=0, keepdims=True) + EPS) * (GDN_D ** -0.5)
            kt = kt * lax.rsqrt(jnp.sum(kt * kt, axis=0, keepdims=True) + EPS)
            beta = beta8[h:h + 1, :]
            gc = gc8[GDN_HEADS + h:GDN_HEADS + h + 1, :]
            g_row = jnp.broadcast_to(gc, (CHUNK, CHUNK))
            g_col = g_row.T
            g_last = g_col[CHUNK - 1:CHUNK, :]
            e_gc = jnp.exp2(gc)
            chains.append(dict(
                c=c, h=h, qt=qt, kt=kt, beta=beta, g_last=g_last, g_col=g_col,
                decay_t=jnp.exp2(jnp.where(lane >= subl, g_row - g_col, NEG_BIG)),
                k_nat=kt.T,
                qg=qt * e_gc,
                rhs=jnp.concatenate([vt * beta, kt * (beta * e_gc)], axis=0).astype(BF16)))

    for ch in chains:
        gram = jnp.dot(ch["k_nat"].astype(BF16),
                       jnp.concatenate([ch["kt"], ch["qt"]], axis=1).astype(BF16),
                       preferred_element_type=F32)
        ch["a_t"] = gram[:, CHUNK:] * ch["decay_t"]
        ch["n"] = jnp.where(lane > subl, gram[:, :CHUNK] * ch["decay_t"] * (-ch["beta"]), 0.0)

    for ch in chains:
        nb = ch["n"].astype(BF16)
        ch["q"] = eye + ch["n"]
        ch["n"] = jnp.dot(nb, nb, preferred_element_type=F32)
    m = 2
    while 2 * m < CHUNK:
        for ch in chains:
            nb = ch["n"].astype(BF16)
            both = jnp.dot(jnp.concatenate([ch["q"].astype(BF16), nb], axis=0), nb,
                           preferred_element_type=F32)
            ch["q"] = ch["q"] + both[:CHUNK]
            ch["n"] = both[CHUNK:]
        m *= 2
    for ch in chains:
        ch["q"] = ch["q"] + jnp.dot(ch["q"].astype(BF16), ch["n"].astype(BF16),
                                    preferred_element_type=F32)

    for ch in chains:
        uw = jnp.dot(ch["rhs"], ch["q"].astype(BF16), preferred_element_type=F32)
        ch["u_t"] = uw[:GDN_D, :]
        ch["r1"] = jnp.concatenate([uw[GDN_D:, :], ch["qg"]], axis=1).astype(BF16)
        k_dec = ch["k_nat"] * jnp.exp2(ch["g_last"] - ch["g_col"])
        ch["r2"] = jnp.concatenate([k_dec, ch["a_t"]], axis=1).astype(BF16)

    def out_a_piece(c, oa_chunk):
        rows = slice(c * CHUNK, (c + 1) * CHUNK)
        y_ref[rows, :] += lax.dot_general(oa_chunk, wo_ref[0:n_a, :], TN_DIMS,
                                          preferred_element_type=F32)

    states = [st_ref[h] for h in range(GDN_HEADS)]
    ready = None
    for c in range(n_chunks):
        lanes = slice(c * CHUNK, (c + 1) * CHUNK)
        row = chains[c * GDN_HEADS:(c + 1) * GDN_HEADS]
        x1 = [jnp.dot(states[h].astype(BF16), row[h]["r1"], preferred_element_type=F32)
              for h in range(GDN_HEADS)]
        if ready is not None:
            out_a_piece(*ready)
        x2 = [jnp.dot((row[h]["u_t"] - x1[h][:, :CHUNK]).astype(BF16), row[h]["r2"],
                      preferred_element_type=F32) for h in range(GDN_HEADS)]
        oa = []
        for h in range(GDN_HEADS):
            states[h] = states[h] * jnp.exp2(row[h]["g_last"]) + x2[h][:, :GDN_D]
            o_t = x1[h][:, CHUNK:] + x2[h][:, GDN_D:]
            o_n = o_t * lax.rsqrt(jnp.mean(o_t * o_t, axis=0, keepdims=True) + EPS) * gw_ref[...]
            gate = g_ref[nqkv + h * GDN_D:nqkv + (h + 1) * GDN_D, lanes].astype(F32)
            oa.append((o_n * gate).astype(BF16))
        ready = (c, jnp.concatenate(oa, axis=0))
    out_a_piece(*ready)
    for h in range(GDN_HEADS):
        st_ref[h] = states[h]


def _out_b_piece(x_ref, ob_ref, wo_ref, y_ref, n_a, lo):
    cols = slice(lo, lo + OUT_PIECE)
    y_ref[:, cols] = x_ref[:, cols] + jnp.dot(
        ob_ref[...], wo_ref[n_a:, cols], preferred_element_type=F32)


def _gdn_call(gdn_t, ba_t, alog8, dt8, gw, x2, ob, w_out):
    s, d = x2.shape
    tb = GDN_TB
    const = lambda shape: pl.BlockSpec(shape, lambda i: (0,) * len(shape))
    return pl.pallas_call(
        _gdn_kernel,
        grid=(s // tb,),
        in_specs=[
            pl.BlockSpec((gdn_t.shape[0], tb), lambda i: (0, i)),
            pl.BlockSpec((8, tb), lambda i: (0, i)),
            const(alog8.shape),
            const(dt8.shape),
            const(gw.shape),
            pl.BlockSpec((tb, d), lambda i: (i, 0)),
            pl.BlockSpec((tb, ob.shape[1]), lambda i: (i, 0)),
            const(w_out.shape),
        ],
        out_specs=pl.BlockSpec((tb, d), lambda i: (i, 0)),
        out_shape=jax.ShapeDtypeStruct((s, d), F32),
        scratch_shapes=[
            pltpu.VMEM((GDN_HEADS, GDN_D, GDN_D), F32),
        ],
        compiler_params=pltpu.CompilerParams(
            dimension_semantics=("arbitrary",), vmem_limit_bytes=VMEM_LIMIT),
        name="gdn",
    )(gdn_t, ba_t, alog8, dt8, gw, x2, ob, w_out)


def _attn_kernel(lam_ref, q_ref, k_ref, v_ref, z_ref, ks_ref, sw_ref, o_ref,
                 acc_ref, l_ref, m_ref, p_ref, *, lam_init):
    n_t = q_ref.shape[1] // ATT_TQ
    first = pl.program_id(1) * n_t

    @pl.when((pl.program_id(0) == 0) & (pl.program_id(1) == 0))
    def _():
        acc_ref[...] = jnp.zeros_like(acc_ref)
        l_ref[...] = jnp.ones_like(l_ref)
        p_ref[...] = jnp.zeros_like(p_ref)

    def finish(t):
        par = t & 1
        _attn_finish(t, first, lam_ref, v_ref, z_ref, sw_ref, o_ref,
                     acc_ref.at[par], l_ref.at[par], p_ref, lam_init)

    def body(t, carry):
        finish(jnp.maximum(t - 1, 0))
        par = t & 1
        _attn_tile(t, first, q_ref, k_ref, v_ref, ks_ref, acc_ref.at[par], l_ref.at[par], m_ref, p_ref)
        return carry

    lax.fori_loop(0, n_t, body, 0)
    finish(n_t - 1)


def _attn_finish(t, first, lam_ref, v_ref, z_ref, sw_ref, o_ref, acc_ref, l_ref, p_ref, lam_init):
    tq, tk = ATT_TQ, ATT_TK
    i = first + t
    j_pending = jnp.where(i == 0, i, i - 1)
    start = pl.multiple_of(j_pending * tk, tk)
    vt = v_ref[:, pl.ds(start, tk)]
    acc = [acc_ref[c] + jnp.dot(vt, p_ref[:, c * tq:(c + 1) * tq], preferred_element_type=F32)
           for c in range(2)]
    lam = lam_ref[0, 0]
    l = jnp.sum(l_ref[...], axis=0, keepdims=True)
    o_t = acc[0] / l[:, :tq] - lam * (acc[1] / l[:, tq:])
    o_n = o_t * lax.rsqrt(jnp.mean(o_t * o_t, axis=0, keepdims=True) + SUBLN_EPS) * sw_ref[...]
    o_n = o_n * (1.0 - lam_init)
    cols = pl.ds(pl.multiple_of(t * tq, tq), tq)
    o_ref[cols, :] = (o_n * z_ref[:, cols].astype(F32)).T.astype(o_ref.dtype)


def _attn_tile(t, first, q_ref, k_ref, v_ref, ks_ref, acc_ref, l_ref, m_ref, p_ref):
    tq, tk = ATT_TQ, ATT_TK
    i = first + t
    qt = q_ref[:, pl.ds(pl.multiple_of(t * tq, tq), tq)]
    row = lax.broadcasted_iota(jnp.int32, qt.shape, 0)
    zero = jnp.zeros_like(qt)
    q_cat = jnp.concatenate(
        [jnp.where(row < DIFF_D, qt, zero), jnp.where(row >= DIFF_D, qt, zero)], axis=1)

    def scores(j):
        start = pl.multiple_of(j * tk, tk)
        return jnp.dot(k_ref[pl.ds(start, tk), :], q_cat, preferred_element_type=F32)

    def flush(j_pending, alpha):
        start = pl.multiple_of(j_pending * tk, tk)
        vt = v_ref[:, pl.ds(start, tk)]
        for c in range(2):
            upd = acc_ref[c] + jnp.dot(vt, p_ref[:, c * tq:(c + 1) * tq],
                                       preferred_element_type=F32)
            acc_ref[c] = upd if alpha is None else upd * alpha[:, c * tq:(c + 1) * tq]

    def exp_store(s_j, m):
        p = jnp.exp2(s_j if m is None else s_j - m)
        p_ref[...] = p.astype(BF16)
        return jnp.sum(p.reshape(tk // 8, 8, 2 * tq), axis=0)

    def pending(j):
        return jnp.where(j == 0, i, j - 1)

    qf = qt.astype(F32)
    qq = qf * qf
    qsq = jnp.concatenate([jnp.sum(qq[:DIFF_D], axis=0, keepdims=True),
                           jnp.sum(qq[DIFF_D:], axis=0, keepdims=True)], axis=1)
    kmax = jnp.max(ks_ref[0], axis=1, keepdims=True)
    kmax = jnp.concatenate([jnp.broadcast_to(kmax[0:1], (1, tq)),
                            jnp.broadcast_to(kmax[1:2], (1, tq))], axis=1)
    frozen_ok = jnp.max(jnp.sqrt(qsq * kmax)) * NORM_SLACK <= FROZEN_MAX_BOUND

    tri = (lax.broadcasted_iota(jnp.int32, (LANES, LANES), 0)
           <= lax.broadcasted_iota(jnp.int32, (LANES, LANES), 1))
    def cat(parts):
        parts = [t for t in parts if t.shape[1] > 0]
        return parts[0] if len(parts) == 1 else jnp.concatenate(parts, axis=1)

    def widen(t, lo, w, fill):
        pad = jnp.full((t.shape[0], lo), fill, t.dtype)
        return cat([pad, t[:, :w], pad, t[:, w:]])

    strips = []
    m0 = jnp.full((1, 2 * tq), NEG_BIG, F32)
    for a in range(tk // LANES):
        lo, w = a * LANES, tq - a * LANES
        start = pl.multiple_of(i * tk + lo, LANES)
        s_a = jnp.dot(k_ref[pl.ds(start, LANES), :], cat([q_cat[:, lo:tq], q_cat[:, tq + lo:]]),
                      preferred_element_type=F32)
        s_a = cat([jnp.where(tri, s_a[:, :LANES], NEG_BIG), s_a[:, LANES:w],
                   jnp.where(tri, s_a[:, w:w + LANES], NEG_BIG), s_a[:, w + LANES:]])
        strips.append(s_a)
        m0 = jnp.maximum(m0, widen(jnp.max(s_a, axis=0, keepdims=True), lo, w, NEG_BIG))
    m0 = jnp.where(frozen_ok, 0.0, m0)
    m_ref[...] = m0
    l0 = jnp.zeros((8, 2 * tq), F32)
    for a, s_a in enumerate(strips):
        lo, w = a * LANES, tq - a * LANES
        p = jnp.exp2(s_a - cat([m0[:, lo:tq], m0[:, tq + lo:]]))
        p_ref[lo:lo + LANES, :] = widen(p.astype(BF16), lo, w, 0.0)
        l0 = l0 + widen(jnp.sum(p.reshape(LANES // 8, 8, 2 * w), axis=0), lo, w, 0.0)
    l_ref[...] = l0
    acc_ref[...] = jnp.zeros_like(acc_ref)

    def frozen_step(j, j_pending):
        s_j = scores(j)
        flush(j_pending, None)
        l_ref[...] += exp_store(s_j, None)

    def frozen_steps(j0, n):
        for u in range(n):
            frozen_step(j0 + u, pending(j0) if u == 0 else j0 + u - 1)

    def frozen_group(t, carry):
        frozen_steps(ATT_UNROLL * t, ATT_UNROLL)
        return carry

    def online_body(j, carry):
        s_j = scores(j)
        m_old = m_ref[...]
        m_new = jnp.maximum(m_old, jnp.max(s_j, axis=0, keepdims=True))
        alpha = jnp.exp2(m_old - m_new)
        flush(pending(j), alpha)
        m_ref[...] = m_new
        l_ref[...] = l_ref[...] * alpha + exp_store(s_j, m_new)
        return carry

    @pl.when(frozen_ok)
    def _():
        groups = lax.shift_right_logical(i, ATT_UNROLL.bit_length() - 1)
        lax.fori_loop(0, groups, frozen_group, 0)
        j0 = groups * ATT_UNROLL
        n = ATT_UNROLL // 2
        while n >= 1:
            @pl.when((i & n) != 0)
            def _(j0=j0, n=n):
                frozen_steps(j0, n)
            j0 = j0 + (i & n)
            n //= 2

    @pl.when(jnp.logical_not(frozen_ok))
    def _():
        lax.fori_loop(0, i, online_body, 0)


def _attn_call(lam, q_t, k_nat, v_t, z_t, ksq, sw, lam_init):
    dv = 2 * DIFF_D
    s = k_nat.shape[0]
    h = k_nat.shape[1] // dv
    tq = ATT_TQ
    tg = s // ATT_STEPS_PER_HEAD
    assert ATT_TQ == ATT_TK and tg % tq == 0
    return pl.pallas_call(
        functools.partial(_attn_kernel, lam_init=lam_init),
        grid=(h, s // tg),
        in_specs=[
            pl.BlockSpec(memory_space=pltpu.SMEM),
            pl.BlockSpec((dv, tg), lambda hh, i: (hh, i)),
            pl.BlockSpec((s, dv), lambda hh, i: (0, hh)),
            pl.BlockSpec((dv, s), lambda hh, i: (hh, 0)),
            pl.BlockSpec((dv, tg), lambda hh, i: (hh, i)),
            pl.BlockSpec((1, 2, s), lambda hh, i: (hh, 0, 0)),
            pl.BlockSpec((dv, tq), lambda hh, i: (0, 0)),
        ],
        out_specs=pl.BlockSpec((tg, dv), lambda hh, i: (i, hh)),
        out_shape=jax.ShapeDtypeStruct((s, h * dv), BF16),
        scratch_shapes=[
            pltpu.VMEM((2, 2, dv, tq), F32),
            pltpu.VMEM((2, 8, 2 * tq), F32),
            pltpu.VMEM((1, 2 * tq), F32),
            pltpu.VMEM((ATT_TK, 2 * tq), BF16),
        ],
        compiler_params=pltpu.CompilerParams(
            dimension_semantics=("arbitrary", "arbitrary"), vmem_limit_bytes=VMEM_LIMIT),
        name="attn",
    )(lam, q_t, k_nat, v_t, z_t, ksq, sw)


def _layer(l, x2, rope, w_norm, w_in, conv_w, a_log, dt_bias, gdn_norm_w, q_norm_w,
           k_norm_w, lambda_q1, lambda_k1, lambda_q2, lambda_k2, subln_w, w_out):
    s, d = x2.shape
    nqk = GDN_HEADS * GDN_D
    ba_lo = 4 * nqk
    ba_hi = ba_lo + 2 * GDN_HEADS
    wt_a = w_in[:, :ba_lo].T.astype(BF16)
    wt_b = w_in[:, ba_hi:].T.astype(BF16)
    wba = w_in[:, ba_lo:ba_hi].T.astype(BF16)
    qw = jnp.broadcast_to(q_norm_w[:, None], (DIFF_D, PROJ_TM)).astype(F32)
    kw = jnp.broadcast_to(k_norm_w[:, None], (DIFF_D, PROJ_TM)).astype(F32)

    cw = jnp.broadcast_to(conv_w.astype(F32)[:, :, None], (CONV_K, 3 * nqk, LANES))
    gdn_t, ba_t, qb_t, kb, ksq, vb_t, zb_t = _proj_call(
        x2, w_norm[None, :].astype(F32), wt_a, wt_b, wba, rope, qw, kw, cw)

    zeros4 = jnp.zeros((GDN_HEADS, LANES), F32)
    alog8 = jnp.concatenate([zeros4, jnp.broadcast_to(a_log.astype(F32)[:, None], (GDN_HEADS, LANES))], 0)
    dt8 = jnp.concatenate([zeros4, jnp.broadcast_to(dt_bias.astype(F32)[:, None], (GDN_HEADS, LANES))], 0)
    gw = jnp.broadcast_to(gdn_norm_w.astype(F32)[:, None], (GDN_D, LANES))

    lam_init = 0.8 - 0.6 * math.exp(-0.3 * l)
    lam = (jnp.exp(jnp.sum(lambda_q1.astype(F32) * lambda_k1.astype(F32)))
           - jnp.exp(jnp.sum(lambda_q2.astype(F32) * lambda_k2.astype(F32))) + lam_init)
    dv = 2 * DIFF_D
    sw = jnp.broadcast_to(subln_w.astype(F32)[:, None], (dv, ATT_TQ))
    ob = _attn_call(lam.reshape(1, 1).astype(F32), qb_t, kb, vb_t, zb_t, ksq, sw, lam_init)

    return _gdn_call(gdn_t, ba_t, alog8, dt8, gw, x2, ob, w_out.astype(BF16))


def kernel(x, w_norm, w_in, conv_w, a_log, dt_bias, gdn_norm_w, q_norm_w, k_norm_w,
           lambda_q1, lambda_k1, lambda_q2, lambda_k2, subln_w, w_out):
    b, s, d = x.shape
    assert b == 1
    inv_freq = ROPE_THETA ** (-jnp.arange(0, DIFF_D, 2, dtype=jnp.float32) / DIFF_D)
    per_tile = PROJ_TM // LANES
    coarse = jnp.arange(0, s, LANES, dtype=jnp.float32)[:, None] * inv_freq[None, :]
    coarse = coarse.reshape(s // PROJ_TM, per_tile, DIFF_D // 2).transpose(0, 2, 1)
    fine = inv_freq[:, None] * jnp.arange(LANES, dtype=jnp.float32)[None, :]
    rope = (jnp.cos(coarse), jnp.sin(coarse), jnp.cos(fine), jnp.sin(fine))
    x2 = x[0]
    for l in range(w_norm.shape[0]):
        x2 = _layer(l, x2, rope, w_norm[l], w_in[l], conv_w[l], a_log[l], dt_bias[l],
                    gdn_norm_w[l], q_norm_w[l], k_norm_w[l], lambda_q1[l], lambda_k1[l],
                    lambda_q2[l], lambda_k2[l], subln_w[l], w_out[l])
    return x2[None]
```

```python
import functools
import math

import jax
import jax.numpy as jnp
from jax import lax
from jax.experimental import pallas as pl
from jax.experimental.pallas import tpu as pltpu

F32 = jnp.float32
BF16 = jnp.bfloat16

GDN_HEADS = 4
GDN_D = 128
CONV_K = 4
DIFF_HEADS = 4
DIFF_D = 64
ROPE_THETA = 10000.0
EPS = 1e-6
SUBLN_EPS = 1e-5

HEAD_GROUP_ROWS = GDN_HEADS * GDN_D
assert HEAD_GROUP_ROWS == DIFF_HEADS * 2 * DIFF_D

LANES = 128
CHUNK = LANES
NEG_BIG = -1e30
LOG2E = 1.4426950408889634
FROZEN_MAX_BOUND = 40.0
NORM_SLACK = 1.01

PROJ_TM = 1024
GDN_TB = 1024
ATT_TQ = 512
ATT_TK = 512
ATT_UNROLL = 8
ATT_SUB = 2
OUT_PIECE = 256
VMEM_LIMIT = 56 * 1024 * 1024

NT_DIMS = (((1,), (1,)), ((), ()))
TN_DIMS = (((0,), (0,)), ((), ()))


def _sigmoid(v):
    return 1.0 / (1.0 + jnp.exp(-v))


def _silu(v):
    return v * _sigmoid(v)


def _proj_kernel(x_ref, wn_ref, wa_ref, wb_ref, ca_ref, sa_ref, cb_ref, sb_ref,
                 qw_ref, kw_ref, cw_ref,
                 gdn_ref, ba_ref, qb_ref, kb_ref, ks_ref, vb_ref, zb_ref, halo_ref):
    @pl.when(pl.program_id(0) == 0)
    def _():
        halo_ref[...] = jnp.zeros_like(halo_ref)

    x = x_ref[...]
    ms = jnp.mean(x * x, axis=-1, keepdims=True)
    hn = (x * lax.rsqrt(ms + EPS) * wn_ref[...]).astype(BF16)

    blk = HEAD_GROUP_ROWS
    n_group_a = 4 * blk
    n_gates = wa_ref.shape[0] - n_group_a

    def proj_t(lo, n):
        w = wa_ref[lo:lo + n, :] if lo < n_group_a else wb_ref[lo - n_group_a:lo - n_group_a + n, :]
        return lax.dot_general(w, hn, NT_DIMS, preferred_element_type=F32)

    tm = x.shape[0]
    lane_t = lax.broadcasted_iota(jnp.int32, (blk, LANES), 1)

    def conv_block(b):
        rows = slice(b * blk, (b + 1) * blk)
        raw = proj_t(b * blk, blk)
        tail = halo_ref[rows, :]
        halo_ref[rows, :] = raw[:, tm - LANES:]
        y = jnp.concatenate([cw_ref[CONV_K - 1, rows, :]] * (tm // LANES), axis=1) * raw
        for j in range(1, CONV_K):
            rolled = pltpu.roll(raw, j, axis=1)
            first = jnp.where(lane_t >= j, rolled[:, :LANES], pltpu.roll(tail, j, axis=1))
            shifted = jnp.concatenate([first, rolled[:, LANES:]], axis=1)
            y = y + jnp.concatenate([cw_ref[CONV_K - 1 - j, rows, :]] * (tm // LANES), axis=1) * shifted
        gdn_ref[rows, :] = _silu(y).astype(BF16)

    q_raw = proj_t(4 * blk, blk)
    k_raw = proj_t(5 * blk, blk)

    ca, sa, cb, sb = ca_ref[0], sa_ref[0], cb_ref[...], sb_ref[...]
    cos = jnp.concatenate([ca[:, c:c + 1] * cb - sa[:, c:c + 1] * sb for c in range(tm // LANES)], axis=1)
    sin = jnp.concatenate([sa[:, c:c + 1] * cb + ca[:, c:c + 1] * sb for c in range(tm // LANES)], axis=1)
    half = DIFF_D // 2

    def norm_rope(t, w, scale):
        outs = []
        for g in range(t.shape[0] // DIFF_D):
            tg = t[g * DIFF_D:(g + 1) * DIFF_D, :]
            tg = tg * lax.rsqrt(jnp.mean(tg * tg, axis=0, keepdims=True) + EPS) * w
            t1, t2 = tg[:half, :], tg[half:, :]
            outs.append((t1 * cos - t2 * sin) * scale)
            outs.append((t2 * cos + t1 * sin) * scale)
        return jnp.concatenate(outs, axis=0)

    qb_ref[...] = norm_rope(q_raw, qw_ref[...], LOG2E / math.sqrt(DIFF_D)).astype(BF16)
    kt = norm_rope(k_raw, kw_ref[...], 1.0)
    for g in range(2 * DIFF_HEADS):
        kg = kt[g * DIFF_D:(g + 1) * DIFF_D, :]
        ks_ref[g // 2, g % 2:g % 2 + 1, :] = jnp.sum(kg * kg, axis=0, keepdims=True)
    kb_ref[...] = kt.T.astype(BF16)

    for b in range(3):
        conv_block(b)
    zg = proj_t(3 * blk, blk + n_gates)
    gdn_ref[3 * blk:4 * blk, :] = _silu(zg[:blk]).astype(BF16)
    ba_ref[...] = zg[blk:]
    zb_ref[...] = _silu(proj_t(7 * blk, blk)).astype(BF16)
    vb_ref[...] = proj_t(6 * blk, blk).astype(BF16)


def _proj_call(x2, w_norm, wt_a, wt_b, rope, qw, kw, cw):
    s, d = x2.shape
    tm = PROJ_TM
    blk = HEAD_GROUP_ROWS
    n_gates = wt_a.shape[0] - 4 * blk
    cos_a, sin_a, cos_b, sin_b = rope
    const = lambda shape: pl.BlockSpec(shape, lambda i: (0,) * len(shape))
    cols = lambda rows: pl.BlockSpec((rows, tm), lambda i: (0, i))
    coarse = pl.BlockSpec((1,) + cos_a.shape[1:], lambda i: (i, 0, 0))
    return pl.pallas_call(
        _proj_kernel,
        grid=(s // tm,),
        in_specs=[
            pl.BlockSpec((tm, d), lambda i: (i, 0)),
            const((1, d)),
            const(wt_a.shape),
            const(wt_b.shape),
            coarse,
            coarse,
            const(cos_b.shape),
            const(sin_b.shape),
            const((DIFF_D, tm)),
            const((DIFF_D, tm)),
            const(cw.shape),
        ],
        out_specs=[
            cols(4 * blk),
            cols(n_gates),
            cols(blk),
            pl.BlockSpec((tm, blk), lambda i: (i, 0)),
            pl.BlockSpec((DIFF_HEADS, 2, tm), lambda i: (0, 0, i)),
            cols(blk),
            cols(blk),
        ],
        out_shape=[
            jax.ShapeDtypeStruct((4 * blk, s), BF16),
            jax.ShapeDtypeStruct((n_gates, s), F32),
            jax.ShapeDtypeStruct((blk, s), BF16),
            jax.ShapeDtypeStruct((s, blk), BF16),
            jax.ShapeDtypeStruct((DIFF_HEADS, 2, s), F32),
            jax.ShapeDtypeStruct((blk, s), BF16),
            jax.ShapeDtypeStruct((blk, s), BF16),
        ],
        scratch_shapes=[
            pltpu.VMEM((cw.shape[1], LANES), F32),
        ],
        compiler_params=pltpu.CompilerParams(
            dimension_semantics=("arbitrary",), vmem_limit_bytes=VMEM_LIMIT),
        name="proj",
    )(x2, w_norm, wt_a, wt_b, cos_a, sin_a, cos_b, sin_b, qw, kw, cw)


def _gdn_kernel(g_ref, ba_ref, alog_ref, dt_ref, gw_ref, x_ref, ob_ref, wo_ref,
                y_ref, st_ref):
    nqkv = 3 * GDN_HEADS * GDN_D
    n_chunks = g_ref.shape[1] // CHUNK
    n_a = GDN_HEADS * GDN_D

    @pl.when(pl.program_id(0) == 0)
    def _():
        st_ref[...] = jnp.zeros_like(st_ref)

    for lo in range(0, y_ref.shape[1], OUT_PIECE):
        _out_b_piece(x_ref, ob_ref, wo_ref, y_ref, n_a, lo)

    lane = lax.broadcasted_iota(jnp.int32, (CHUNK, CHUNK), 1)
    subl = lax.broadcasted_iota(jnp.int32, (CHUNK, CHUNK), 0)
    eye = (lane == subl).astype(F32)
    lane8 = lax.broadcasted_iota(jnp.int32, (8, CHUNK), 1)

    chains = []
    for c in range(n_chunks):
        lanes = slice(c * CHUNK, (c + 1) * CHUNK)
        y = g_ref[0:nqkv, lanes].astype(F32)

        ba = ba_ref[:, lanes]
        beta8 = _sigmoid(ba)
        sp = ba + dt_ref[...]
        softplus = jnp.maximum(sp, 0.0) + jnp.log(1.0 + jnp.exp(-jnp.abs(sp)))
        gc8 = (-LOG2E) * jnp.exp(alog_ref[...]) * softplus
        sh = 1
        while sh < CHUNK:
            gc8 = gc8 + jnp.where(lane8 >= sh, pltpu.roll(gc8, sh, axis=1), 0.0)
            sh *= 2

        for h in range(GDN_HEADS):
            qt = y[h * GDN_D:(h + 1) * GDN_D, :]
            kt = y[(GDN_HEADS + h) * GDN_D:(GDN_HEADS + h + 1) * GDN_D, :]
            vt = y[(2 * GDN_HEADS + h) * GDN_D:(2 * GDN_HEADS + h + 1) * GDN_D, :]
            qt = qt * lax.rsqrt(jnp.sum(qt * qt, axis=0, keepdims=True) + EPS) * (GDN_D ** -0.5)
            kt = kt * lax.rsqrt(jnp.sum(kt * kt, axis=0, keepdims=True) + EPS)
            beta = beta8[h:h + 1, :]
            gc = gc8[GDN_HEADS + h:GDN_HEADS + h + 1, :]
            g_row = jnp.broadcast_to(gc, (CHUNK, CHUNK))
            g_col = g_row.T
            g_last = g_col[CHUNK - 1:CHUNK, :]
            e_gc = jnp.exp2(gc)
            chains.append(dict(
                c=c, h=h, qt=qt, kt=kt, beta=beta, g_last=g_last, g_col=g_col,
                decay_t=jnp.exp2(jnp.where(lane >= subl, g_row - g_col, NEG_BIG)),
                k_nat=kt.T,
                qg=qt * e_gc,
                rhs=jnp.concatenate([vt * beta, kt * (beta * e_gc)], axis=0).astype(BF16)))

    for ch in chains:
        gram = jnp.dot(ch["k_nat"].astype(BF16),
                       jnp.concatenate([ch["kt"], ch["qt"]], axis=1).astype(BF16),
                       preferred_element_type=F32)
        ch["a_t"] = gram[:, CHUNK:] * ch["decay_t"]
        ch["n"] = jnp.where(lane > subl, gram[:, :CHUNK] * ch["decay_t"] * (-ch["beta"]), 0.0)

    for ch in chains:
        nb = ch["n"].astype(BF16)
        ch["q"] = eye + ch["n"]
        ch["n"] = jnp.dot(nb, nb, preferred_element_type=F32)
    m = 2
    while 2 * m < CHUNK:
        for ch in chains:
            nb = ch["n"].astype(BF16)
            both = jnp.dot(jnp.concatenate([ch["q"].astype(BF16), nb], axis=0), nb,
                           preferred_element_type=F32)
            ch["q"] = ch["q"] + both[:CHUNK]
            ch["n"] = both[CHUNK:]
        m *= 2
    for ch in chains:
        ch["q"] = ch["q"] + jnp.dot(ch["q"].astype(BF16), ch["n"].astype(BF16),
                                    preferred_element_type=F32)

    for ch in chains:
        uw = jnp.dot(ch["rhs"], ch["q"].astype(BF16), preferred_element_type=F32)
        ch["u_t"] = uw[:GDN_D, :]
        ch["r1"] = jnp.concatenate([uw[GDN_D:, :], ch["qg"]], axis=1).astype(BF16)
        k_dec = ch["k_nat"] * jnp.exp2(ch["g_last"] - ch["g_col"])
        ch["r2"] = jnp.concatenate([k_dec, ch["a_t"]], axis=1).astype(BF16)

    def out_a_piece(c, oa_chunk):
        rows = slice(c * CHUNK, (c + 1) * CHUNK)
        y_ref[rows, :] += lax.dot_general(oa_chunk, wo_ref[0:n_a, :], TN_DIMS,
                                          preferred_element_type=F32)

    states = [st_ref[h] for h in range(GDN_HEADS)]
    ready = None
    for c in range(n_chunks):
        lanes = slice(c * CHUNK, (c + 1) * CHUNK)
        row = chains[c * GDN_HEADS:(c + 1) * GDN_HEADS]
        x1 = [jnp.dot(states[h].astype(BF16), row[h]["r1"], preferred_element_type=F32)
              for h in range(GDN_HEADS)]
        if ready is not None:
            out_a_piece(*ready)
        x2 = [jnp.dot((row[h]["u_t"] - x1[h][:, :CHUNK]).astype(BF16), row[h]["r2"],
                      preferred_element_type=F32) for h in range(GDN_HEADS)]
        oa = []
        for h in range(GDN_HEADS):
            states[h] = states[h] * jnp.exp2(row[h]["g_last"]) + x2[h][:, :GDN_D]
            o_t = x1[h][:, CHUNK:] + x2[h][:, GDN_D:]
            o_n = o_t * lax.rsqrt(jnp.mean(o_t * o_t, axis=0, keepdims=True) + EPS) * gw_ref[...]
            gate = g_ref[nqkv + h * GDN_D:nqkv + (h + 1) * GDN_D, lanes].astype(F32)
            oa.append((o_n * gate).astype(BF16))
        ready = (c, jnp.concatenate(oa, axis=0))
    out_a_piece(*ready)
    for h in range(GDN_HEADS):
        st_ref[h] = states[h]


def _out_b_piece(x_ref, ob_ref, wo_ref, y_ref, n_a, lo):
    cols = slice(lo, lo + OUT_PIECE)
    y_ref[:, cols] = x_ref[:, cols] + jnp.dot(
        ob_ref[...], wo_ref[n_a:, cols], preferred_element_type=F32)


def _gdn_call(gdn_t, ba_t, alog8, dt8, gw, x2, ob, w_out):
    s, d = x2.shape
    tb = GDN_TB
    const = lambda shape: pl.BlockSpec(shape, lambda i: (0,) * len(shape))
    return pl.pallas_call(
        _gdn_kernel,
        grid=(s // tb,),
        in_specs=[
            pl.BlockSpec((gdn_t.shape[0], tb), lambda i: (0, i)),
            pl.BlockSpec((8, tb), lambda i: (0, i)),
            const(alog8.shape),
            const(dt8.shape),
            const(gw.shape),
            pl.BlockSpec((tb, d), lambda i: (i, 0)),
            pl.BlockSpec((tb, ob.shape[1]), lambda i: (i, 0)),
            const(w_out.shape),
        ],
        out_specs=pl.BlockSpec((tb, d), lambda i: (i, 0)),
        out_shape=jax.ShapeDtypeStruct((s, d), F32),
        scratch_shapes=[
            pltpu.VMEM((GDN_HEADS, GDN_D, GDN_D), F32),
        ],
        compiler_params=pltpu.CompilerParams(
            dimension_semantics=("arbitrary",), vmem_limit_bytes=VMEM_LIMIT),
        name="gdn",
    )(gdn_t, ba_t, alog8, dt8, gw, x2, ob, w_out)


def _attn_kernel(lam_ref, q_ref, k_ref, v_ref, z_ref, ks_ref, sw_ref, o_ref,
                 acc_ref, l_ref, m_ref, p_ref, *, lam_init):
    for sub in range(ATT_SUB):
        _attn_tile(sub, lam_ref, q_ref, k_ref, v_ref, z_ref, ks_ref, sw_ref, o_ref,
                   acc_ref.at[sub], l_ref.at[sub], m_ref, p_ref, lam_init)


def _attn_tile(sub, lam_ref, q_ref, k_ref, v_ref, z_ref, ks_ref, sw_ref, o_ref,
               acc_ref, l_ref, m_ref, p_ref, lam_init):
    tq, tk = ATT_TQ, ATT_TK
    dv = 2 * DIFF_D
    cols = slice(sub * tq, (sub + 1) * tq)
    i = pl.program_id(1) * ATT_SUB + sub
    qt = q_ref[:, cols]
    row = lax.broadcasted_iota(jnp.int32, qt.shape, 0)
    zero = jnp.zeros_like(qt)
    q_cat = jnp.concatenate(
        [jnp.where(row < DIFF_D, qt, zero), jnp.where(row >= DIFF_D, qt, zero)], axis=1)

    def scores(j):
        start = pl.multiple_of(j * tk, tk)
        return jnp.dot(k_ref[pl.ds(start, tk), :], q_cat, preferred_element_type=F32)

    def flush(j_pending, alpha):
        start = pl.multiple_of(j_pending * tk, tk)
        vt = v_ref[:, pl.ds(start, tk)]
        for c in range(2):
            upd = acc_ref[c] + jnp.dot(vt, p_ref[:, c * tq:(c + 1) * tq],
                                       preferred_element_type=F32)
            acc_ref[c] = upd if alpha is None else upd * alpha[:, c * tq:(c + 1) * tq]

    def exp_store(s_j, m):
        p = jnp.exp2(s_j if m is None else s_j - m)
        p_ref[...] = p.astype(BF16)
        return jnp.sum(p.reshape(tk // 8, 8, 2 * tq), axis=0)

    def pending(j):
        return jnp.where(j == 0, i, j - 1)

    qf = qt.astype(F32)
    qq = qf * qf
    qsq = jnp.concatenate([jnp.sum(qq[:DIFF_D], axis=0, keepdims=True),
                           jnp.sum(qq[DIFF_D:], axis=0, keepdims=True)], axis=1)
    kmax = jnp.max(ks_ref[0], axis=1, keepdims=True)
    kmax = jnp.concatenate([jnp.broadcast_to(kmax[0:1], (1, tq)),
                            jnp.broadcast_to(kmax[1:2], (1, tq))], axis=1)
    frozen_ok = jnp.max(jnp.sqrt(qsq * kmax)) * NORM_SLACK <= FROZEN_MAX_BOUND

    tri = (lax.broadcasted_iota(jnp.int32, (LANES, LANES), 0)
           <= lax.broadcasted_iota(jnp.int32, (LANES, LANES), 1))
    def cat(parts):
        parts = [t for t in parts if t.shape[1] > 0]
        return parts[0] if len(parts) == 1 else jnp.concatenate(parts, axis=1)

    def widen(t, lo, w, fill):
        pad = jnp.full((t.shape[0], lo), fill, t.dtype)
        return cat([pad, t[:, :w], pad, t[:, w:]])

    strips = []
    m0 = jnp.full((1, 2 * tq), NEG_BIG, F32)
    for a in range(tk // LANES):
        lo, w = a * LANES, tq - a * LANES
        start = pl.multiple_of(i * tk + lo, LANES)
        s_a = jnp.dot(k_ref[pl.ds(start, LANES), :], cat([q_cat[:, lo:tq], q_cat[:, tq + lo:]]),
                      preferred_element_type=F32)
        s_a = cat([jnp.where(tri, s_a[:, :LANES], NEG_BIG), s_a[:, LANES:w],
                   jnp.where(tri, s_a[:, w:w + LANES], NEG_BIG), s_a[:, w + LANES:]])
        strips.append(s_a)
        m0 = jnp.maximum(m0, widen(jnp.max(s_a, axis=0, keepdims=True), lo, w, NEG_BIG))
    m0 = jnp.where(frozen_ok, 0.0, m0)
    m_ref[...] = m0
    l0 = jnp.zeros((8, 2 * tq), F32)
    for a, s_a in enumerate(strips):
        lo, w = a * LANES, tq - a * LANES
        p = jnp.exp2(s_a - cat([m0[:, lo:tq], m0[:, tq + lo:]]))
        p_ref[lo:lo + LANES, :] = widen(p.astype(BF16), lo, w, 0.0)
        l0 = l0 + widen(jnp.sum(p.reshape(LANES // 8, 8, 2 * w), axis=0), lo, w, 0.0)
    l_ref[...] = l0
    acc_ref[...] = jnp.zeros_like(acc_ref)

    def frozen_step(j, j_pending):
        s_j = scores(j)
        flush(j_pending, None)
        l_ref[...] += exp_store(s_j, None)

    def frozen_steps(j0, n):
        for u in range(n):
            frozen_step(j0 + u, pending(j0) if u == 0 else j0 + u - 1)

    def frozen_group(t, carry):
        frozen_steps(ATT_UNROLL * t, ATT_UNROLL)
        return carry

    def online_body(j, carry):
        s_j = scores(j)
        m_old = m_ref[...]
        m_new = jnp.maximum(m_old, jnp.max(s_j, axis=0, keepdims=True))
        alpha = jnp.exp2(m_old - m_new)
        flush(pending(j), alpha)
        m_ref[...] = m_new
        l_ref[...] = l_ref[...] * alpha + exp_store(s_j, m_new)
        return carry

    @pl.when(frozen_ok)
    def _():
        groups = lax.shift_right_logical(i, ATT_UNROLL.bit_length() - 1)
        lax.fori_loop(0, groups, frozen_group, 0)
        j0 = groups * ATT_UNROLL
        n = ATT_UNROLL // 2
        while n >= 1:
            @pl.when((i & n) != 0)
            def _(j0=j0, n=n):
                frozen_steps(j0, n)
            j0 = j0 + (i & n)
            n //= 2

    @pl.when(jnp.logical_not(frozen_ok))
    def _():
        lax.fori_loop(0, i, online_body, 0)

    flush(pending(i), None)

    lam = lam_ref[0, 0]
    l = jnp.sum(l_ref[...], axis=0, keepdims=True)
    o_t = acc_ref[0] / l[:, :tq] - lam * (acc_ref[1] / l[:, tq:])
    o_n = o_t * lax.rsqrt(jnp.mean(o_t * o_t, axis=0, keepdims=True) + SUBLN_EPS) * sw_ref[...]
    o_n = o_n * (1.0 - lam_init)
    o_ref[cols, :] = (o_n * z_ref[:, cols].astype(F32)).T.astype(o_ref.dtype)


def _attn_call(lam, q_t, k_nat, v_t, z_t, ksq, sw, lam_init):
    dv = 2 * DIFF_D
    s = k_nat.shape[0]
    h = k_nat.shape[1] // dv
    tq = ATT_TQ
    tg = ATT_SUB * tq
    assert ATT_TQ == ATT_TK
    return pl.pallas_call(
        functools.partial(_attn_kernel, lam_init=lam_init),
        grid=(h, s // tg),
        in_specs=[
            pl.BlockSpec(memory_space=pltpu.SMEM),
            pl.BlockSpec((dv, tg), lambda hh, i: (hh, i)),
            pl.BlockSpec((s, dv), lambda hh, i: (0, hh)),
            pl.BlockSpec((dv, s), lambda hh, i: (hh, 0)),
            pl.BlockSpec((dv, tg), lambda hh, i: (hh, i)),
            pl.BlockSpec((1, 2, s), lambda hh, i: (hh, 0, 0)),
            pl.BlockSpec((dv, tq), lambda hh, i: (0, 0)),
        ],
        out_specs=pl.BlockSpec((tg, dv), lambda hh, i: (i, hh)),
        out_shape=jax.ShapeDtypeStruct((s, h * dv), BF16),
        scratch_shapes=[
            pltpu.VMEM((ATT_SUB, 2, dv, tq), F32),
            pltpu.VMEM((ATT_SUB, 8, 2 * tq), F32),
            pltpu.VMEM((1, 2 * tq), F32),
            pltpu.VMEM((ATT_TK, 2 * tq), BF16),
        ],
        compiler_params=pltpu.CompilerParams(
            dimension_semantics=("arbitrary", "arbitrary"), vmem_limit_bytes=VMEM_LIMIT),
        name="attn",
    )(lam, q_t, k_nat, v_t, z_t, ksq, sw)


def _layer(l, x2, rope, w_norm, w_in, conv_w, a_log, dt_bias, gdn_norm_w, q_norm_w,
           k_norm_w, lambda_q1, lambda_k1, lambda_q2, lambda_k2, subln_w, w_out):
    s, d = x2.shape
    nqk = GDN_HEADS * GDN_D
    ba_lo = 4 * nqk
    ba_hi = ba_lo + 2 * GDN_HEADS
    wt_a = w_in[:, :ba_hi].T.astype(BF16)
    wt_b = w_in[:, ba_hi:].T.astype(BF16)
    qw = jnp.broadcast_to(q_norm_w[:, None], (DIFF_D, PROJ_TM)).astype(F32)
    kw = jnp.broadcast_to(k_norm_w[:, None], (DIFF_D, PROJ_TM)).astype(F32)

    cw = jnp.broadcast_to(conv_w.astype(F32)[:, :, None], (CONV_K, 3 * nqk, LANES))
    gdn_t, ba_t, qb_t, kb, ksq, vb_t, zb_t = _proj_call(
        x2, w_norm[None, :].astype(F32), wt_a, wt_b, rope, qw, kw, cw)

    zeros4 = jnp.zeros((GDN_HEADS, LANES), F32)
    alog8 = jnp.concatenate([zeros4, jnp.broadcast_to(a_log.astype(F32)[:, None], (GDN_HEADS, LANES))], 0)
    dt8 = jnp.concatenate([zeros4, jnp.broadcast_to(dt_bias.astype(F32)[:, None], (GDN_HEADS, LANES))], 0)
    gw = jnp.broadcast_to(gdn_norm_w.astype(F32)[:, None], (GDN_D, LANES))

    lam_init = 0.8 - 0.6 * math.exp(-0.3 * l)
    lam = (jnp.exp(jnp.sum(lambda_q1.astype(F32) * lambda_k1.astype(F32)))
           - jnp.exp(jnp.sum(lambda_q2.astype(F32) * lambda_k2.astype(F32))) + lam_init)
    dv = 2 * DIFF_D
    sw = jnp.broadcast_to(subln_w.astype(F32)[:, None], (dv, ATT_TQ))
    ob = _attn_call(lam.reshape(1, 1).astype(F32), qb_t, kb, vb_t, zb_t, ksq, sw, lam_init)

    return _gdn_call(gdn_t, ba_t, alog8, dt8, gw, x2, ob, w_out.astype(BF16))


def kernel(x, w_norm, w_in, conv_w, a_log, dt_bias, gdn_norm_w, q_norm_w, k_norm_w,
           lambda_q1, lambda_k1, lambda_q2, lambda_k2, subln_w, w_out):
    b, s, d = x.shape
    assert b == 1
    inv_freq = ROPE_THETA ** (-jnp.arange(0, DIFF_D, 2, dtype=jnp.float32) / DIFF_D)
    per_tile = PROJ_TM // LANES
    coarse = jnp.arange(0, s, LANES, dtype=jnp.float32)[:, None] * inv_freq[None, :]
    coarse = coarse.reshape(s // PROJ_TM, per_tile, DIFF_D // 2).transpose(0, 2, 1)
    fine = inv_freq[:, None] * jnp.arange(LANES, dtype=jnp.float32)[None, :]
    rope = (jnp.cos(coarse), jnp.sin(coarse), jnp.cos(fine), jnp.sin(fine))
    x2 = x[0]
    for l in range(w_norm.shape[0]):
        x2 = _layer(l, x2, rope, w_norm[l], w_in[l], conv_w[l], a_log[l], dt_bias[l],
                    gdn_norm_w[l], q_norm_w[l], k_norm_w[l], lambda_q1[l], lambda_k1[l],
                    lambda_q2[l], lambda_k2[l], subln_w[l], w_out[l])
    return x2[None]
```

```python
import functools
import math

import jax
import jax.numpy as jnp
from jax import lax
from jax.experimental import pallas as pl
from jax.experimental.pallas import tpu as pltpu

F32 = jnp.float32
BF16 = jnp.bfloat16

GDN_HEADS = 4
GDN_D = 128
CONV_K = 4
DIFF_HEADS = 4
DIFF_D = 64
ROPE_THETA = 10000.0
EPS = 1e-6
SUBLN_EPS = 1e-5

HEAD_GROUP_ROWS = GDN_HEADS * GDN_D
assert HEAD_GROUP_ROWS == DIFF_HEADS * 2 * DIFF_D

LANES = 128
CHUNK = LANES
NEG_BIG = -1e30
LOG2E = 1.4426950408889634
FROZEN_MAX_BOUND = 40.0
NORM_SLACK = 1.01

PROJ_TM = 1024
GDN_TB = 1024
ATT_TQ = 1024
ATT_TK = 512
ATT_UNROLL = 4
ATT_SUB = 2
OUT_PIECE = 256
VMEM_LIMIT = 56 * 1024 * 1024

NT_DIMS = (((1,), (1,)), ((), ()))
TN_DIMS = (((0,), (0,)), ((), ()))


def _sigmoid(v):
    return 1.0 / (1.0 + jnp.exp(-v))


def _silu(v):
    return v * _sigmoid(v)


def _proj_kernel(x_ref, wn_ref, wa_ref, wb_ref, ca_ref, sa_ref, cb_ref, sb_ref,
                 qw_ref, kw_ref, cw_ref,
                 gdn_ref, ba_ref, qb_ref, kb_ref, ks_ref, vb_ref, zb_ref, halo_ref):
    @pl.when(pl.program_id(0) == 0)
    def _():
        halo_ref[...] = jnp.zeros_like(halo_ref)

    x = x_ref[...]
    ms = jnp.mean(x * x, axis=-1, keepdims=True)
    hn = (x * lax.rsqrt(ms + EPS) * wn_ref[...]).astype(BF16)

    blk = HEAD_GROUP_ROWS
    n_group_a = 4 * blk
    n_gates = wa_ref.shape[0] - n_group_a

    def proj_t(lo, n):
        w = wa_ref[lo:lo + n, :] if lo < n_group_a else wb_ref[lo - n_group_a:lo - n_group_a + n, :]
        return lax.dot_general(w, hn, NT_DIMS, preferred_element_type=F32)

    tm = x.shape[0]
    lane_t = lax.broadcasted_iota(jnp.int32, (blk, LANES), 1)

    def conv_block(b):
        rows = slice(b * blk, (b + 1) * blk)
        raw = proj_t(b * blk, blk)
        tail = halo_ref[rows, :]
        halo_ref[rows, :] = raw[:, tm - LANES:]
        y = jnp.concatenate([cw_ref[CONV_K - 1, rows, :]] * (tm // LANES), axis=1) * raw
        for j in range(1, CONV_K):
            rolled = pltpu.roll(raw, j, axis=1)
            first = jnp.where(lane_t >= j, rolled[:, :LANES], pltpu.roll(tail, j, axis=1))
            shifted = jnp.concatenate([first, rolled[:, LANES:]], axis=1)
            y = y + jnp.concatenate([cw_ref[CONV_K - 1 - j, rows, :]] * (tm // LANES), axis=1) * shifted
        gdn_ref[rows, :] = _silu(y).astype(BF16)

    q_raw = proj_t(4 * blk, blk)
    k_raw = proj_t(5 * blk, blk)

    ca, sa, cb, sb = ca_ref[0], sa_ref[0], cb_ref[...], sb_ref[...]
    cos = jnp.concatenate([ca[:, c:c + 1] * cb - sa[:, c:c + 1] * sb for c in range(tm // LANES)], axis=1)
    sin = jnp.concatenate([sa[:, c:c + 1] * cb + ca[:, c:c + 1] * sb for c in range(tm // LANES)], axis=1)
    half = DIFF_D // 2

    def norm_rope(t, w, scale):
        outs = []
        for g in range(t.shape[0] // DIFF_D):
            tg = t[g * DIFF_D:(g + 1) * DIFF_D, :]
            tg = tg * lax.rsqrt(jnp.mean(tg * tg, axis=0, keepdims=True) + EPS) * w
            t1, t2 = tg[:half, :], tg[half:, :]
            outs.append((t1 * cos - t2 * sin) * scale)
            outs.append((t2 * cos + t1 * sin) * scale)
        return jnp.concatenate(outs, axis=0)

    qb_ref[...] = norm_rope(q_raw, qw_ref[...], LOG2E / math.sqrt(DIFF_D)).astype(BF16)
    kt = norm_rope(k_raw, kw_ref[...], 1.0)
    for g in range(2 * DIFF_HEADS):
        kg = kt[g * DIFF_D:(g + 1) * DIFF_D, :]
        ks_ref[g // 2, g % 2:g % 2 + 1, :] = jnp.sum(kg * kg, axis=0, keepdims=True)
    kb_ref[...] = kt.T.astype(BF16)

    for b in range(3):
        conv_block(b)
    zg = proj_t(3 * blk, blk + n_gates)
    gdn_ref[3 * blk:4 * blk, :] = _silu(zg[:blk]).astype(BF16)
    ba_ref[...] = zg[blk:]
    zb_ref[...] = _silu(proj_t(7 * blk, blk)).astype(BF16)
    vb_ref[...] = proj_t(6 * blk, blk).astype(BF16)


def _proj_call(x2, w_norm, wt_a, wt_b, rope, qw, kw, cw):
    s, d = x2.shape
    tm = PROJ_TM
    blk = HEAD_GROUP_ROWS
    n_gates = wt_a.shape[0] - 4 * blk
    cos_a, sin_a, cos_b, sin_b = rope
    const = lambda shape: pl.BlockSpec(shape, lambda i: (0,) * len(shape))
    cols = lambda rows: pl.BlockSpec((rows, tm), lambda i: (0, i))
    coarse = pl.BlockSpec((1,) + cos_a.shape[1:], lambda i: (i, 0, 0))
    return pl.pallas_call(
        _proj_kernel,
        grid=(s // tm,),
        in_specs=[
            pl.BlockSpec((tm, d), lambda i: (i, 0)),
            const((1, d)),
            const(wt_a.shape),
            const(wt_b.shape),
            coarse,
            coarse,
            const(cos_b.shape),
            const(sin_b.shape),
            const((DIFF_D, tm)),
            const((DIFF_D, tm)),
            const(cw.shape),
        ],
        out_specs=[
            cols(4 * blk),
            cols(n_gates),
            cols(blk),
            pl.BlockSpec((tm, blk), lambda i: (i, 0)),
            pl.BlockSpec((DIFF_HEADS, 2, tm), lambda i: (0, 0, i)),
            cols(blk),
            cols(blk),
        ],
        out_shape=[
            jax.ShapeDtypeStruct((4 * blk, s), BF16),
            jax.ShapeDtypeStruct((n_gates, s), F32),
            jax.ShapeDtypeStruct((blk, s), BF16),
            jax.ShapeDtypeStruct((s, blk), BF16),
            jax.ShapeDtypeStruct((DIFF_HEADS, 2, s), F32),
            jax.ShapeDtypeStruct((blk, s), BF16),
            jax.ShapeDtypeStruct((blk, s), BF16),
        ],
        scratch_shapes=[
            pltpu.VMEM((cw.shape[1], LANES), F32),
        ],
        compiler_params=pltpu.CompilerParams(
            dimension_semantics=("arbitrary",), vmem_limit_bytes=VMEM_LIMIT),
        name="proj",
    )(x2, w_norm, wt_a, wt_b, cos_a, sin_a, cos_b, sin_b, qw, kw, cw)


def _gdn_kernel(g_ref, ba_ref, alog_ref, dt_ref, gw_ref, x_ref, ob_ref, wo_ref,
                y_ref, st_ref):
    nqkv = 3 * GDN_HEADS * GDN_D
    n_chunks = g_ref.shape[1] // CHUNK
    n_a = GDN_HEADS * GDN_D

    @pl.when(pl.program_id(0) == 0)
    def _():
        st_ref[...] = jnp.zeros_like(st_ref)

    for lo in range(0, y_ref.shape[1], OUT_PIECE):
        _out_b_piece(x_ref, ob_ref, wo_ref, y_ref, n_a, lo)

    lane = lax.broadcasted_iota(jnp.int32, (CHUNK, CHUNK), 1)
    subl = lax.broadcasted_iota(jnp.int32, (CHUNK, CHUNK), 0)
    eye = (lane == subl).astype(F32)
    lane8 = lax.broadcasted_iota(jnp.int32, (8, CHUNK), 1)

    chains = []
    for c in range(n_chunks):
        lanes = slice(c * CHUNK, (c + 1) * CHUNK)
        y = g_ref[0:nqkv, lanes].astype(F32)

        ba = ba_ref[:, lanes]
        beta8 = _sigmoid(ba)
        sp = ba + dt_ref[...]
        softplus = jnp.maximum(sp, 0.0) + jnp.log(1.0 + jnp.exp(-jnp.abs(sp)))
        gc8 = (-LOG2E) * jnp.exp(alog_ref[...]) * softplus
        sh = 1
        while sh < CHUNK:
            gc8 = gc8 + jnp.where(lane8 >= sh, pltpu.roll(gc8, sh, axis=1), 0.0)
            sh *= 2

        for h in range(GDN_HEADS):
            qt = y[h * GDN_D:(h + 1) * GDN_D, :]
            kt = y[(GDN_HEADS + h) * GDN_D:(GDN_HEADS + h + 1) * GDN_D, :]
            vt = y[(2 * GDN_HEADS + h) * GDN_D:(2 * GDN_HEADS + h + 1) * GDN_D, :]
            qt = qt * lax.rsqrt(jnp.sum(qt * qt, axis=0, keepdims=True) + EPS) * (GDN_D ** -0.5)
            kt = kt * lax.rsqrt(jnp.sum(kt * kt, axis=0, keepdims=True) + EPS)
            beta = beta8[h:h + 1, :]
            gc = gc8[GDN_HEADS + h:GDN_HEADS + h + 1, :]
            g_row = jnp.broadcast_to(gc, (CHUNK, CHUNK))
            g_col = g_row.T
            g_last = g_col[CHUNK - 1:CHUNK, :]
            e_gc = jnp.exp2(gc)
            chains.append(dict(
                c=c, h=h, qt=qt, kt=kt, beta=beta, g_last=g_last, g_col=g_col,
                decay_t=jnp.exp2(jnp.where(lane >= subl, g_row - g_col, NEG_BIG)),
                k_nat=kt.T,
                qg=qt * e_gc,
                rhs=jnp.concatenate([vt * beta, kt * (beta * e_gc)], axis=0).astype(BF16)))

    for ch in chains:
        gram = jnp.dot(ch["k_nat"].astype(BF16),
                       jnp.concatenate([ch["kt"], ch["qt"]], axis=1).astype(BF16),
                       preferred_element_type=F32)
        ch["a_t"] = gram[:, CHUNK:] * ch["decay_t"]
        ch["n"] = jnp.where(lane > subl, gram[:, :CHUNK] * ch["decay_t"] * (-ch["beta"]), 0.0)

    for ch in chains:
        nb = ch["n"].astype(BF16)
        ch["q"] = eye + ch["n"]
        ch["n"] = jnp.dot(nb, nb, preferred_element_type=F32)
    m = 2
    while 2 * m < CHUNK:
        for ch in chains:
            nb = ch["n"].astype(BF16)
            both = jnp.dot(jnp.concatenate([ch["q"].astype(BF16), nb], axis=0), nb,
                           preferred_element_type=F32)
            ch["q"] = ch["q"] + both[:CHUNK]
            ch["n"] = both[CHUNK:]
        m *= 2
    for ch in chains:
        ch["q"] = ch["q"] + jnp.dot(ch["q"].astype(BF16), ch["n"].astype(BF16),
                                    preferred_element_type=F32)

    for ch in chains:
        uw = jnp.dot(ch["rhs"], ch["q"].astype(BF16), preferred_element_type=F32)
        ch["u_t"] = uw[:GDN_D, :]
        ch["r1"] = jnp.concatenate([uw[GDN_D:, :], ch["qg"]], axis=1).astype(BF16)
        k_dec = ch["k_nat"] * jnp.exp2(ch["g_last"] - ch["g_col"])
        ch["r2"] = jnp.concatenate([k_dec, ch["a_t"]], axis=1).astype(BF16)

    def out_a_piece(c, oa_chunk):
        rows = slice(c * CHUNK, (c + 1) * CHUNK)
        y_ref[rows, :] += lax.dot_general(oa_chunk, wo_ref[0:n_a, :], TN_DIMS,
                                          preferred_element_type=F32)

    states = [st_ref[h] for h in range(GDN_HEADS)]
    ready = None
    for c in range(n_chunks):
        lanes = slice(c * CHUNK, (c + 1) * CHUNK)
        row = chains[c * GDN_HEADS:(c + 1) * GDN_HEADS]
        x1 = [jnp.dot(states[h].astype(BF16), row[h]["r1"], preferred_element_type=F32)
              for h in range(GDN_HEADS)]
        if ready is not None:
            out_a_piece(*ready)
        x2 = [jnp.dot((row[h]["u_t"] - x1[h][:, :CHUNK]).astype(BF16), row[h]["r2"],
                      preferred_element_type=F32) for h in range(GDN_HEADS)]
        oa = []
        for h in range(GDN_HEADS):
            states[h] = states[h] * jnp.exp2(row[h]["g_last"]) + x2[h][:, :GDN_D]
            o_t = x1[h][:, CHUNK:] + x2[h][:, GDN_D:]
            o_n = o_t * lax.rsqrt(jnp.mean(o_t * o_t, axis=0, keepdims=True) + EPS) * gw_ref[...]
            gate = g_ref[nqkv + h * GDN_D:nqkv + (h + 1) * GDN_D, lanes].astype(F32)
            oa.append((o_n * gate).astype(BF16))
        ready = (c, jnp.concatenate(oa, axis=0))
    out_a_piece(*ready)
    for h in range(GDN_HEADS):
        st_ref[h] = states[h]


def _out_b_piece(x_ref, ob_ref, wo_ref, y_ref, n_a, lo):
    cols = slice(lo, lo + OUT_PIECE)
    y_ref[:, cols] = x_ref[:, cols] + jnp.dot(
        ob_ref[...], wo_ref[n_a:, cols], preferred_element_type=F32)


def _gdn_call(gdn_t, ba_t, alog8, dt8, gw, x2, ob, w_out):
    s, d = x2.shape
    tb = GDN_TB
    const = lambda shape: pl.BlockSpec(shape, lambda i: (0,) * len(shape))
    return pl.pallas_call(
        _gdn_kernel,
        grid=(s // tb,),
        in_specs=[
            pl.BlockSpec((gdn_t.shape[0], tb), lambda i: (0, i)),
            pl.BlockSpec((8, tb), lambda i: (0, i)),
            const(alog8.shape),
            const(dt8.shape),
            const(gw.shape),
            pl.BlockSpec((tb, d), lambda i: (i, 0)),
            pl.BlockSpec((tb, ob.shape[1]), lambda i: (i, 0)),
            const(w_out.shape),
        ],
        out_specs=pl.BlockSpec((tb, d), lambda i: (i, 0)),
        out_shape=jax.ShapeDtypeStruct((s, d), F32),
        scratch_shapes=[
            pltpu.VMEM((GDN_HEADS, GDN_D, GDN_D), F32),
        ],
        compiler_params=pltpu.CompilerParams(
            dimension_semantics=("arbitrary",), vmem_limit_bytes=VMEM_LIMIT),
        name="gdn",
    )(gdn_t, ba_t, alog8, dt8, gw, x2, ob, w_out)


def _attn_kernel(lam_ref, q_ref, k_ref, v_ref, z_ref, ks_ref, sw_ref, o_ref,
                 acc_ref, l_ref, m_ref, p_ref, *, lam_init):
    for sub in range(ATT_SUB):
        _attn_tile(sub, lam_ref, q_ref, k_ref, v_ref, z_ref, ks_ref, sw_ref, o_ref,
                   acc_ref.at[sub], l_ref.at[sub], m_ref, p_ref, lam_init)


def _attn_tile(sub, lam_ref, q_ref, k_ref, v_ref, z_ref, ks_ref, sw_ref, o_ref,
               acc_ref, l_ref, m_ref, p_ref, lam_init):
    tq, tk = ATT_TQ, ATT_TK
    dv = 2 * DIFF_D
    cols = slice(sub * tq, (sub + 1) * tq)
    i = pl.program_id(1) * ATT_SUB + sub
    qt = q_ref[:, cols]
    row = lax.broadcasted_iota(jnp.int32, qt.shape, 0)
    zero = jnp.zeros_like(qt)
    q_cat = jnp.concatenate(
        [jnp.where(row < DIFF_D, qt, zero), jnp.where(row >= DIFF_D, qt, zero)], axis=1)

    def scores(j):
        start = pl.multiple_of(j * tk, tk)
        return jnp.dot(k_ref[pl.ds(start, tk), :], q_cat, preferred_element_type=F32)

    def flush(j_pending, alpha):
        start = pl.multiple_of(j_pending * tk, tk)
        vt = v_ref[:, pl.ds(start, tk)]
        for c in range(2):
            upd = acc_ref[c] + jnp.dot(vt, p_ref[:, c * tq:(c + 1) * tq],
                                       preferred_element_type=F32)
            acc_ref[c] = upd if alpha is None else upd * alpha[:, c * tq:(c + 1) * tq]

    def exp_store(s_j, m):
        p = jnp.exp2(s_j if m is None else s_j - m)
        p_ref[...] = p.astype(BF16)
        return jnp.sum(p.reshape(tk // 8, 8, 2 * tq), axis=0)

    n_diag = tq // tk
    n_kv = i * n_diag

    def pending(j):
        return jnp.where(j == 0, n_kv + n_diag - 1, j - 1)

    qf = qt.astype(F32)
    qq = qf * qf
    qsq = jnp.concatenate([jnp.sum(qq[:DIFF_D], axis=0, keepdims=True),
                           jnp.sum(qq[DIFF_D:], axis=0, keepdims=True)], axis=1)
    kmax = jnp.max(ks_ref[0], axis=1, keepdims=True)
    kmax = jnp.concatenate([jnp.broadcast_to(kmax[0:1], (1, tq)),
                            jnp.broadcast_to(kmax[1:2], (1, tq))], axis=1)
    frozen_ok = jnp.max(jnp.sqrt(qsq * kmax)) * NORM_SLACK <= FROZEN_MAX_BOUND

    tri = (lax.broadcasted_iota(jnp.int32, (LANES, LANES), 0)
           <= lax.broadcasted_iota(jnp.int32, (LANES, LANES), 1))
    def cat(parts):
        parts = [t for t in parts if t.shape[1] > 0]
        return parts[0] if len(parts) == 1 else jnp.concatenate(parts, axis=1)

    def widen(t, lo, w, fill):
        pad = jnp.full((t.shape[0], lo), fill, t.dtype)
        return cat([pad, t[:, :w], pad, t[:, w:]])

    strips = []
    m0 = jnp.full((1, 2 * tq), NEG_BIG, F32)
    for a in range(tq // LANES):
        lo, w = a * LANES, tq - a * LANES
        start = pl.multiple_of(i * tq + lo, LANES)
        s_a = jnp.dot(k_ref[pl.ds(start, LANES), :], cat([q_cat[:, lo:tq], q_cat[:, tq + lo:]]),
                      preferred_element_type=F32)
        s_a = cat([jnp.where(tri, s_a[:, :LANES], NEG_BIG), s_a[:, LANES:w],
                   jnp.where(tri, s_a[:, w:w + LANES], NEG_BIG), s_a[:, w + LANES:]])
        strips.append(s_a)
        m0 = jnp.maximum(m0, widen(jnp.max(s_a, axis=0, keepdims=True), lo, w, NEG_BIG))
    m0 = jnp.where(frozen_ok, 0.0, m0)
    m_ref[...] = m0
    acc_ref[...] = jnp.zeros_like(acc_ref)
    l0 = jnp.zeros((8, 2 * tq), F32)
    per_tile = tk // LANES
    for a, s_a in enumerate(strips):
        lo, w = a * LANES, tq - a * LANES
        if a % per_tile == 0 and a > 0:
            flush(n_kv + a // per_tile - 1, None)
        p = jnp.exp2(s_a - cat([m0[:, lo:tq], m0[:, tq + lo:]]))
        r0 = (a % per_tile) * LANES
        p_ref[r0:r0 + LANES, :] = widen(p.astype(BF16), lo, w, 0.0)
        l0 = l0 + widen(jnp.sum(p.reshape(LANES // 8, 8, 2 * w), axis=0), lo, w, 0.0)
    l_ref[...] = l0

    def frozen_step(j, j_pending):
        s_j = scores(j)
        flush(j_pending, None)
        l_ref[...] += exp_store(s_j, None)

    def frozen_steps(j0, n):
        for u in range(n):
            frozen_step(j0 + u, pending(j0) if u == 0 else j0 + u - 1)

    def frozen_group(t, carry):
        frozen_steps(ATT_UNROLL * t, ATT_UNROLL)
        return carry

    def online_body(j, carry):
        s_j = scores(j)
        m_old = m_ref[...]
        m_new = jnp.maximum(m_old, jnp.max(s_j, axis=0, keepdims=True))
        alpha = jnp.exp2(m_old - m_new)
        flush(pending(j), alpha)
        m_ref[...] = m_new
        l_ref[...] = l_ref[...] * alpha + exp_store(s_j, m_new)
        return carry

    @pl.when(frozen_ok)
    def _():
        groups = lax.shift_right_logical(n_kv, ATT_UNROLL.bit_length() - 1)
        lax.fori_loop(0, groups, frozen_group, 0)
        j0 = groups * ATT_UNROLL
        n = ATT_UNROLL // 2
        while n >= n_diag:
            @pl.when((n_kv & n) != 0)
            def _(j0=j0, n=n):
                frozen_steps(j0, n)
            j0 = j0 + (n_kv & n)
            n //= 2

    @pl.when(jnp.logical_not(frozen_ok))
    def _():
        lax.fori_loop(0, n_kv, online_body, 0)

    flush(pending(n_kv), None)

    lam = lam_ref[0, 0]
    l = jnp.sum(l_ref[...], axis=0, keepdims=True)
    o_t = acc_ref[0] / l[:, :tq] - lam * (acc_ref[1] / l[:, tq:])
    o_n = o_t * lax.rsqrt(jnp.mean(o_t * o_t, axis=0, keepdims=True) + SUBLN_EPS) * sw_ref[...]
    o_n = o_n * (1.0 - lam_init)
    o_ref[cols, :] = (o_n * z_ref[:, cols].astype(F32)).T.astype(o_ref.dtype)


def _attn_call(lam, q_t, k_nat, v_t, z_t, ksq, sw, lam_init):
    dv = 2 * DIFF_D
    s = k_nat.shape[0]
    h = k_nat.shape[1] // dv
    tq = ATT_TQ
    tg = ATT_SUB * tq
    assert ATT_TQ % ATT_TK == 0 and ATT_UNROLL >= ATT_TQ // ATT_TK
    return pl.pallas_call(
        functools.partial(_attn_kernel, lam_init=lam_init),
        grid=(h, s // tg),
        in_specs=[
            pl.BlockSpec(memory_space=pltpu.SMEM),
            pl.BlockSpec((dv, tg), lambda hh, i: (hh, i)),
            pl.BlockSpec((s, dv), lambda hh, i: (0, hh)),
            pl.BlockSpec((dv, s), lambda hh, i: (hh, 0)),
            pl.BlockSpec((dv, tg), lambda hh, i: (hh, i)),
            pl.BlockSpec((1, 2, s), lambda hh, i: (hh, 0, 0)),
            pl.BlockSpec((dv, tq), lambda hh, i: (0, 0)),
        ],
        out_specs=pl.BlockSpec((tg, dv), lambda hh, i: (i, hh)),
        out_shape=jax.ShapeDtypeStruct((s, h * dv), BF16),
        scratch_shapes=[
            pltpu.VMEM((ATT_SUB, 2, dv, tq), F32),
            pltpu.VMEM((ATT_SUB, 8, 2 * tq), F32),
            pltpu.VMEM((1, 2 * tq), F32),
            pltpu.VMEM((ATT_TK, 2 * tq), BF16),
        ],
        compiler_params=pltpu.CompilerParams(
            dimension_semantics=("arbitrary", "arbitrary"), vmem_limit_bytes=VMEM_LIMIT),
        name="attn",
    )(lam, q_t, k_nat, v_t, z_t, ksq, sw)


def _layer(l, x2, rope, w_norm, w_in, conv_w, a_log, dt_bias, gdn_norm_w, q_norm_w,
           k_norm_w, lambda_q1, lambda_k1, lambda_q2, lambda_k2, subln_w, w_out):
    s, d = x2.shape
    nqk = GDN_HEADS * GDN_D
    ba_lo = 4 * nqk
    ba_hi = ba_lo + 2 * GDN_HEADS
    wt_a = w_in[:, :ba_hi].T.astype(BF16)
    wt_b = w_in[:, ba_hi:].T.astype(BF16)
    qw = jnp.broadcast_to(q_norm_w[:, None], (DIFF_D, PROJ_TM)).astype(F32)
    kw = jnp.broadcast_to(k_norm_w[:, None], (DIFF_D, PROJ_TM)).astype(F32)

    cw = jnp.broadcast_to(conv_w.astype(F32)[:, :, None], (CONV_K, 3 * nqk, LANES))
    gdn_t, ba_t, qb_t, kb, ksq, vb_t, zb_t = _proj_call(
        x2, w_norm[None, :].astype(F32), wt_a, wt_b, rope, qw, kw, cw)

    zeros4 = jnp.zeros((GDN_HEADS, LANES), F32)
    alog8 = jnp.concatenate([zeros4, jnp.broadcast_to(a_log.astype(F32)[:, None], (GDN_HEADS, LANES))], 0)
    dt8 = jnp.concatenate([zeros4, jnp.broadcast_to(dt_bias.astype(F32)[:, None], (GDN_HEADS, LANES))], 0)
    gw = jnp.broadcast_to(gdn_norm_w.astype(F32)[:, None], (GDN_D, LANES))

    lam_init = 0.8 - 0.6 * math.exp(-0.3 * l)
    lam = (jnp.exp(jnp.sum(lambda_q1.astype(F32) * lambda_k1.astype(F32)))
           - jnp.exp(jnp.sum(lambda_q2.astype(F32) * lambda_k2.astype(F32))) + lam_init)
    dv = 2 * DIFF_D
    sw = jnp.broadcast_to(subln_w.astype(F32)[:, None], (dv, ATT_TQ))
    ob = _attn_call(lam.reshape(1, 1).astype(F32), qb_t, kb, vb_t, zb_t, ksq, sw, lam_init)

    return _gdn_call(gdn_t, ba_t, alog8, dt8, gw, x2, ob, w_out.astype(BF16))


def kernel(x, w_norm, w_in, conv_w, a_log, dt_bias, gdn_norm_w, q_norm_w, k_norm_w,
           lambda_q1, lambda_k1, lambda_q2, lambda_k2, subln_w, w_out):
    b, s, d = x.shape
    assert b == 1
    inv_freq = ROPE_THETA ** (-jnp.arange(0, DIFF_D, 2, dtype=jnp.float32) / DIFF_D)
    per_tile = PROJ_TM // LANES
    coarse = jnp.arange(0, s, LANES, dtype=jnp.float32)[:, None] * inv_freq[None, :]
    coarse = coarse.reshape(s // PROJ_TM, per_tile, DIFF_D // 2).transpose(0, 2, 1)
    fine = inv_freq[:, None] * jnp.arange(LANES, dtype=jnp.float32)[None, :]
    rope = (jnp.cos(coarse), jnp.sin(coarse), jnp.cos(fine), jnp.sin(fine))
    x2 = x[0]
    for l in range(w_norm.shape[0]):
        x2 = _layer(l, x2, rope, w_norm[l], w_in[l], conv_w[l], a_log[l], dt_bias[l],
                    gdn_norm_w[l], q_norm_w[l], k_norm_w[l], lambda_q1[l], lambda_k1[l],
                    lambda_q2[l], lambda_k2[l], subln_w[l], w_out[l])
    return x2[None]
```

```python
import functools
import math

import jax
import jax.numpy as jnp
from jax import lax
from jax.experimental import pallas as pl
from jax.experimental.pallas import tpu as pltpu

F32 = jnp.float32
BF16 = jnp.bfloat16

GDN_HEADS = 4
GDN_D = 128
CONV_K = 4
DIFF_HEADS = 4
DIFF_D = 64
ROPE_THETA = 10000.0
EPS = 1e-6
SUBLN_EPS = 1e-5

HEAD_GROUP_ROWS = GDN_HEADS * GDN_D
assert HEAD_GROUP_ROWS == DIFF_HEADS * 2 * DIFF_D

LANES = 128
CHUNK = LANES
NEG_BIG = -1e30
LOG2E = 1.4426950408889634
FROZEN_MAX_BOUND = 40.0
NORM_SLACK = 1.01

PROJ_TM = 1024
GDN_TB = 1024
ATT_TQ = 1024
ATT_TK = 512
ATT_UNROLL = 4
ATT_SUB = 2
OUT_PIECE = 256
VMEM_LIMIT = 56 * 1024 * 1024

NT_DIMS = (((1,), (1,)), ((), ()))
TN_DIMS = (((0,), (0,)), ((), ()))


def _sigmoid(v):
    return 1.0 / (1.0 + jnp.exp(-v))


def _silu(v):
    h = 0.5 * v
    return h * jnp.tanh(h) + h


def _proj_kernel(x_ref, wn_ref, wa_ref, wb_ref, ca_ref, sa_ref, cb_ref, sb_ref,
                 qw_ref, kw_ref, cw_ref,
                 gdn_ref, ba_ref, qb_ref, kb_ref, ks_ref, vb_ref, zb_ref, halo_ref):
    @pl.when(pl.program_id(0) == 0)
    def _():
        halo_ref[...] = jnp.zeros_like(halo_ref)

    x = x_ref[...]
    ms = jnp.mean(x * x, axis=-1, keepdims=True)
    hn = (x * lax.rsqrt(ms + EPS) * wn_ref[...]).astype(BF16)

    blk = HEAD_GROUP_ROWS
    n_group_a = 4 * blk
    n_gates = wa_ref.shape[0] - n_group_a

    def proj_t(lo, n):
        w = wa_ref[lo:lo + n, :] if lo < n_group_a else wb_ref[lo - n_group_a:lo - n_group_a + n, :]
        return lax.dot_general(w, hn, NT_DIMS, preferred_element_type=F32)

    tm = x.shape[0]
    lane_t = lax.broadcasted_iota(jnp.int32, (blk, LANES), 1)

    def conv_block(b):
        rows = slice(b * blk, (b + 1) * blk)
        raw = proj_t(b * blk, blk)
        tail = halo_ref[rows, :]
        halo_ref[rows, :] = raw[:, tm - LANES:]
        y = jnp.concatenate([cw_ref[CONV_K - 1, rows, :]] * (tm // LANES), axis=1) * raw
        for j in range(1, CONV_K):
            rolled = pltpu.roll(raw, j, axis=1)
            first = jnp.where(lane_t >= j, rolled[:, :LANES], pltpu.roll(tail, j, axis=1))
            shifted = jnp.concatenate([first, rolled[:, LANES:]], axis=1)
            y = y + jnp.concatenate([cw_ref[CONV_K - 1 - j, rows, :]] * (tm // LANES), axis=1) * shifted
        gdn_ref[rows, :] = _silu(y).astype(BF16)

    q_raw = proj_t(4 * blk, blk)
    k_raw = proj_t(5 * blk, blk)

    ca, sa, cb, sb = ca_ref[0], sa_ref[0], cb_ref[...], sb_ref[...]
    cos = jnp.concatenate([ca[:, c:c + 1] * cb - sa[:, c:c + 1] * sb for c in range(tm // LANES)], axis=1)
    sin = jnp.concatenate([sa[:, c:c + 1] * cb + ca[:, c:c + 1] * sb for c in range(tm // LANES)], axis=1)
    half = DIFF_D // 2

    def norm_rope(t, w, scale):
        w1, w2 = w[:half, :] * scale, w[half:, :] * scale
        c1, s1, c2, s2 = w1 * cos, w1 * sin, w2 * cos, w2 * sin
        outs = []
        for g in range(t.shape[0] // DIFF_D):
            tg = t[g * DIFF_D:(g + 1) * DIFF_D, :]
            r = lax.rsqrt(jnp.mean(tg * tg, axis=0, keepdims=True) + EPS)
            t1, t2 = tg[:half, :], tg[half:, :]
            outs.append((t1 * c1 - t2 * s2) * r)
            outs.append((t2 * c2 + t1 * s1) * r)
        return jnp.concatenate(outs, axis=0)

    qb_ref[...] = norm_rope(q_raw, qw_ref[...], LOG2E / math.sqrt(DIFF_D)).astype(BF16)
    kt = norm_rope(k_raw, kw_ref[...], 1.0)
    for g in range(2 * DIFF_HEADS):
        kg = kt[g * DIFF_D:(g + 1) * DIFF_D, :]
        ks_ref[g // 2, g % 2:g % 2 + 1, :] = jnp.sum(kg * kg, axis=0, keepdims=True)
    kb_ref[...] = kt.T.astype(BF16)

    for b in range(3):
        conv_block(b)
    zg = proj_t(3 * blk, blk + n_gates)
    gdn_ref[3 * blk:4 * blk, :] = _silu(zg[:blk]).astype(BF16)
    ba_ref[...] = zg[blk:]
    zb_ref[...] = _silu(proj_t(7 * blk, blk)).astype(BF16)
    vb_ref[...] = proj_t(6 * blk, blk).astype(BF16)


def _proj_call(x2, w_norm, wt_a, wt_b, rope, qw, kw, cw):
    s, d = x2.shape
    tm = PROJ_TM
    blk = HEAD_GROUP_ROWS
    n_gates = wt_a.shape[0] - 4 * blk
    cos_a, sin_a, cos_b, sin_b = rope
    const = lambda shape: pl.BlockSpec(shape, lambda i: (0,) * len(shape))
    cols = lambda rows: pl.BlockSpec((rows, tm), lambda i: (0, i))
    coarse = pl.BlockSpec((1,) + cos_a.shape[1:], lambda i: (i, 0, 0))
    return pl.pallas_call(
        _proj_kernel,
        grid=(s // tm,),
        in_specs=[
            pl.BlockSpec((tm, d), lambda i: (i, 0)),
            const((1, d)),
            const(wt_a.shape),
            const(wt_b.shape),
            coarse,
            coarse,
            const(cos_b.shape),
            const(sin_b.shape),
            const((DIFF_D, tm)),
            const((DIFF_D, tm)),
            const(cw.shape),
        ],
        out_specs=[
            cols(4 * blk),
            cols(n_gates),
            cols(blk),
            pl.BlockSpec((tm, blk), lambda i: (i, 0)),
            pl.BlockSpec((DIFF_HEADS, 2, tm), lambda i: (0, 0, i)),
            cols(blk),
            cols(blk),
        ],
        out_shape=[
            jax.ShapeDtypeStruct((4 * blk, s), BF16),
            jax.ShapeDtypeStruct((n_gates, s), F32),
            jax.ShapeDtypeStruct((blk, s), BF16),
            jax.ShapeDtypeStruct((s, blk), BF16),
            jax.ShapeDtypeStruct((DIFF_HEADS, 2, s), F32),
            jax.ShapeDtypeStruct((blk, s), BF16),
            jax.ShapeDtypeStruct((blk, s), BF16),
        ],
        scratch_shapes=[
            pltpu.VMEM((cw.shape[1], LANES), F32),
        ],
        compiler_params=pltpu.CompilerParams(
            dimension_semantics=("arbitrary",), vmem_limit_bytes=VMEM_LIMIT),
        name="proj",
    )(x2, w_norm, wt_a, wt_b, cos_a, sin_a, cos_b, sin_b, qw, kw, cw)


def _gdn_kernel(g_ref, ba_ref, alog_ref, dt_ref, gw_ref, x_ref, ob_ref, wo_ref,
                y_ref, st_ref):
    nqkv = 3 * GDN_HEADS * GDN_D
    n_chunks = g_ref.shape[1] // CHUNK
    n_a = GDN_HEADS * GDN_D

    @pl.when(pl.program_id(0) == 0)
    def _():
        st_ref[...] = jnp.zeros_like(st_ref)

    for lo in range(0, y_ref.shape[1], OUT_PIECE):
        _out_b_piece(x_ref, ob_ref, wo_ref, y_ref, n_a, lo)

    lane = lax.broadcasted_iota(jnp.int32, (CHUNK, CHUNK), 1)
    subl = lax.broadcasted_iota(jnp.int32, (CHUNK, CHUNK), 0)
    eye = (lane == subl).astype(F32)
    lane8 = lax.broadcasted_iota(jnp.int32, (8, CHUNK), 1)

    chains = []
    for c in range(n_chunks):
        lanes = slice(c * CHUNK, (c + 1) * CHUNK)
        y = g_ref[0:nqkv, lanes].astype(F32)

        ba = ba_ref[:, lanes]
        beta8 = _sigmoid(ba)
        sp = ba + dt_ref[...]
        softplus = jnp.maximum(sp, 0.0) + jnp.log(1.0 + jnp.exp(-jnp.abs(sp)))
        gc8 = (-LOG2E) * jnp.exp(alog_ref[...]) * softplus
        sh = 1
        while sh < CHUNK:
            gc8 = gc8 + jnp.where(lane8 >= sh, pltpu.roll(gc8, sh, axis=1), 0.0)
            sh *= 2

        for h in range(GDN_HEADS):
            qt = y[h * GDN_D:(h + 1) * GDN_D, :]
            kt = y[(GDN_HEADS + h) * GDN_D:(GDN_HEADS + h + 1) * GDN_D, :]
            vt = y[(2 * GDN_HEADS + h) * GDN_D:(2 * GDN_HEADS + h + 1) * GDN_D, :]
            qt = qt * lax.rsqrt(jnp.sum(qt * qt, axis=0, keepdims=True) + EPS) * (GDN_D ** -0.5)
            kt = kt * lax.rsqrt(jnp.sum(kt * kt, axis=0, keepdims=True) + EPS)
            beta = beta8[h:h + 1, :]
            gc = gc8[GDN_HEADS + h:GDN_HEADS + h + 1, :]
            g_row = jnp.broadcast_to(gc, (CHUNK, CHUNK))
            g_col = g_row.T
            g_last = g_col[CHUNK - 1:CHUNK, :]
            e_gc = jnp.exp2(gc)
            chains.append(dict(
                c=c, h=h, qt=qt, kt=kt, beta=beta, g_last=g_last, g_col=g_col,
                decay_t=jnp.exp2(jnp.where(lane >= subl, g_row - g_col, NEG_BIG)),
                k_nat=kt.T,
                qg=qt * e_gc,
                rhs=jnp.concatenate([vt * beta, kt * (beta * e_gc)], axis=0).astype(BF16)))

    for ch in chains:
        gram = jnp.dot(ch["k_nat"].astype(BF16),
                       jnp.concatenate([ch["kt"], ch["qt"]], axis=1).astype(BF16),
                       preferred_element_type=F32)
        ch["a_t"] = gram[:, CHUNK:] * ch["decay_t"]
        ch["n"] = jnp.where(lane > subl, gram[:, :CHUNK] * ch["decay_t"] * (-ch["beta"]), 0.0)

    for ch in chains:
        nb = ch["n"].astype(BF16)
        ch["q"] = eye + ch["n"]
        ch["n"] = jnp.dot(nb, nb, preferred_element_type=F32)
    m = 2
    while 2 * m < CHUNK:
        for ch in chains:
            nb = ch["n"].astype(BF16)
            both = jnp.dot(jnp.concatenate([ch["q"].astype(BF16), nb], axis=0), nb,
                           preferred_element_type=F32)
            ch["q"] = ch["q"] + both[:CHUNK]
            ch["n"] = both[CHUNK:]
        m *= 2
    for ch in chains:
        ch["q"] = ch["q"] + jnp.dot(ch["q"].astype(BF16), ch["n"].astype(BF16),
                                    preferred_element_type=F32)

    for ch in chains:
        uw = jnp.dot(ch["rhs"], ch["q"].astype(BF16), preferred_element_type=F32)
        ch["u_t"] = uw[:GDN_D, :]
        ch["r1"] = jnp.concatenate([uw[GDN_D:, :], ch["qg"]], axis=1).astype(BF16)
        k_dec = ch["k_nat"] * jnp.exp2(ch["g_last"] - ch["g_col"])
        ch["r2"] = jnp.concatenate([k_dec, ch["a_t"]], axis=1).astype(BF16)

    def out_a_piece(c, oa_chunk):
        rows = slice(c * CHUNK, (c + 1) * CHUNK)
        y_ref[rows, :] += lax.dot_general(oa_chunk, wo_ref[0:n_a, :], TN_DIMS,
                                          preferred_element_type=F32)

    states = [st_ref[h] for h in range(GDN_HEADS)]
    ready = None
    for c in range(n_chunks):
        lanes = slice(c * CHUNK, (c + 1) * CHUNK)
        row = chains[c * GDN_HEADS:(c + 1) * GDN_HEADS]
        x1 = [jnp.dot(states[h].astype(BF16), row[h]["r1"], preferred_element_type=F32)
              for h in range(GDN_HEADS)]
        if ready is not None:
            out_a_piece(*ready)
        x2 = [jnp.dot((row[h]["u_t"] - x1[h][:, :CHUNK]).astype(BF16), row[h]["r2"],
                      preferred_element_type=F32) for h in range(GDN_HEADS)]
        oa = []
        for h in range(GDN_HEADS):
            states[h] = states[h] * jnp.exp2(row[h]["g_last"]) + x2[h][:, :GDN_D]
            o_t = x1[h][:, CHUNK:] + x2[h][:, GDN_D:]
            o_n = o_t * lax.rsqrt(jnp.mean(o_t * o_t, axis=0, keepdims=True) + EPS) * gw_ref[...]
            gate = g_ref[nqkv + h * GDN_D:nqkv + (h + 1) * GDN_D, lanes].astype(F32)
            oa.append((o_n * gate).astype(BF16))
        ready = (c, jnp.concatenate(oa, axis=0))
    out_a_piece(*ready)
    for h in range(GDN_HEADS):
        st_ref[h] = states[h]


def _out_b_piece(x_ref, ob_ref, wo_ref, y_ref, n_a, lo):
    cols = slice(lo, lo + OUT_PIECE)
    y_ref[:, cols] = x_ref[:, cols] + jnp.dot(
        ob_ref[...], wo_ref[n_a:, cols], preferred_element_type=F32)


def _gdn_call(gdn_t, ba_t, alog8, dt8, gw, x2, ob, w_out):
    s, d = x2.shape
    tb = GDN_TB
    const = lambda shape: pl.BlockSpec(shape, lambda i: (0,) * len(shape))
    return pl.pallas_call(
        _gdn_kernel,
        grid=(s // tb,),
        in_specs=[
            pl.BlockSpec((gdn_t.shape[0], tb), lambda i: (0, i)),
            pl.BlockSpec((8, tb), lambda i: (0, i)),
            const(alog8.shape),
            const(dt8.shape),
            const(gw.shape),
            pl.BlockSpec((tb, d), lambda i: (i, 0)),
            pl.BlockSpec((tb, ob.shape[1]), lambda i: (i, 0)),
            const(w_out.shape),
        ],
        out_specs=pl.BlockSpec((tb, d), lambda i: (i, 0)),
        out_shape=jax.ShapeDtypeStruct((s, d), F32),
        scratch_shapes=[
            pltpu.VMEM((GDN_HEADS, GDN_D, GDN_D), F32),
        ],
        compiler_params=pltpu.CompilerParams(
            dimension_semantics=("arbitrary",), vmem_limit_bytes=VMEM_LIMIT),
        name="gdn",
    )(gdn_t, ba_t, alog8, dt8, gw, x2, ob, w_out)


def _attn_kernel(lam_ref, q_ref, k_ref, v_ref, z_ref, ks_ref, sw_ref, o_ref,
                 acc_ref, l_ref, m_ref, p_ref, *, lam_init):
    for sub in range(ATT_SUB):
        _attn_tile(sub, lam_ref, q_ref, k_ref, v_ref, z_ref, ks_ref, sw_ref, o_ref,
                   acc_ref.at[sub], l_ref.at[sub], m_ref, p_ref, lam_init)


def _attn_tile(sub, lam_ref, q_ref, k_ref, v_ref, z_ref, ks_ref, sw_ref, o_ref,
               acc_ref, l_ref, m_ref, p_ref, lam_init):
    tq, tk = ATT_TQ, ATT_TK
    dv = 2 * DIFF_D
    cols = slice(sub * tq, (sub + 1) * tq)
    i = pl.program_id(1) * ATT_SUB + sub
    qt = q_ref[:, cols]
    row = lax.broadcasted_iota(jnp.int32, qt.shape, 0)
    zero = jnp.zeros_like(qt)
    q_cat = jnp.concatenate(
        [jnp.where(row < DIFF_D, qt, zero), jnp.where(row >= DIFF_D, qt, zero)], axis=1)

    def scores(j):
        start = pl.multiple_of(j * tk, tk)
        return jnp.dot(k_ref[pl.ds(start, tk), :], q_cat, preferred_element_type=F32)

    def flush(j_pending, alpha):
        start = pl.multiple_of(j_pending * tk, tk)
        vt = v_ref[:, pl.ds(start, tk)]
        for c in range(2):
            upd = acc_ref[c] + jnp.dot(vt, p_ref[:, c * tq:(c + 1) * tq],
                                       preferred_element_type=F32)
            acc_ref[c] = upd if alpha is None else upd * alpha[:, c * tq:(c + 1) * tq]

    def exp_store(s_j, m):
        p = jnp.exp2(s_j if m is None else s_j - m)
        p_ref[...] = p.astype(BF16)
        return jnp.sum(p.reshape(tk // 8, 8, 2 * tq), axis=0)

    n_diag = tq // tk
    n_kv = i * n_diag

    def pending(j):
        return jnp.where(j == 0, n_kv + n_diag - 1, j - 1)

    qf = qt.astype(F32)
    qq = qf * qf
    qsq = jnp.concatenate([jnp.sum(qq[:DIFF_D], axis=0, keepdims=True),
                           jnp.sum(qq[DIFF_D:], axis=0, keepdims=True)], axis=1)
    kmax = jnp.max(ks_ref[0], axis=1, keepdims=True)
    kmax = jnp.concatenate([jnp.broadcast_to(kmax[0:1], (1, tq)),
                            jnp.broadcast_to(kmax[1:2], (1, tq))], axis=1)
    frozen_ok = jnp.max(jnp.sqrt(qsq * kmax)) * NORM_SLACK <= FROZEN_MAX_BOUND

    tri = (lax.broadcasted_iota(jnp.int32, (LANES, LANES), 0)
           <= lax.broadcasted_iota(jnp.int32, (LANES, LANES), 1))
    def cat(parts):
        parts = [t for t in parts if t.shape[1] > 0]
        return parts[0] if len(parts) == 1 else jnp.concatenate(parts, axis=1)

    def widen(t, lo, w, fill):
        pad = jnp.full((t.shape[0], lo), fill, t.dtype)
        return cat([pad, t[:, :w], pad, t[:, w:]])

    strips = []
    m0 = jnp.full((1, 2 * tq), NEG_BIG, F32)
    for a in range(tq // LANES):
        lo, w = a * LANES, tq - a * LANES
        start = pl.multiple_of(i * tq + lo, LANES)
        s_a = jnp.dot(k_ref[pl.ds(start, LANES), :], cat([q_cat[:, lo:tq], q_cat[:, tq + lo:]]),
                      preferred_element_type=F32)
        s_a = cat([jnp.where(tri, s_a[:, :LANES], NEG_BIG), s_a[:, LANES:w],
                   jnp.where(tri, s_a[:, w:w + LANES], NEG_BIG), s_a[:, w + LANES:]])
        strips.append(s_a)
        m0 = jnp.maximum(m0, widen(jnp.max(s_a, axis=0, keepdims=True), lo, w, NEG_BIG))
    m0 = jnp.where(frozen_ok, 0.0, m0)
    m_ref[...] = m0
    acc_ref[...] = jnp.zeros_like(acc_ref)
    l0 = jnp.zeros((8, 2 * tq), F32)
    per_tile = tk // LANES
    for a, s_a in enumerate(strips):
        lo, w = a * LANES, tq - a * LANES
        if a % per_tile == 0 and a > 0:
            flush(n_kv + a // per_tile - 1, None)
        p = jnp.exp2(s_a - cat([m0[:, lo:tq], m0[:, tq + lo:]]))
        r0 = (a % per_tile) * LANES
        p_ref[r0:r0 + LANES, :] = widen(p.astype(BF16), lo, w, 0.0)
        l0 = l0 + widen(jnp.sum(p.reshape(LANES // 8, 8, 2 * w), axis=0), lo, w, 0.0)
    l_ref[...] = l0

    def frozen_step(j, j_pending):
        s_j = scores(j)
        flush(j_pending, None)
        l_ref[...] += exp_store(s_j, None)

    def frozen_steps(j0, n):
        for u in range(n):
            frozen_step(j0 + u, pending(j0) if u == 0 else j0 + u - 1)

    def frozen_group(t, carry):
        frozen_steps(ATT_UNROLL * t, ATT_UNROLL)
        return carry

    def online_body(j, carry):
        s_j = scores(j)
        m_old = m_ref[...]
        m_new = jnp.maximum(m_old, jnp.max(s_j, axis=0, keepdims=True))
        alpha = jnp.exp2(m_old - m_new)
        flush(pending(j), alpha)
        m_ref[...] = m_new
        l_ref[...] = l_ref[...] * alpha + exp_store(s_j, m_new)
        return carry

    @pl.when(frozen_ok)
    def _():
        groups = lax.shift_right_logical(n_kv, ATT_UNROLL.bit_length() - 1)
        lax.fori_loop(0, groups, frozen_group, 0)
        j0 = groups * ATT_UNROLL
        n = ATT_UNROLL // 2
        while n >= n_diag:
            @pl.when((n_kv & n) != 0)
            def _(j0=j0, n=n):
                frozen_steps(j0, n)
            j0 = j0 + (n_kv & n)
            n //= 2

    @pl.when(jnp.logical_not(frozen_ok))
    def _():
        lax.fori_loop(0, n_kv, online_body, 0)

    flush(pending(n_kv), None)

    lam = lam_ref[0, 0]
    l = jnp.sum(l_ref[...], axis=0, keepdims=True)
    o_t = acc_ref[0] / l[:, :tq] - lam * (acc_ref[1] / l[:, tq:])
    o_n = o_t * lax.rsqrt(jnp.mean(o_t * o_t, axis=0, keepdims=True) + SUBLN_EPS) * sw_ref[...]
    o_n = o_n * (1.0 - lam_init)
    o_ref[cols, :] = (o_n * z_ref[:, cols].astype(F32)).T.astype(o_ref.dtype)


def _attn_call(lam, q_t, k_nat, v_t, z_t, ksq, sw, lam_init):
    dv = 2 * DIFF_D
    s = k_nat.shape[0]
    h = k_nat.shape[1] // dv
    tq = ATT_TQ
    tg = ATT_SUB * tq
    assert ATT_TQ % ATT_TK == 0 and ATT_UNROLL >= ATT_TQ // ATT_TK
    return pl.pallas_call(
        functools.partial(_attn_kernel, lam_init=lam_init),
        grid=(h, s // tg),
        in_specs=[
            pl.BlockSpec(memory_space=pltpu.SMEM),
            pl.BlockSpec((dv, tg), lambda hh, i: (hh, i)),
            pl.BlockSpec((s, dv), lambda hh, i: (0, hh)),
            pl.BlockSpec((dv, s), lambda hh, i: (hh, 0)),
            pl.BlockSpec((dv, tg), lambda hh, i: (hh, i)),
            pl.BlockSpec((1, 2, s), lambda hh, i: (hh, 0, 0)),
            pl.BlockSpec((dv, tq), lambda hh, i: (0, 0)),
        ],
        out_specs=pl.BlockSpec((tg, dv), lambda hh, i: (i, hh)),
        out_shape=jax.ShapeDtypeStruct((s, h * dv), BF16),
        scratch_shapes=[
            pltpu.VMEM((ATT_SUB, 2, dv, tq), F32),
            pltpu.VMEM((ATT_SUB, 8, 2 * tq), F32),
            pltpu.VMEM((1, 2 * tq), F32),
            pltpu.VMEM((ATT_TK, 2 * tq), BF16),
        ],
        compiler_params=pltpu.CompilerParams(
            dimension_semantics=("arbitrary", "arbitrary"), vmem_limit_bytes=VMEM_LIMIT),
        name="attn",
    )(lam, q_t, k_nat, v_t, z_t, ksq, sw)


def _layer(l, x2, rope, w_norm, w_in, conv_w, a_log, dt_bias, gdn_norm_w, q_norm_w,
           k_norm_w, lambda_q1, lambda_k1, lambda_q2, lambda_k2, subln_w, w_out):
    s, d = x2.shape
    nqk = GDN_HEADS * GDN_D
    ba_lo = 4 * nqk
    ba_hi = ba_lo + 2 * GDN_HEADS
    wt_a = w_in[:, :ba_hi].T.astype(BF16)
    wt_b = w_in[:, ba_hi:].T.astype(BF16)
    qw = jnp.broadcast_to(q_norm_w[:, None], (DIFF_D, PROJ_TM)).astype(F32)
    kw = jnp.broadcast_to(k_norm_w[:, None], (DIFF_D, PROJ_TM)).astype(F32)

    cw = jnp.broadcast_to(conv_w.astype(F32)[:, :, None], (CONV_K, 3 * nqk, LANES))
    gdn_t, ba_t, qb_t, kb, ksq, vb_t, zb_t = _proj_call(
        x2, w_norm[None, :].astype(F32), wt_a, wt_b, rope, qw, kw, cw)

    zeros4 = jnp.zeros((GDN_HEADS, LANES), F32)
    alog8 = jnp.concatenate([zeros4, jnp.broadcast_to(a_log.astype(F32)[:, None], (GDN_HEADS, LANES))], 0)
    dt8 = jnp.concatenate([zeros4, jnp.broadcast_to(dt_bias.astype(F32)[:, None], (GDN_HEADS, LANES))], 0)
    gw = jnp.broadcast_to(gdn_norm_w.astype(F32)[:, None], (GDN_D, LANES))

    lam_init = 0.8 - 0.6 * math.exp(-0.3 * l)
    lam = (jnp.exp(jnp.sum(lambda_q1.astype(F32) * lambda_k1.astype(F32)))
           - jnp.exp(jnp.sum(lambda_q2.astype(F32) * lambda_k2.astype(F32))) + lam_init)
    dv = 2 * DIFF_D
    sw = jnp.broadcast_to(subln_w.astype(F32)[:, None], (dv, ATT_TQ))
    ob = _attn_call(lam.reshape(1, 1).astype(F32), qb_t, kb, vb_t, zb_t, ksq, sw, lam_init)

    return _gdn_call(gdn_t, ba_t, alog8, dt8, gw, x2, ob, w_out.astype(BF16))


def kernel(x, w_norm, w_in, conv_w, a_log, dt_bias, gdn_norm_w, q_norm_w, k_norm_w,
           lambda_q1, lambda_k1, lambda_q2, lambda_k2, subln_w, w_out):
    b, s, d = x.shape
    assert b == 1
    inv_freq = ROPE_THETA ** (-jnp.arange(0, DIFF_D, 2, dtype=jnp.float32) / DIFF_D)
    per_tile = PROJ_TM // LANES
    coarse = jnp.arange(0, s, LANES, dtype=jnp.float32)[:, None] * inv_freq[None, :]
    coarse = coarse.reshape(s // PROJ_TM, per_tile, DIFF_D // 2).transpose(0, 2, 1)
    fine = inv_freq[:, None] * jnp.arange(LANES, dtype=jnp.float32)[None, :]
    rope = (jnp.cos(coarse), jnp.sin(coarse), jnp.cos(fine), jnp.sin(fine))
    x2 = x[0]
    for l in range(w_norm.shape[0]):
        x2 = _layer(l, x2, rope, w_norm[l], w_in[l], conv_w[l], a_log[l], dt_bias[l],
                    gdn_norm_w[l], q_norm_w[l], k_norm_w[l], lambda_q1[l], lambda_k1[l],
                    lambda_q2[l], lambda_k2[l], subln_w[l], w_out[l])
    return x2[None]
```

```python
import functools
import math

import jax
import jax.numpy as jnp
from jax import lax
from jax.experimental import pallas as pl
from jax.experimental.pallas import tpu as pltpu

F32 = jnp.float32
BF16 = jnp.bfloat16

GDN_HEADS = 4
GDN_D = 128
CONV_K = 4
DIFF_HEADS = 4
DIFF_D = 64
ROPE_THETA = 10000.0
EPS = 1e-6
SUBLN_EPS = 1e-5

HEAD_GROUP_ROWS = GDN_HEADS * GDN_D
assert HEAD_GROUP_ROWS == DIFF_HEADS * 2 * DIFF_D

LANES = 128
CHUNK = LANES
NEG_BIG = -1e30
LOG2E = 1.4426950408889634
FROZEN_MAX_BOUND = 40.0
NORM_SLACK = 1.01

PROJ_TM = 1024
GDN_TB = 1024
ATT_TQ = 1024
ATT_TK = 512
ATT_UNROLL = 4
ATT_SUB = 2
OUT_PIECE = 256
VMEM_LIMIT = 56 * 1024 * 1024

NT_DIMS = (((1,), (1,)), ((), ()))
TN_DIMS = (((0,), (0,)), ((), ()))


def _sigmoid(v):
    return 1.0 / (1.0 + jnp.exp(-v))


def _silu(v):
    h = 0.5 * v
    return h * jnp.tanh(h) + h


def _proj_kernel(x_ref, wn_ref, wa_ref, wb_ref, ca_ref, sa_ref, cb_ref, sb_ref,
                 qw_ref, kw_ref, cw_ref,
                 gdn_ref, ba_ref, qb_ref, kb_ref, ks_ref, vb_ref, zb_ref, halo_ref):
    @pl.when(pl.program_id(0) == 0)
    def _():
        halo_ref[...] = jnp.zeros_like(halo_ref)

    x = x_ref[...]
    ms = jnp.mean(x * x, axis=-1, keepdims=True)
    hn = (x * lax.rsqrt(ms + EPS) * wn_ref[...]).astype(BF16)

    blk = HEAD_GROUP_ROWS
    n_group_a = 4 * blk
    n_gates = wa_ref.shape[0] - n_group_a

    def proj_t(lo, n):
        w = wa_ref[lo:lo + n, :] if lo < n_group_a else wb_ref[lo - n_group_a:lo - n_group_a + n, :]
        return lax.dot_general(w, hn, NT_DIMS, preferred_element_type=F32)

    tm = x.shape[0]
    lane_t = lax.broadcasted_iota(jnp.int32, (blk, LANES), 1)

    def conv_block(b):
        rows = slice(b * blk, (b + 1) * blk)
        raw = proj_t(b * blk, blk)
        tail = halo_ref[rows, :]
        halo_ref[rows, :] = raw[:, tm - LANES:]
        y = jnp.concatenate([cw_ref[CONV_K - 1, rows, :]] * (tm // LANES), axis=1) * raw
        for j in range(1, CONV_K):
            rolled = pltpu.roll(raw, j, axis=1)
            first = jnp.where(lane_t >= j, rolled[:, :LANES], pltpu.roll(tail, j, axis=1))
            shifted = jnp.concatenate([first, rolled[:, LANES:]], axis=1)
            y = y + jnp.concatenate([cw_ref[CONV_K - 1 - j, rows, :]] * (tm // LANES), axis=1) * shifted
        gdn_ref[rows, :] = _silu(y).astype(BF16)

    q_raw = proj_t(4 * blk, blk)
    k_raw = proj_t(5 * blk, blk)

    ca, sa, cb, sb = ca_ref[0], sa_ref[0], cb_ref[...], sb_ref[...]
    cos = jnp.concatenate([ca[:, c:c + 1] * cb - sa[:, c:c + 1] * sb for c in range(tm // LANES)], axis=1)
    sin = jnp.concatenate([sa[:, c:c + 1] * cb + ca[:, c:c + 1] * sb for c in range(tm // LANES)], axis=1)
    half = DIFF_D // 2

    def norm_rope(t, w, scale):
        w1, w2 = w[:half, :] * scale, w[half:, :] * scale
        c1, s1, c2, s2 = w1 * cos, w1 * sin, w2 * cos, w2 * sin
        outs = []
        for g in range(t.shape[0] // DIFF_D):
            tg = t[g * DIFF_D:(g + 1) * DIFF_D, :]
            r = lax.rsqrt(jnp.mean(tg * tg, axis=0, keepdims=True) + EPS)
            t1, t2 = tg[:half, :], tg[half:, :]
            outs.append((t1 * c1 - t2 * s2) * r)
            outs.append((t2 * c2 + t1 * s1) * r)
        return jnp.concatenate(outs, axis=0)

    qb_ref[...] = norm_rope(q_raw, qw_ref[...], LOG2E / math.sqrt(DIFF_D)).astype(BF16)
    kt = norm_rope(k_raw, kw_ref[...], 1.0)
    for g in range(2 * DIFF_HEADS):
        kg = kt[g * DIFF_D:(g + 1) * DIFF_D, :]
        ks_ref[g // 2, g % 2:g % 2 + 1, :] = jnp.sum(kg * kg, axis=0, keepdims=True)
    kb_ref[...] = kt.T.astype(BF16)

    for b in range(3):
        conv_block(b)
    zg = proj_t(3 * blk, blk + n_gates)
    gdn_ref[3 * blk:4 * blk, :] = _silu(zg[:blk]).astype(BF16)
    ba_ref[...] = zg[blk:]
    zb_ref[...] = _silu(proj_t(7 * blk, blk)).astype(BF16)
    vb_ref[...] = proj_t(6 * blk, blk).astype(BF16)


def _proj_call(x2, w_norm, wt_a, wt_b, rope, qw, kw, cw):
    s, d = x2.shape
    tm = PROJ_TM
    blk = HEAD_GROUP_ROWS
    n_gates = wt_a.shape[0] - 4 * blk
    cos_a, sin_a, cos_b, sin_b = rope
    const = lambda shape: pl.BlockSpec(shape, lambda i: (0,) * len(shape))
    cols = lambda rows: pl.BlockSpec((rows, tm), lambda i: (0, i))
    coarse = pl.BlockSpec((1,) + cos_a.shape[1:], lambda i: (i, 0, 0))
    return pl.pallas_call(
        _proj_kernel,
        grid=(s // tm,),
        in_specs=[
            pl.BlockSpec((tm, d), lambda i: (i, 0)),
            const((1, d)),
            const(wt_a.shape),
            const(wt_b.shape),
            coarse,
            coarse,
            const(cos_b.shape),
            const(sin_b.shape),
            const((DIFF_D, tm)),
            const((DIFF_D, tm)),
            const(cw.shape),
        ],
        out_specs=[
            cols(4 * blk),
            cols(n_gates),
            cols(blk),
            pl.BlockSpec((tm, blk), lambda i: (i, 0)),
            pl.BlockSpec((DIFF_HEADS, 2, tm), lambda i: (0, 0, i)),
            cols(blk),
            cols(blk),
        ],
        out_shape=[
            jax.ShapeDtypeStruct((4 * blk, s), BF16),
            jax.ShapeDtypeStruct((n_gates, s), F32),
            jax.ShapeDtypeStruct((blk, s), BF16),
            jax.ShapeDtypeStruct((s, blk), BF16),
            jax.ShapeDtypeStruct((DIFF_HEADS, 2, s), F32),
            jax.ShapeDtypeStruct((blk, s), BF16),
            jax.ShapeDtypeStruct((blk, s), BF16),
        ],
        scratch_shapes=[
            pltpu.VMEM((cw.shape[1], LANES), F32),
        ],
        compiler_params=pltpu.CompilerParams(
            dimension_semantics=("arbitrary",), vmem_limit_bytes=VMEM_LIMIT),
        name="proj",
    )(x2, w_norm, wt_a, wt_b, cos_a, sin_a, cos_b, sin_b, qw, kw, cw)


def _gdn_kernel(g_ref, ba_ref, alog_ref, dt_ref, gw_ref, x_ref, ob_ref, wo_ref,
                y_ref, st_ref):
    nqkv = 3 * GDN_HEADS * GDN_D
    n_chunks = g_ref.shape[1] // CHUNK
    n_a = GDN_HEADS * GDN_D

    @pl.when(pl.program_id(0) == 0)
    def _():
        st_ref[...] = jnp.zeros_like(st_ref)

    for lo in range(0, y_ref.shape[1], OUT_PIECE):
        _out_b_piece(x_ref, ob_ref, wo_ref, y_ref, n_a, lo)

    lane = lax.broadcasted_iota(jnp.int32, (CHUNK, CHUNK), 1)
    subl = lax.broadcasted_iota(jnp.int32, (CHUNK, CHUNK), 0)
    eye = (lane == subl).astype(F32)
    lane8 = lax.broadcasted_iota(jnp.int32, (8, CHUNK), 1)

    chains = []
    for c in range(n_chunks):
        lanes = slice(c * CHUNK, (c + 1) * CHUNK)
        y = g_ref[0:nqkv, lanes].astype(F32)

        ba = ba_ref[:, lanes]
        beta8 = _sigmoid(ba)
        sp = ba + dt_ref[...]
        softplus = jnp.maximum(sp, 0.0) + jnp.log(1.0 + jnp.exp(-jnp.abs(sp)))
        gc8 = (-LOG2E) * jnp.exp(alog_ref[...]) * softplus
        sh = 1
        while sh < CHUNK:
            gc8 = gc8 + jnp.where(lane8 >= sh, pltpu.roll(gc8, sh, axis=1), 0.0)
            sh *= 2

        for h in range(GDN_HEADS):
            qt = y[h * GDN_D:(h + 1) * GDN_D, :]
            kt = y[(GDN_HEADS + h) * GDN_D:(GDN_HEADS + h + 1) * GDN_D, :]
            vt = y[(2 * GDN_HEADS + h) * GDN_D:(2 * GDN_HEADS + h + 1) * GDN_D, :]
            qt = qt * lax.rsqrt(jnp.sum(qt * qt, axis=0, keepdims=True) + EPS) * (GDN_D ** -0.5)
            kt = kt * lax.rsqrt(jnp.sum(kt * kt, axis=0, keepdims=True) + EPS)
            beta = beta8[h:h + 1, :]
            gc = gc8[GDN_HEADS + h:GDN_HEADS + h + 1, :]
            g_row = jnp.broadcast_to(gc, (CHUNK, CHUNK))
            g_col = g_row.T
            g_last = g_col[CHUNK - 1:CHUNK, :]
            e_gc = jnp.exp2(gc)
            chains.append(dict(
                c=c, h=h, qt=qt, kt=kt, beta=beta, g_last=g_last, g_col=g_col,
                decay_t=jnp.exp2(jnp.where(lane >= subl, g_row - g_col, NEG_BIG)),
                k_nat=kt.T,
                qg=qt * e_gc,
                rhs=jnp.concatenate([vt * beta, kt * (beta * e_gc)], axis=0).astype(BF16)))

    for ch in chains:
        gram = jnp.dot(ch["k_nat"].astype(BF16),
                       jnp.concatenate([ch["kt"], ch["qt"]], axis=1).astype(BF16),
                       preferred_element_type=F32)
        ch["a_t"] = gram[:, CHUNK:] * ch["decay_t"]
        ch["n"] = jnp.where(lane > subl, gram[:, :CHUNK] * ch["decay_t"] * (-ch["beta"]), 0.0)

    for ch in chains:
        nb = ch["n"].astype(BF16)
        ch["q"] = eye + ch["n"]
        ch["n"] = jnp.dot(nb, nb, preferred_element_type=F32)
    m = 2
    while 2 * m < CHUNK:
        for ch in chains:
            nb = ch["n"].astype(BF16)
            both = jnp.dot(jnp.concatenate([ch["q"].astype(BF16), nb], axis=0), nb,
                           preferred_element_type=F32)
            ch["q"] = ch["q"] + both[:CHUNK]
            ch["n"] = both[CHUNK:]
        m *= 2
    for ch in chains:
        ch["q"] = ch["q"] + jnp.dot(ch["q"].astype(BF16), ch["n"].astype(BF16),
                                    preferred_element_type=F32)

    for ch in chains:
        uw = jnp.dot(ch["rhs"], ch["q"].astype(BF16), preferred_element_type=F32)
        ch["u_t"] = uw[:GDN_D, :]
        ch["r1"] = jnp.concatenate([uw[GDN_D:, :], ch["qg"]], axis=1).astype(BF16)
        k_dec = ch["k_nat"] * jnp.exp2(ch["g_last"] - ch["g_col"])
        ch["r2"] = jnp.concatenate([k_dec, ch["a_t"]], axis=1).astype(BF16)

    def out_a_piece(c, oa_chunk):
        rows = slice(c * CHUNK, (c + 1) * CHUNK)
        y_ref[rows, :] += lax.dot_general(oa_chunk, wo_ref[0:n_a, :], TN_DIMS,
                                          preferred_element_type=F32)

    states = [st_ref[h] for h in range(GDN_HEADS)]
    ready = None
    for c in range(n_chunks):
        lanes = slice(c * CHUNK, (c + 1) * CHUNK)
        row = chains[c * GDN_HEADS:(c + 1) * GDN_HEADS]
        x1 = [jnp.dot(states[h].astype(BF16), row[h]["r1"], preferred_element_type=F32)
              for h in range(GDN_HEADS)]
        if ready is not None:
            out_a_piece(*ready)
        x2 = [jnp.dot((row[h]["u_t"] - x1[h][:, :CHUNK]).astype(BF16), row[h]["r2"],
                      preferred_element_type=F32) for h in range(GDN_HEADS)]
        oa = []
        for h in range(GDN_HEADS):
            states[h] = states[h] * jnp.exp2(row[h]["g_last"]) + x2[h][:, :GDN_D]
            o_t = x1[h][:, CHUNK:] + x2[h][:, GDN_D:]
            o_n = o_t * lax.rsqrt(jnp.mean(o_t * o_t, axis=0, keepdims=True) + EPS) * gw_ref[...]
            gate = g_ref[nqkv + h * GDN_D:nqkv + (h + 1) * GDN_D, lanes].astype(F32)
            oa.append((o_n * gate).astype(BF16))
        ready = (c, jnp.concatenate(oa, axis=0))
    out_a_piece(*ready)
    for h in range(GDN_HEADS):
        st_ref[h] = states[h]


def _out_b_piece(x_ref, ob_ref, wo_ref, y_ref, n_a, lo):
    cols = slice(lo, lo + OUT_PIECE)
    y_ref[:, cols] = x_ref[:, cols] + jnp.dot(
        ob_ref[...], wo_ref[n_a:, cols], preferred_element_type=F32)


def _gdn_call(gdn_t, ba_t, alog8, dt8, gw, x2, ob, w_out):
    s, d = x2.shape
    tb = GDN_TB
    const = lambda shape: pl.BlockSpec(shape, lambda i: (0,) * len(shape))
    return pl.pallas_call(
        _gdn_kernel,
        grid=(s // tb,),
        in_specs=[
            pl.BlockSpec((gdn_t.shape[0], tb), lambda i: (0, i)),
            pl.BlockSpec((8, tb), lambda i: (0, i)),
            const(alog8.shape),
            const(dt8.shape),
            const(gw.shape),
            pl.BlockSpec((tb, d), lambda i: (i, 0)),
            pl.BlockSpec((tb, ob.shape[1]), lambda i: (i, 0)),
            const(w_out.shape),
        ],
        out_specs=pl.BlockSpec((tb, d), lambda i: (i, 0)),
        out_shape=jax.ShapeDtypeStruct((s, d), F32),
        scratch_shapes=[
            pltpu.VMEM((GDN_HEADS, GDN_D, GDN_D), F32),
        ],
        compiler_params=pltpu.CompilerParams(
            dimension_semantics=("arbitrary",), vmem_limit_bytes=VMEM_LIMIT),
        name="gdn",
    )(gdn_t, ba_t, alog8, dt8, gw, x2, ob, w_out)


def _attn_kernel(lam_ref, q_ref, k_ref, v_ref, z_ref, ks_ref, sw_ref, o_ref,
                 acc_ref, l_ref, m_ref, p_ref, *, lam_init):
    for sub in range(ATT_SUB):
        _attn_tile(sub, lam_ref, q_ref, k_ref, v_ref, z_ref, ks_ref, sw_ref, o_ref,
                   acc_ref.at[sub], l_ref.at[sub], m_ref, p_ref, lam_init)


def _attn_tile(sub, lam_ref, q_ref, k_ref, v_ref, z_ref, ks_ref, sw_ref, o_ref,
               acc_ref, l_ref, m_ref, p_ref, lam_init):
    tq, tk = ATT_TQ, ATT_TK
    dv = 2 * DIFF_D
    cols = slice(sub * tq, (sub + 1) * tq)
    i = pl.program_id(1) * ATT_SUB + sub
    qt = q_ref[:, cols]
    row = lax.broadcasted_iota(jnp.int32, qt.shape, 0)
    zero = jnp.zeros_like(qt)
    q_cat = jnp.concatenate(
        [jnp.where(row < DIFF_D, qt, zero), jnp.where(row >= DIFF_D, qt, zero)], axis=1)

    def scores(j):
        start = pl.multiple_of(j * tk, tk)
        return jnp.dot(k_ref[pl.ds(start, tk), :], q_cat, preferred_element_type=F32)

    def flush(j_pending, alpha):
        start = pl.multiple_of(j_pending * tk, tk)
        vt = v_ref[:, pl.ds(start, tk)]
        for c in range(2):
            upd = acc_ref[c] + jnp.dot(vt, p_ref[:, c * tq:(c + 1) * tq],
                                       preferred_element_type=F32)
            acc_ref[c] = upd if alpha is None else upd * alpha[:, c * tq:(c + 1) * tq]

    def exp_store(s_j, m):
        p = jnp.exp2(s_j if m is None else s_j - m)
        p_ref[...] = p.astype(BF16)
        return jnp.sum(p.reshape(tk // 8, 8, 2 * tq), axis=0)

    n_diag = tq // tk
    n_kv = i * n_diag

    def pending(j):
        return jnp.where(j == 0, n_kv + n_diag - 1, j - 1)

    qf = qt.astype(F32)
    qq = qf * qf
    qsq = jnp.concatenate([jnp.sum(qq[:DIFF_D], axis=0, keepdims=True),
                           jnp.sum(qq[DIFF_D:], axis=0, keepdims=True)], axis=1)
    kmax = jnp.max(ks_ref[0], axis=1, keepdims=True)
    kmax = jnp.concatenate([jnp.broadcast_to(kmax[0:1], (1, tq)),
                            jnp.broadcast_to(kmax[1:2], (1, tq))], axis=1)
    frozen_ok = jnp.max(jnp.sqrt(qsq * kmax)) * NORM_SLACK <= FROZEN_MAX_BOUND

    tri = (lax.broadcasted_iota(jnp.int32, (LANES, LANES), 0)
           <= lax.broadcasted_iota(jnp.int32, (LANES, LANES), 1))
    def cat(parts):
        parts = [t for t in parts if t.shape[1] > 0]
        return parts[0] if len(parts) == 1 else jnp.concatenate(parts, axis=1)

    def widen(t, lo, w, fill):
        pad = jnp.full((t.shape[0], lo), fill, t.dtype)
        return cat([pad, t[:, :w], pad, t[:, w:]])

    strips = []
    m0 = jnp.full((1, 2 * tq), NEG_BIG, F32)
    for a in range(tq // LANES):
        lo, w = a * LANES, tq - a * LANES
        start = pl.multiple_of(i * tq + lo, LANES)
        s_a = jnp.dot(k_ref[pl.ds(start, LANES), :], cat([q_cat[:, lo:tq], q_cat[:, tq + lo:]]),
                      preferred_element_type=F32)
        s_a = cat([jnp.where(tri, s_a[:, :LANES], NEG_BIG), s_a[:, LANES:w],
                   jnp.where(tri, s_a[:, w:w + LANES], NEG_BIG), s_a[:, w + LANES:]])
        strips.append(s_a)
        m0 = jnp.maximum(m0, widen(jnp.max(s_a, axis=0, keepdims=True), lo, w, NEG_BIG))
    m0 = jnp.where(frozen_ok, 0.0, m0)
    m_ref[...] = m0
    acc_ref[...] = jnp.zeros_like(acc_ref)
    l0 = jnp.zeros((8, 2 * tq), F32)
    per_tile = tk // LANES
    for a, s_a in enumerate(strips):
        lo, w = a * LANES, tq - a * LANES
        if a % per_tile == 0 and a > 0:
            flush(n_kv + a // per_tile - 1, None)
        p = jnp.exp2(s_a - cat([m0[:, lo:tq], m0[:, tq + lo:]]))
        r0 = (a % per_tile) * LANES
        p_ref[r0:r0 + LANES, :] = widen(p.astype(BF16), lo, w, 0.0)
        l0 = l0 + widen(jnp.sum(p.reshape(LANES // 8, 8, 2 * w), axis=0), lo, w, 0.0)
    l_ref[...] = l0

    def frozen_step(j, j_pending):
        s_j = scores(j)
        flush(j_pending, None)
        l_ref[...] += exp_store(s_j, None)

    def frozen_steps(j0, n):
        for u in range(n):
            frozen_step(j0 + u, pending(j0) if u == 0 else j0 + u - 1)

    def frozen_group(t, carry):
        frozen_steps(ATT_UNROLL * t, ATT_UNROLL)
        return carry

    def online_body(j, carry):
        s_j = scores(j)
        m_old = m_ref[...]
        m_new = jnp.maximum(m_old, jnp.max(s_j, axis=0, keepdims=True))
        alpha = jnp.exp2(m_old - m_new)
        flush(pending(j), alpha)
        m_ref[...] = m_new
        l_ref[...] = l_ref[...] * alpha + exp_store(s_j, m_new)
        return carry

    @pl.when(frozen_ok)
    def _():
        groups = lax.shift_right_logical(n_kv, ATT_UNROLL.bit_length() - 1)
        lax.fori_loop(0, groups, frozen_group, 0)
        j0 = groups * ATT_UNROLL
        n = ATT_UNROLL // 2
        while n >= n_diag:
            @pl.when((n_kv & n) != 0)
            def _(j0=j0, n=n):
                frozen_steps(j0, n)
            j0 = j0 + (n_kv & n)
            n //= 2

    @pl.when(jnp.logical_not(frozen_ok))
    def _():
        lax.fori_loop(0, n_kv, online_body, 0)

    flush(pending(n_kv), None)

    lam = lam_ref[0, 0]
    l = jnp.sum(l_ref[...], axis=0, keepdims=True)
    inv = 1.0 / l
    o_t = acc_ref[0] * inv[:, :tq] - acc_ref[1] * (lam * inv[:, tq:])
    o_n = o_t * lax.rsqrt(jnp.mean(o_t * o_t, axis=0, keepdims=True) + SUBLN_EPS) * sw_ref[...]
    o_n = o_n * (1.0 - lam_init)
    o_ref[cols, :] = (o_n * z_ref[:, cols].astype(F32)).T.astype(o_ref.dtype)


def _attn_call(lam, q_t, k_nat, v_t, z_t, ksq, sw, lam_init):
    dv = 2 * DIFF_D
    s = k_nat.shape[0]
    h = k_nat.shape[1] // dv
    tq = ATT_TQ
    tg = ATT_SUB * tq
    assert ATT_TQ % ATT_TK == 0 and ATT_UNROLL >= ATT_TQ // ATT_TK
    return pl.pallas_call(
        functools.partial(_attn_kernel, lam_init=lam_init),
        grid=(h, s // tg),
        in_specs=[
            pl.BlockSpec(memory_space=pltpu.SMEM),
            pl.BlockSpec((dv, tg), lambda hh, i: (hh, i)),
            pl.BlockSpec((s, dv), lambda hh, i: (0, hh)),
            pl.BlockSpec((dv, s), lambda hh, i: (hh, 0)),
            pl.BlockSpec((dv, tg), lambda hh, i: (hh, i)),
            pl.BlockSpec((1, 2, s), lambda hh, i: (hh, 0, 0)),
            pl.BlockSpec((dv, tq), lambda hh, i: (0, 0)),
        ],
        out_specs=pl.BlockSpec((tg, dv), lambda hh, i: (i, hh)),
        out_shape=jax.ShapeDtypeStruct((s, h * dv), BF16),
        scratch_shapes=[
            pltpu.VMEM((ATT_SUB, 2, dv, tq), F32),
            pltpu.VMEM((ATT_SUB, 8, 2 * tq), F32),
            pltpu.VMEM((1, 2 * tq), F32),
            pltpu.VMEM((ATT_TK, 2 * tq), BF16),
        ],
        compiler_params=pltpu.CompilerParams(
            dimension_semantics=("arbitrary", "arbitrary"), vmem_limit_bytes=VMEM_LIMIT),
        name="attn",
    )(lam, q_t, k_nat, v_t, z_t, ksq, sw)


def _layer(l, x2, rope, w_norm, w_in, conv_w, a_log, dt_bias, gdn_norm_w, q_norm_w,
           k_norm_w, lambda_q1, lambda_k1, lambda_q2, lambda_k2, subln_w, w_out):
    s, d = x2.shape
    nqk = GDN_HEADS * GDN_D
    ba_lo = 4 * nqk
    ba_hi = ba_lo + 2 * GDN_HEADS
    wt_a = w_in[:, :ba_hi].T.astype(BF16)
    wt_b = w_in[:, ba_hi:].T.astype(BF16)
    qw = jnp.broadcast_to(q_norm_w[:, None], (DIFF_D, PROJ_TM)).astype(F32)
    kw = jnp.broadcast_to(k_norm_w[:, None], (DIFF_D, PROJ_TM)).astype(F32)

    cw = jnp.broadcast_to(conv_w.astype(F32)[:, :, None], (CONV_K, 3 * nqk, LANES))
    gdn_t, ba_t, qb_t, kb, ksq, vb_t, zb_t = _proj_call(
        x2, w_norm[None, :].astype(F32), wt_a, wt_b, rope, qw, kw, cw)

    zeros4 = jnp.zeros((GDN_HEADS, LANES), F32)
    alog8 = jnp.concatenate([zeros4, jnp.broadcast_to(a_log.astype(F32)[:, None], (GDN_HEADS, LANES))], 0)
    dt8 = jnp.concatenate([zeros4, jnp.broadcast_to(dt_bias.astype(F32)[:, None], (GDN_HEADS, LANES))], 0)
    gw = jnp.broadcast_to(gdn_norm_w.astype(F32)[:, None], (GDN_D, LANES))

    lam_init = 0.8 - 0.6 * math.exp(-0.3 * l)
    lam = (jnp.exp(jnp.sum(lambda_q1.astype(F32) * lambda_k1.astype(F32)))
           - jnp.exp(jnp.sum(lambda_q2.astype(F32) * lambda_k2.astype(F32))) + lam_init)
    dv = 2 * DIFF_D
    sw = jnp.broadcast_to(subln_w.astype(F32)[:, None], (dv, ATT_TQ))
    ob = _attn_call(lam.reshape(1, 1).astype(F32), qb_t, kb, vb_t, zb_t, ksq, sw, lam_init)

    return _gdn_call(gdn_t, ba_t, alog8, dt8, gw, x2, ob, w_out.astype(BF16))


def kernel(x, w_norm, w_in, conv_w, a_log, dt_bias, gdn_norm_w, q_norm_w, k_norm_w,
           lambda_q1, lambda_k1, lambda_q2, lambda_k2, subln_w, w_out):
    b, s, d = x.shape
    assert b == 1
    inv_freq = ROPE_THETA ** (-jnp.arange(0, DIFF_D, 2, dtype=jnp.float32) / DIFF_D)
    per_tile = PROJ_TM // LANES
    coarse = jnp.arange(0, s, LANES, dtype=jnp.float32)[:, None] * inv_freq[None, :]
    coarse = coarse.reshape(s // PROJ_TM, per_tile, DIFF_D // 2).transpose(0, 2, 1)
    fine = inv_freq[:, None] * jnp.arange(LANES, dtype=jnp.float32)[None, :]
    rope = (jnp.cos(coarse), jnp.sin(coarse), jnp.cos(fine), jnp.sin(fine))
    x2 = x[0]
    for l in range(w_norm.shape[0]):
        x2 = _layer(l, x2, rope, w_norm[l], w_in[l], conv_w[l], a_log[l], dt_bias[l],
                    gdn_norm_w[l], q_norm_w[l], k_norm_w[l], lambda_q1[l], lambda_k1[l],
                    lambda_q2[l], lambda_k2[l], subln_w[l], w_out[l])
    return x2[None]
```
